```python
import math
import jax
import jax.numpy as jnp
from jax import lax
import numpy as np

D_MODEL = 2048
BATCH = 32
SEQ = 256
DEPTH = 4
DEC_BATCH = 8
DEC_SEQ = 2048
PAST_LEN = 512

GRID_W = 64
BRANCH_W = D_MODEL // 4
HG_DK = 128
HG_H = BRANCH_W // HG_DK
GD_DK = 128
GD_H = BRANCH_W // GD_DK
RW_DK = 64
RW_H = BRANCH_W // RW_DK
RW_LORA_W = D_MODEL // 32
RW_LORA_A = D_MODEL // 32
RW_LORA_G = D_MODEL // 16
D_FF = 5632
N_MOD = 9
HG_CHUNK = 32
GD_CHUNK = 64
CONV_K = 3
NORM_EPS = 1e-6
RW_GN_EPS = 64e-5
GATE_FLOOR = 1e-30
IN_SPLITS = (
    ('hg_q', BRANCH_W), ('hg_i', BRANCH_W), ('hg_f', 2 * BRANCH_W), ('hg_g', BRANCH_W),
    ('gd_qkv', 3 * BRANCH_W), ('gd_z', BRANCH_W), ('gd_a', 2 * GD_H), ('gd_b', 2 * GD_H),
    ('rw_rkv', 3 * BRANCH_W), ('rw_wd', 2 * RW_LORA_W), ('rw_ad', 2 * RW_LORA_A), ('rw_gd', RW_LORA_G),
    ('merge', 3 * D_MODEL),
)
IN_WIDTH = sum(size for _, size in IN_SPLITS)

kernel_name = 'hybrid_bidir_flow_trunk_step'


def _rms_norm(x, g):
    xf = x.astype(jnp.float32)
    y = xf * lax.rsqrt(jnp.mean(xf * xf, axis=-1, keepdims=True) + NORM_EPS)
    return (y * g.astype(jnp.float32)).astype(x.dtype)


def _l2_normalize(x):
    xf = x.astype(jnp.float32)
    return xf * lax.rsqrt(jnp.sum(xf * xf, axis=-1, keepdims=True) + NORM_EPS)


def _group_norm(x, g, b):
    xf = x.astype(jnp.float32)
    mu = jnp.mean(xf, axis=-1, keepdims=True)
    var = jnp.mean(jnp.square(xf - mu), axis=-1, keepdims=True)
    y = (xf - mu) * lax.rsqrt(var + RW_GN_EPS)
    return y * g.reshape(x.shape[-2:]).astype(jnp.float32) + b.reshape(x.shape[-2:]).astype(jnp.float32)


def _heads(t, d):
    return t.reshape(t.shape[:-1] + (t.shape[-1] // d, d))


def _split_cols(proj):
    cols, start = {}, 0
    for name, size in IN_SPLITS:
        cols[name] = proj[..., start:start + size]
        start += size
    return cols


def _swiglu(h, w_up, w_down):
    a, b = jnp.split(h @ w_up, 2, axis=-1)
    return (jax.nn.silu(a) * b) @ w_down


def _masked_decay(rel, mask):
    return jnp.where(mask, jnp.exp(jnp.minimum(rel, 0.0)), 0.0)


def _short_conv(x, w, on_grid):
    B, L, C = x.shape
    w = w.astype(x.dtype)
    if on_grid:
        rows = L // GRID_W
        y = lax.conv_general_dilated(x.reshape(B, rows, GRID_W, C), w[:, :, None, :], (1, 1), 'SAME',
                                     dimension_numbers=('NHWC', 'HWIO', 'NHWC'), feature_group_count=C)
        return y.reshape(B, L, C)
    return lax.conv_general_dilated(x, w[CONV_K // 2][:, None, :], (1,), 'SAME',
                                    dimension_numbers=('NWC', 'WIO', 'NWC'), feature_group_count=C)


def _gla_chunked(q, k, v, log_f, s0):
    f32 = jnp.float32
    B, L, H, dk = q.shape
    dv = v.shape[-1]
    n = L // HG_CHUNK

    def blocks(t):
        return t.astype(f32).reshape(B, n, HG_CHUNK, H, t.shape[-1]).transpose(1, 0, 3, 2, 4)

    qb, kb, vb = blocks(q), blocks(k), blocks(v)
    cum = jnp.cumsum(blocks(log_f), axis=3)
    incl = jnp.tril(jnp.ones((HG_CHUNK, HG_CHUNK), bool))

    def step(s, inp):
        qc, kc, vc, bc = inp
        last = bc[:, :, -1:, :]
        inter = jnp.einsum('bhtd,bhde->bhte', qc * jnp.exp(bc), s)
        rel = bc[:, :, :, None, :] - bc[:, :, None, :, :]
        decay = _masked_decay(rel, incl[:, :, None])
        scores = jnp.sum(qc[:, :, :, None, :] * kc[:, :, None, :, :] * decay, axis=-1)
        intra = jnp.einsum('bhts,bhse->bhte', scores, vc)
        s_new = jnp.exp(last[:, :, 0, :])[..., None] * s + jnp.einsum('bhsd,bhse->bhde', kc * jnp.exp(last - bc), vc)
        return s_new, inter + intra

    s_fin, ob = lax.scan(step, s0.astype(f32), (qb, kb, vb, cum))
    return ob.transpose(1, 0, 3, 2, 4).reshape(B, L, H, dv), s_fin


def _gated_delta_chunked(q, k, v, log_g, beta, s0):
    f32 = jnp.float32
    B, L, H, dk = q.shape
    dv = v.shape[-1]
    n = L // GD_CHUNK

    def blocks(t):
        return t.astype(f32).reshape(B, n, GD_CHUNK, H, t.shape[-1]).transpose(1, 0, 3, 2, 4)

    def scal(t):
        return t.astype(f32).reshape(B, n, GD_CHUNK, H).transpose(1, 0, 3, 2)

    qb, kb, vb = blocks(q), blocks(k), blocks(v)
    bb = scal(beta)
    G = jnp.cumsum(scal(log_g), axis=-1)
    rel = G[..., :, None] - G[..., None, :]
    strict = jnp.tril(jnp.ones((GD_CHUNK, GD_CHUNK), bool), -1)
    incl = jnp.tril(jnp.ones((GD_CHUNK, GD_CHUNK), bool))
    dec_strict = _masked_decay(rel, strict)
    dec_incl = _masked_decay(rel, incl)
    a_mat = jnp.eye(GD_CHUNK, dtype=f32) + bb[..., :, None] * jnp.einsum('nbhid,nbhjd->nbhij', kb, kb) * dec_strict
    w = lax.linalg.triangular_solve(a_mat, (bb * jnp.exp(G))[..., None] * kb,
                                    left_side=True, lower=True, unit_diagonal=True)
    u = lax.linalg.triangular_solve(a_mat, bb[..., None] * vb, left_side=True, lower=True, unit_diagonal=True)
    p_mat = jnp.einsum('nbhid,nbhjd->nbhij', qb, kb) * dec_incl
    q_dec = qb * jnp.exp(G)[..., None]
    k_dec = kb * jnp.exp(G[..., -1:] - G)[..., None]
    g_last = jnp.exp(G[..., -1])

    def step(s, inp):
        q_c, k_c, w_c, u_c, p_c, gl = inp
        v_new = u_c - jnp.einsum('bhcd,bhde->bhce', w_c, s)
        o = jnp.einsum('bhcd,bhde->bhce', q_c, s) + jnp.einsum('bhij,bhje->bhie', p_c, v_new)
        s = gl[..., None, None] * s + jnp.einsum('bhcd,bhce->bhde', k_c, v_new)
        return s, o

    s_fin, ob = lax.scan(step, s0.astype(f32), (q_dec, k_dec, w, u, p_mat, g_last))
    return ob.transpose(1, 0, 3, 2, 4).reshape(B, L, H, dv), s_fin


def _rwkv7_scan(r, decay, k, v, a, b, s0):
    f32 = jnp.float32

    def step(s, inp):
        r_t, w_t, k_t, v_t, a_t, b_t = inp
        sa = jnp.einsum('bhij,bhj->bhi', s, a_t)
        s = s * w_t[:, :, None, :] + sa[..., None] * b_t[:, :, None, :] + v_t[..., None] * k_t[:, :, None, :]
        return s, jnp.einsum('bhij,bhj->bhi', s, r_t)

    xs = tuple(t.astype(f32).transpose(1, 0, 2, 3) for t in (r, decay, k, v, a, b))
    s_fin, y = lax.scan(step, s0.astype(f32), xs)
    return y.transpose(1, 0, 2, 3), s_fin


def _bidir(scan_fn, fwd_args, bwd_args, s0):
    o_f, s_f = scan_fn(*fwd_args, s0[:, 0])
    o_b, s_b = scan_fn(*[jnp.flip(t, axis=1) for t in bwd_args], s0[:, 1])
    return o_f + jnp.flip(o_b, axis=1), jnp.stack([s_f, s_b], axis=1)


def _token_mix(h, p, s0, on_grid):
    f32 = jnp.float32
    B, L, _ = h.shape
    cols = _split_cols(h @ p['mix_in'])

    hg_q = _heads(jax.nn.silu(cols['hg_q'].astype(f32)) * HG_DK ** -0.5, HG_DK)
    hg_v = _heads(cols['hg_i'].astype(f32), HG_DK)
    f_logit = cols['hg_f'].astype(f32).reshape(B, L, 2, BRANCH_W)
    lb = p['hgrn_lb']
    f_gate = lb + (1.0 - lb) * jax.nn.sigmoid(f_logit)
    log_f = jnp.log(jnp.maximum(f_gate, GATE_FLOOR))
    hg_k = (1.0 - lb) * jax.nn.sigmoid(-f_logit)
    o_hg, s_hg = _bidir(_gla_chunked,
                        (hg_q, _heads(hg_k[:, :, 0], HG_DK), hg_v, _heads(log_f[:, :, 0], HG_DK)),
                        (hg_q, _heads(hg_k[:, :, 1], HG_DK), hg_v, _heads(log_f[:, :, 1], HG_DK)),
                        s0[0])
    o_hg = _rms_norm(o_hg, p['hgrn_norm'].reshape(HG_H, HG_DK)).reshape(B, L, BRANCH_W)
    o_hg = o_hg * jax.nn.silu(cols['hg_g'].astype(f32))

    qkv = jax.nn.silu(_short_conv(cols['gd_qkv'], p['gdn_conv'], on_grid).astype(f32))
    gq, gk, gv = jnp.split(qkv, 3, axis=-1)
    gq = _l2_normalize(_heads(gq, GD_DK)) * GD_DK ** -0.5
    gk = _l2_normalize(_heads(gk, GD_DK))
    gv = _heads(gv, GD_DK)
    a_in = cols['gd_a'].astype(f32).reshape(B, L, 2, GD_H)
    beta = jax.nn.sigmoid(cols['gd_b'].astype(f32).reshape(B, L, 2, GD_H))
    log_g = -jnp.exp(p['gdn_a_log'].astype(f32)) * jax.nn.softplus(a_in + p['gdn_dt_bias'].astype(f32))
    o_gd, s_gd = _bidir(_gated_delta_chunked,
                        (gq, gk, gv, log_g[:, :, 0], beta[:, :, 0]),
                        (gq, gk, gv, log_g[:, :, 1], beta[:, :, 1]),
                        s0[1])
    o_gd = _rms_norm(o_gd, p['gdn_norm']).reshape(B, L, BRANCH_W) * jax.nn.silu(cols['gd_z'].astype(f32))

    rr, rk, rv = jnp.split(cols['rw_rkv'].astype(f32), 3, axis=-1)
    w_low = jnp.tanh(cols['rw_wd'].astype(f32).reshape(B, L, 2, RW_LORA_W))
    a_low = cols['rw_ad'].astype(f32).reshape(B, L, 2, RW_LORA_A)
    log_w = -jax.nn.softplus(-(p['rwkv_w0'] + jnp.einsum('bldr,drc->bldc', w_low, p['rwkv_w2']))) - 0.5
    decay = jnp.exp(-jnp.exp(log_w)).reshape(B, L, 2, RW_H, RW_DK)
    icl = jax.nn.sigmoid(p['rwkv_a0'] + jnp.einsum('bldr,drc->bldc', a_low, p['rwkv_a2']))
    out_gate = jax.nn.sigmoid(cols['rw_gd'].astype(f32)) @ p['rwkv_g2'].astype(f32)
    kk = _l2_normalize(_heads(rk * p['rwkv_kk'], RW_DK))
    k_dir = (rk[:, :, None, :] * (1.0 + (icl - 1.0) * p['rwkv_ka'])).reshape(B, L, 2, RW_H, RW_DK)
    icl_h = icl.reshape(B, L, 2, RW_H, RW_DK)
    r_h, v_h = _heads(rr, RW_DK), _heads(rv, RW_DK)
    o_rw, s_rw = _bidir(_rwkv7_scan,
                        (r_h, decay[:, :, 0], k_dir[:, :, 0], v_h, -kk, kk * icl_h[:, :, 0]),
                        (r_h, decay[:, :, 1], k_dir[:, :, 1], v_h, -kk, kk * icl_h[:, :, 1]),
                        s0[2])
    bonus = jnp.sum(r_h[:, :, None] * k_dir * p['rwkv_rk'], axis=(2, 4))[..., None] * v_h
    o_rw = (_group_norm(o_rw, p['rwkv_ln_g'], p['rwkv_ln_b']) + bonus).reshape(B, L, BRANCH_W) * out_gate

    gates = jax.nn.sigmoid(cols['merge'].astype(f32)).reshape(B, L, 3, D_MODEL)
    wb = p['mix_branch']
    merged = (gates[:, :, 0] * (o_hg.astype(h.dtype) @ wb[0])
              + gates[:, :, 1] * (o_gd.astype(h.dtype) @ wb[1])
              + gates[:, :, 2] * (o_rw.astype(h.dtype) @ wb[2]))
    return merged.astype(h.dtype) @ p['mix_out'], (s_hg, s_gd, s_rw)


def _layer(x, cond, p, s0, on_grid):
    mod = (jax.nn.silu(cond) @ p['mod_w'] + p['mod_b'])[:, None, :]
    sh1, sc1, g1, sh2, sc2, g2, sh3, sc3, g3 = jnp.split(mod, N_MOD, axis=-1)
    h = _rms_norm(x, p['norm_g'][0]) * (1.0 + sc1) + sh1
    x = x + 0.5 * g1 * _swiglu(h, p['ffn_up'][0], p['ffn_down'][0])
    h = _rms_norm(x, p['norm_g'][1]) * (1.0 + sc2) + sh2
    mixed, states = _token_mix(h, p, s0, on_grid)
    x = x + g2 * mixed
    h = _rms_norm(x, p['norm_g'][2]) * (1.0 + sc3) + sh3
    x = x + 0.5 * g3 * _swiglu(h, p['ffn_up'][1], p['ffn_down'][1])
    return x, states


def setup_inputs(seed: int = 0) -> dict:
    key = jax.random.key(seed)
    ks = iter(jax.random.split(key, 40))
    f32 = jnp.float32

    def nrm(shape, scale):
        return scale * jax.random.normal(next(ks), shape, f32)

    def unif(shape, lo, hi):
        return jax.random.uniform(next(ks), shape, f32, lo, hi)

    D = D_MODEL
    x_prompt = nrm((BATCH, SEQ, D), 1.0)
    x_sample = nrm((DEC_BATCH, DEC_SEQ, D), 1.0)
    state_hgrn = nrm((DEC_BATCH, DEPTH, 2, HG_H, HG_DK, HG_DK), 0.3)
    state_gdn = nrm((DEC_BATCH, DEPTH, 2, GD_H, GD_DK, GD_DK), 0.1)
    state_rwkv = nrm((DEC_BATCH, DEPTH, 2, RW_H, RW_DK, RW_DK), 0.3)
    c = nrm((DEC_BATCH, D), 1.0)
    c_ctx = nrm((D,), 1.0)
    mod_w = nrm((DEPTH, D, N_MOD * D), 0.5 * D ** -0.5)
    mod_b = nrm((DEPTH, N_MOD * D), 0.02)
    norm_g = 1.0 + nrm((DEPTH, 3, D), 0.02)
    ffn_up = nrm((DEPTH, 2, D, 2 * D_FF), D ** -0.5)
    ffn_down = nrm((DEPTH, 2, D_FF, D), D_FF ** -0.5)
    mix_in = nrm((DEPTH, D, IN_WIDTH), D ** -0.5)
    hgrn_lb = nrm((DEPTH, 2, BRANCH_W), 1.0)
    hgrn_norm = 1.0 + nrm((DEPTH, BRANCH_W), 0.02)
    gdn_conv = nrm((DEPTH, CONV_K, CONV_K, 3 * BRANCH_W), 1.0 / CONV_K)
    gdn_a_log = jnp.log(unif((DEPTH, 2, GD_H), 1.0, 16.0))
    dt = jnp.exp(unif((DEPTH, 2, GD_H), math.log(1e-3), math.log(1e-1)))
    gdn_dt_bias = dt + jnp.log(-jnp.expm1(-dt))
    gdn_norm = 1.0 + nrm((DEPTH, GD_DK), 0.02)
    rwkv_w0 = unif((DEPTH, 2, BRANCH_W), -6.0, 1.0)
    rwkv_w2 = nrm((DEPTH, 2, RW_LORA_W, BRANCH_W), 0.5 * RW_LORA_W ** -0.5)
    rwkv_a0 = nrm((DEPTH, 2, BRANCH_W), 0.3)
    rwkv_a2 = nrm((DEPTH, 2, RW_LORA_A, BRANCH_W), RW_LORA_A ** -0.5)
    rwkv_g2 = nrm((DEPTH, RW_LORA_G, BRANCH_W), RW_LORA_G ** -0.5)
    rwkv_kk = 0.85 + nrm((DEPTH, BRANCH_W), 0.05)
    rwkv_ka = 1.0 + nrm((DEPTH, BRANCH_W), 0.05)
    rwkv_rk = nrm((DEPTH, RW_H, RW_DK), 0.1)
    rwkv_ln_g = 1.0 + nrm((DEPTH, BRANCH_W), 0.02)
    rwkv_ln_b = nrm((DEPTH, BRANCH_W), 0.02)
    mix_branch = nrm((DEPTH, 3, BRANCH_W, D), BRANCH_W ** -0.5)
    mix_out = nrm((DEPTH, D, D), D ** -0.5)
    final_norm = 1.0 + nrm((D,), 0.02)
    return {'x_prompt': x_prompt, 'x_sample': x_sample, 'state_hgrn': state_hgrn, 'state_gdn': state_gdn,
            'state_rwkv': state_rwkv, 'c': c, 'c_ctx': c_ctx, 'mod_w': mod_w, 'mod_b': mod_b, 'norm_g': norm_g,
            'ffn_up': ffn_up, 'ffn_down': ffn_down, 'mix_in': mix_in, 'hgrn_lb': hgrn_lb, 'hgrn_norm': hgrn_norm,
            'gdn_conv': gdn_conv, 'gdn_a_log': gdn_a_log, 'gdn_dt_bias': gdn_dt_bias, 'gdn_norm': gdn_norm,
            'rwkv_w0': rwkv_w0, 'rwkv_w2': rwkv_w2, 'rwkv_a0': rwkv_a0, 'rwkv_a2': rwkv_a2, 'rwkv_g2': rwkv_g2,
            'rwkv_kk': rwkv_kk, 'rwkv_ka': rwkv_ka, 'rwkv_rk': rwkv_rk, 'rwkv_ln_g': rwkv_ln_g,
            'rwkv_ln_b': rwkv_ln_b, 'mix_branch': mix_branch, 'mix_out': mix_out, 'final_norm': final_norm}


def reference(x_prompt, x_sample, state_hgrn, state_gdn, state_rwkv, c, c_ctx, mod_w, mod_b, norm_g,
              ffn_up, ffn_down, mix_in, hgrn_lb, hgrn_norm, gdn_conv, gdn_a_log, gdn_dt_bias, gdn_norm,
              rwkv_w0, rwkv_w2, rwkv_a0, rwkv_a2, rwkv_g2, rwkv_kk, rwkv_ka, rwkv_rk, rwkv_ln_g, rwkv_ln_b,
              mix_branch, mix_out, final_norm):
    f32 = jnp.float32
    lb_p = jax.nn.softmax(hgrn_lb.astype(f32), axis=0)
    lb_all = jnp.cumsum(lb_p, axis=0) - lb_p[0:1]
    bp = x_prompt.shape[0]
    ctx_s0 = (jnp.zeros((bp, 2, HG_H, HG_DK, HG_DK), f32),
              jnp.zeros((bp, 2, GD_H, GD_DK, GD_DK), f32),
              jnp.zeros((bp, 2, RW_H, RW_DK, RW_DK), f32))
    xp, xs = x_prompt, x_sample
    new_hg, new_gd, new_rw = [], [], []
    for l in range(DEPTH):
        p = {'mod_w': mod_w[l], 'mod_b': mod_b[l], 'norm_g': norm_g[l], 'ffn_up': ffn_up[l],
             'ffn_down': ffn_down[l], 'mix_in': mix_in[l], 'hgrn_lb': lb_all[l], 'hgrn_norm': hgrn_norm[l],
             'gdn_conv': gdn_conv[l], 'gdn_a_log': gdn_a_log[l], 'gdn_dt_bias': gdn_dt_bias[l],
             'gdn_norm': gdn_norm[l], 'rwkv_w0': rwkv_w0[l], 'rwkv_w2': rwkv_w2[l], 'rwkv_a0': rwkv_a0[l],
             'rwkv_a2': rwkv_a2[l], 'rwkv_g2': rwkv_g2[l], 'rwkv_kk': rwkv_kk[l], 'rwkv_ka': rwkv_ka[l],
             'rwkv_rk': rwkv_rk[l], 'rwkv_ln_g': rwkv_ln_g[l], 'rwkv_ln_b': rwkv_ln_b[l],
             'mix_branch': mix_branch[l], 'mix_out': mix_out[l]}
        xp, (s_hg, s_gd, s_rw) = _layer(xp, c_ctx[None, :], p, ctx_s0, False)
        xs, _ = _layer(xs, c, p, (state_hgrn[:, l], state_gdn[:, l], state_rwkv[:, l]), True)
        new_hg.append(s_hg)
        new_gd.append(s_gd)
        new_rw.append(s_rw)
    y_prompt = _rms_norm(xp, final_norm)
    y_sample = _rms_norm(xs, final_norm)
    return (y_prompt, y_sample, jnp.stack(new_hg, axis=1), jnp.stack(new_gd, axis=1), jnp.stack(new_rw, axis=1))
```

```python
import functools

import numpy as np
import jax
import jax.numpy as jnp
from jax import lax
from jax.experimental import pallas as pl
from jax.experimental.pallas import tpu as pltpu

f32 = jnp.float32
bf16 = jnp.bfloat16

LANES = 128
SUBLANES = 8
VMEM_LIMIT = 56 * 1024 * 1024

HG_DK = 128
GD_DK = 128
RW_DK = 64
N_MOD = 9
GRID_W = 64
CONV_K = 3
NORM_EPS = 1e-6
RW_GN_EPS = 64e-5
GATE_FLOOR = 1e-30
N_COND_PAD = 16

HG_C = 64
GD_C = 128
RW_C = 64
SUB = 8
INV_PASSES = 3
MISC_W = 4 * LANES


def _sigmoid(x):
    return 1.0 / (1.0 + jnp.exp(-x))


def _silu(x):
    return x * _sigmoid(x)


def _softplus(x):
    return jnp.maximum(x, 0.0) + jnp.log(1.0 + jnp.exp(-jnp.abs(x)))


def _dot(a, b):
    return jnp.dot(a.astype(bf16), b.astype(bf16), preferred_element_type=f32)


def _dot_nt(a, b):
    return lax.dot_general(a.astype(bf16), b.astype(bf16), (((1,), (1,)), ((), ())),
                           preferred_element_type=f32)


def _dot_tn(a, b):
    return lax.dot_general(a.astype(bf16), b.astype(bf16), (((0,), (0,)), ((), ())),
                           preferred_element_type=f32)


def _split3(x):
    hi = x.astype(bf16)
    r = x - hi.astype(f32)
    mid = r.astype(bf16)
    lo = (r - mid.astype(f32)).astype(bf16)
    return hi, mid, lo


def _ldot01(m, x):
    hi, mid, lo = _split3(x)
    m = m.astype(bf16)
    d = functools.partial(jnp.dot, preferred_element_type=f32)
    return d(m, hi) + d(m, mid) + d(m, lo)


def _rdot01(x, m):
    hi, mid, lo = _split3(x)
    m = m.astype(bf16)
    d = functools.partial(jnp.dot, preferred_element_type=f32)
    return d(hi, m) + d(mid, m) + d(lo, m)


def _dotp(a, b):
    if INV_PASSES == 1:
        return _dot(a, b)
    ah = a.astype(bf16)
    al = (a - ah.astype(f32)).astype(bf16)
    bh = b.astype(bf16)
    bl = (b - bh.astype(f32)).astype(bf16)
    d = functools.partial(jnp.dot, preferred_element_type=f32)
    return d(ah, bh) + d(ah, bl) + d(al, bh)


def _tri_inverse(n, eye, mdiag, moffs):
    nd = n * mdiag
    n2 = _dotp(nd, nd)
    n4 = _dotp(n2, n2)
    x = eye - nd
    x = x + _dotp(x, n2)
    x = x + _dotp(x, n4)
    for m in moffs:
        x = x - _dotp(_dotp(x, n * m), x)
    return x


def _normmod(x, g, scale, shift):
    ms = jnp.mean(x * x, axis=-1, keepdims=True)
    return (x * lax.rsqrt(ms + NORM_EPS) * g) * (1.0 + scale) + shift


def _tri_np(c, rev):
    i = np.arange(c)
    return (i[None, :] >= i[:, None]) if rev else (i[None, :] <= i[:, None])


def _inverse_masks_np(n, c):
    i = np.arange(n)
    same = (i[:, None] // c) == (i[None, :] // c)
    eye = np.eye(n, dtype=bool)
    mdiag = (i[:, None] // SUB) == (i[None, :] // SUB)
    out = [eye, mdiag]
    m = SUB
    while m < c:
        out.append(same & ((i[:, None] // (2 * m)) == (i[None, :] // (2 * m)))
                   & ((i[:, None] // m) != (i[None, :] // m)))
        m *= 2
    return out


def _causal_np(n, c, rev):
    i = np.arange(n)
    same = (i[:, None] // c) == (i[None, :] // c)
    il, jl = (i % c)[:, None], (i % c)[None, :]
    strict = same & ((jl > il) if rev else (jl < il))
    incl = same & ((jl >= il) if rev else (jl <= il))
    return strict, incl


def _hgrn_tables_np(c):
    levels = []
    m = SUB
    while m < c:
        levels.append(m)
        m *= 2
    stacks, pmasks = [], []
    t = np.arange(c)
    for rev in (False, True):
        mats = [_tri_np(c, rev)]
        pms = []
        for m in levels:
            pair = t // (2 * m)
            half = (t // m) % 2
            mat = np.zeros((c, c), bool)
            tau = t[None, :]
            if not rev:
                bd = (pair * 2 * m + m - 1)[:, None]
                qside = (half == 1)[:, None]
                mat = np.where(qside, (tau > bd) & (tau <= t[:, None]), (tau > t[:, None]) & (tau <= bd))
                pm = (pair[:, None] == pair[None, :]) & (half == 1)[:, None] & (half == 0)[None, :]
            else:
                bd = (pair * 2 * m + m)[:, None]
                qside = (half == 0)[:, None]
                mat = np.where(qside, (tau >= t[:, None]) & (tau < bd), (tau >= bd) & (tau < t[:, None]))
                pm = (pair[:, None] == pair[None, :]) & (half == 0)[:, None] & (half == 1)[None, :]
            mats.append(mat)
            pms.append(pm)
        stacks.append(np.concatenate(mats, axis=0))
        pmasks.append(np.stack(pms, axis=0))
    return (np.stack(stacks).astype(np.float32), np.stack(pmasks).astype(np.float32), len(levels))


def _block_tables_np(n_ctx, n_lat_seq, bps):
    nb = n_ctx + n_lat_seq * bps
    blk = np.zeros((2, nb), np.int32)
    first = np.zeros((2, nb), np.int32)
    seq = np.zeros((2, nb), np.int32)
    for d in range(2):
        j = 0
        for s in range(n_ctx):
            blk[d, j], first[d, j], seq[d, j] = s, 1, s
            j += 1
        for s in range(n_lat_seq):
            for p in range(bps):
                pos = p if d == 0 else bps - 1 - p
                blk[d, j], first[d, j], seq[d, j] = n_ctx + s * bps + pos, int(p == 0), n_ctx + s
                j += 1
    return blk, first, seq


def _pick(n, cap):
    best = LANES
    t = LANES
    while t <= min(n, cap):
        if n % t == 0:
            best = t
        t += LANES
    return best


def _cparams(sem):
    return pltpu.CompilerParams(dimension_semantics=sem, vmem_limit_bytes=VMEM_LIMIT)


def _mod_kernel(cond_ref, w_ref, b_ref, o_ref):
    x = cond_ref[...]
    o_ref[...] = _dot(_silu(x), w_ref[...]) + b_ref[...]


def _modulation(cond, mod_w, mod_b):
    depth, d, nd = mod_w.shape
    tn = _pick(nd, 1024)
    return pl.pallas_call(
        _mod_kernel,
        grid=(depth, nd // tn),
        in_specs=[pl.BlockSpec((N_COND_PAD, d), lambda l, j: (0, 0)),
                  pl.BlockSpec((None, d, tn), lambda l, j: (l, 0, j)),
                  pl.BlockSpec((None, 1, tn), lambda l, j: (l, 0, j))],
        out_specs=pl.BlockSpec((None, N_COND_PAD, tn), lambda l, j: (l, 0, j)),
        out_shape=jax.ShapeDtypeStruct((depth, N_COND_PAD, nd), f32),
        compiler_params=_cparams(("arbitrary", "arbitrary")),
        name="modulation",
    )(cond, mod_w, mod_b.reshape(depth, 1, nd))


def _ffn_kernel(x_ref, mod_ref, g_ref, wa_ref, wb_ref, wd_ref, o_ref, h_ref, acc_ref, *, slot, nf):
    f = pl.program_id(1)

    @pl.when(f == 0)
    def _():
        h = _normmod(x_ref[...], g_ref[slot:slot + 1, :], mod_ref[3 * slot + 1:3 * slot + 2, :],
                     mod_ref[3 * slot:3 * slot + 1, :])
        h_ref[...] = h.astype(bf16)
        acc_ref[...] = jnp.zeros_like(acc_ref)

    h = h_ref[...]
    a = jnp.dot(h, wa_ref[...], preferred_element_type=f32)
    b = jnp.dot(h, wb_ref[...], preferred_element_type=f32)
    act = (_silu(a) * b).astype(bf16)
    acc_ref[...] += jnp.dot(act, wd_ref[...], preferred_element_type=f32)

    @pl.when(f == nf - 1)
    def _():
        o_ref[...] = x_ref[...] + 0.5 * mod_ref[3 * slot + 2:3 * slot + 3, :] * acc_ref[...]


def _ffn(x, mod, norm_g, w_up, w_down, slot, cond_of_tile, tm):
    t, d = x.shape
    dff = w_down.shape[0]
    tf = _pick(dff, 512)
    nf = dff // tf
    return pl.pallas_call(
        functools.partial(_ffn_kernel, slot=slot, nf=nf),
        grid=(t // tm, nf),
        in_specs=[pl.BlockSpec((tm, d), lambda i, f: (i, 0)),
                  pl.BlockSpec((None, N_MOD, d), lambda i, f: (cond_of_tile(i), 0, 0)),
                  pl.BlockSpec((3, d), lambda i, f: (0, 0)),
                  pl.BlockSpec((d, tf), lambda i, f: (0, f)),
                  pl.BlockSpec((d, tf), lambda i, f: (0, f + nf)),
                  pl.BlockSpec((tf, d), lambda i, f: (f, 0))],
        out_specs=pl.BlockSpec((tm, d), lambda i, f: (i, 0)),
        out_shape=jax.ShapeDtypeStruct((t, d), f32),
        scratch_shapes=[pltpu.VMEM((tm, d), bf16), pltpu.VMEM((tm, d), f32)],
        compiler_params=_cparams(("arbitrary", "arbitrary")),
        name="ffn%d" % slot,
    )(x, mod, norm_g, w_up, w_up, w_down)


def _mixin_kernel(x_ref, mod_ref, g_ref, w_ref, o_ref, h_ref):
    @pl.when(pl.program_id(1) == 0)
    def _():
        h = _normmod(x_ref[...], g_ref[1:2, :], mod_ref[4:5, :], mod_ref[3:4, :])
        h_ref[...] = h.astype(bf16)

    o_ref[...] = jnp.dot(h_ref[...], w_ref[...], preferred_element_type=f32)


def _mixin(x, mod, norm_g, w, cond_of_tile, tm):
    t, d = x.shape
    nw = w.shape[1]
    tn = _pick(nw, 1280)
    return pl.pallas_call(
        _mixin_kernel,
        grid=(t // tm, nw // tn),
        in_specs=[pl.BlockSpec((tm, d), lambda i, j: (i, 0)),
                  pl.BlockSpec((None, N_MOD, d), lambda i, j: (cond_of_tile(i), 0, 0)),
                  pl.BlockSpec((3, d), lambda i, j: (0, 0)),
                  pl.BlockSpec((d, tn), lambda i, j: (0, j))],
        out_specs=pl.BlockSpec((tm, tn), lambda i, j: (i, j)),
        out_shape=jax.ShapeDtypeStruct((t, nw), f32),
        scratch_shapes=[pltpu.VMEM((tm, d), bf16)],
        compiler_params=_cparams(("arbitrary", "arbitrary")),
        name="mix_in",
    )(x, mod, norm_g, w)


def _hgrn_kernel(blk_ref, first_ref, seq_ref, q_ref, v_ref, f_ref, lb_ref, s0_ref, mst_ref, pm_ref,
                 o_ref, sfin_ref, st_ref, qs_ref, ks_ref, lf_ref, *, n_ctx, tb, bw, nlev):
    d = pl.program_id(0)
    j = pl.program_id(1)
    c = HG_C
    nh = bw // HG_DK
    is_lat = seq_ref[d, j] >= n_ctx

    @pl.when(first_ref[d, j] == 1)
    def _():
        st_ref[...] = jnp.where(is_lat, s0_ref[...], 0.0)

    lb = lb_ref[...]
    fl = f_ref[...]
    lf_ref[...] = jnp.log(jnp.maximum(lb + (1.0 - lb) * _sigmoid(fl), GATE_FLOOR))
    ks_ref[...] = (1.0 - lb) * _sigmoid(-fl)
    qs_ref[...] = _silu(q_ref[...]) * HG_DK ** -0.5

    mst = mst_ref[...].astype(bf16)
    sgn = 1 - 2 * d
    tl = (lax.broadcasted_iota(jnp.int32, (c, bw), 0) % SUB) * sgn

    def chunk(ci, carry):
        ce = jnp.where(d == 0, ci, tb // c - 1 - ci)
        r0 = pl.multiple_of(ce * c, c)
        lf = lf_ref[pl.ds(r0, c), :]
        q = qs_ref[pl.ds(r0, c), :]
        k = ks_ref[pl.ds(r0, c), :]
        v = v_ref[pl.ds(r0, c), :]
        xs = _ldot01(mst, lf)
        b = xs[0:c]
        btot = jnp.sum(lf, axis=0, keepdims=True)
        qe = q * jnp.exp(b)
        kd = k * jnp.exp(btot - b)
        etot = jnp.exp(btot)
        qm, km = [], []
        for lv in range(nlev):
            e = jnp.exp(xs[(lv + 1) * c:(lv + 2) * c])
            qm.append(q * e)
            km.append(k * e)
        b3 = b.reshape(c // SUB, SUB, bw)
        k3 = k.reshape(c // SUB, SUB, bw)
        v3 = v.reshape(c // SUB, SUB, bw)
        accs = [jnp.zeros((c, HG_DK), f32) for _ in range(nh)]
        for sl in range(SUB):
            bs = jnp.broadcast_to(b3[:, sl:sl + 1, :], b3.shape).reshape(c, bw)
            kk = jnp.broadcast_to(k3[:, sl:sl + 1, :], k3.shape).reshape(c, bw)
            vv = jnp.broadcast_to(v3[:, sl:sl + 1, :], v3.shape).reshape(c, bw)
            p = jnp.where(tl >= sl * sgn, q * kk * jnp.exp(jnp.minimum(b - bs, 0.0)), 0.0)
            for h in range(nh):
                hs = slice(h * HG_DK, (h + 1) * HG_DK)
                accs[h] = accs[h] + jnp.sum(p[:, hs], axis=-1, keepdims=True) * vv[:, hs]
        for h in range(nh):
            hs = slice(h * HG_DK, (h + 1) * HG_DK)
            st = st_ref[h]
            o = _dot_nt(qe[:, hs], st) + accs[h]
            sc = jnp.zeros((c, c), f32)
            for lv in range(nlev):
                sc = sc + _dot_nt(qm[lv][:, hs], km[lv][:, hs]) * pm_ref[lv]
            o = o + _dot(sc, v[:, hs])
            o_ref[pl.ds(r0, c), hs] = o
            st_ref[h] = st * etot[:, hs] + _dot_tn(v[:, hs], kd[:, hs])
        return carry

    lax.fori_loop(0, tb // c, chunk, 0)

    @pl.when(jnp.logical_not(is_lat))
    def _():
        sfin_ref[...] = st_ref[...]


def _hgrn_scan(proj, lb, s0t, layer, tables, n_ctx, tb, bw):
    t = proj.shape[0]
    nh = bw // HG_DK
    blk, first, seq = tables
    nb = blk.shape[1]
    mst, pm, nlev = _hgrn_tables_np(HG_C)
    kern = functools.partial(_hgrn_kernel, n_ctx=n_ctx, tb=tb, bw=bw, nlev=nlev)
    row = lambda col: (lambda d, j, b, f, s: (b[d, j], col))
    grid_spec = pltpu.PrefetchScalarGridSpec(
        num_scalar_prefetch=3,
        grid=(2, nb),
        in_specs=[pl.BlockSpec((tb, bw), row(0)),
                  pl.BlockSpec((tb, bw), row(1)),
                  pl.BlockSpec((tb, bw), lambda d, j, b, f, s: (b[d, j], 2 + d)),
                  pl.BlockSpec((None, 1, bw), lambda d, j, b, f, s: (d, 0, 0)),
                  pl.BlockSpec((None, None, None, nh, HG_DK, HG_DK),
                               lambda d, j, b, f, s: (jnp.maximum(s[d, j] - n_ctx, 0), layer, d, 0, 0, 0)),
                  pl.BlockSpec((None,) + mst.shape[1:], lambda d, j, b, f, s: (d, 0, 0)),
                  pl.BlockSpec((None,) + pm.shape[1:], lambda d, j, b, f, s: (d, 0, 0, 0))],
        out_specs=[pl.BlockSpec((None, tb, bw), lambda d, j, b, f, s: (d, b[d, j], 0)),
                   pl.BlockSpec((None, None, nh, HG_DK, HG_DK),
                                lambda d, j, b, f, s: (jnp.minimum(s[d, j], n_ctx - 1), d, 0, 0, 0))],
        scratch_shapes=[pltpu.VMEM((nh, HG_DK, HG_DK), f32), pltpu.VMEM((tb, bw), f32),
                        pltpu.VMEM((tb, bw), f32), pltpu.VMEM((tb, bw), f32)],
    )
    return pl.pallas_call(
        kern, grid_spec=grid_spec,
        out_shape=[jax.ShapeDtypeStruct((2, t, bw), f32),
                   jax.ShapeDtypeStruct((n_ctx, 2, nh, HG_DK, HG_DK), f32)],
        compiler_params=_cparams(("arbitrary", "arbitrary")),
        name="hgrn_scan",
    )(jnp.asarray(blk), jnp.asarray(first), jnp.asarray(seq), proj, proj, proj, lb, s0t,
      jnp.asarray(mst), jnp.asarray(pm))


def _gdn_prep_kernel(prev_ref, cur_ref, nxt_ref, w_ref, o_ref, xe_ref, *, n_ctx, bps, tb, bw, halo):
    r = pl.program_id(0)
    part = pl.program_id(1)
    is_lat = r >= n_ctx
    pos = jnp.where(is_lat, (r - n_ctx) % bps, 0)
    pv = jnp.where(jnp.logical_and(is_lat, pos > 0), 1.0, 0.0)
    nv = jnp.where(jnp.logical_and(is_lat, pos < bps - 1), 1.0, 0.0)
    lat = jnp.where(is_lat, 1.0, 0.0)
    xe_ref[0:halo, :] = prev_ref[...] * pv
    xe_ref[halo:halo + tb, :] = cur_ref[...]
    xe_ref[halo + tb:halo + tb + halo, :] = nxt_ref[...] * nv
    col = lax.broadcasted_iota(jnp.int32, (tb, LANES), 0) % GRID_W
    m_lo = jnp.where(col >= 1, 1.0, 1.0 - lat)
    m_hi = jnp.where(col <= GRID_W - 2, 1.0, 1.0 - lat)
    for g in range(bw // LANES):
        gs = slice(g * LANES, (g + 1) * LANES)
        acc = jnp.zeros((tb, LANES), f32)
        for dr in (-1, 0, 1):
            for dc in (-1, 0, 1):
                off = halo + GRID_W * dr + dc
                tap = (dr + 1) * CONV_K + (dc + 1)
                w = w_ref[tap:tap + 1, gs]
                if dr != 0:
                    w = w * lat
                term = xe_ref[off:off + tb, gs] * w
                if dc == -1:
                    term = term * m_lo
                elif dc == 1:
                    term = term * m_hi
                acc = acc + term
        y = _silu(acc)
        ss = jnp.sum(y * y, axis=-1, keepdims=True)
        nrm = y * lax.rsqrt(ss + NORM_EPS)
        scale = jnp.where(part == 0, GD_DK ** -0.5, 1.0)
        o_ref[:, gs] = jnp.where(part == 2, y, nrm * scale)


def _gdn_prep(proj, conv_w, n_ctx, bps, tb, bw, qkv_col):
    t = proj.shape[0]
    nb = t // tb
    halo = LANES
    hb = tb // halo
    nhb = t // halo
    kern = functools.partial(_gdn_prep_kernel, n_ctx=n_ctx, bps=bps, tb=tb, bw=bw, halo=halo)
    return pl.pallas_call(
        kern,
        grid=(nb, 3),
        in_specs=[pl.BlockSpec((halo, bw), lambda r, p: (jnp.maximum(r * hb - 1, 0), qkv_col + p)),
                  pl.BlockSpec((tb, bw), lambda r, p: (r, qkv_col + p)),
                  pl.BlockSpec((halo, bw), lambda r, p: (jnp.minimum((r + 1) * hb, nhb - 1), qkv_col + p)),
                  pl.BlockSpec((CONV_K * CONV_K, bw), lambda r, p: (0, p))],
        out_specs=pl.BlockSpec((tb, bw), lambda r, p: (r, p)),
        out_shape=jax.ShapeDtypeStruct((t, 3 * bw), f32),
        scratch_shapes=[pltpu.VMEM((tb + 2 * halo, bw), f32)],
        compiler_params=_cparams(("arbitrary", "arbitrary")),
        name="gdn_prep",
    )(proj, proj, proj, conv_w)


def _gdn_kernel(blk_ref, first_ref, seq_ref, q_ref, k_ref, v_ref, misc_ref, exa_ref, exb_ref, alog_ref,
                dtb_ref, s0_ref, tri_ref, msk_ref, o_ref, sfin_ref, st_ref, lg_ref, be_ref,
                *, n_ctx, tb, bw, nmerge):
    d = pl.program_id(0)
    j = pl.program_id(1)
    c = GD_C
    nh = bw // GD_DK
    is_lat = seq_ref[d, j] >= n_ctx

    @pl.when(first_ref[d, j] == 1)
    def _():
        st_ref[...] = jnp.where(is_lat, s0_ref[...], 0.0)

    ab = misc_ref[:, MISC_W - LANES:MISC_W]
    a_in = _rdot01(ab, exa_ref[...])
    b_in = _rdot01(ab, exb_ref[...])
    lg_ref[...] = -jnp.exp(alog_ref[...]) * _softplus(a_in + dtb_ref[...])
    be_ref[...] = _sigmoid(b_in)

    tri = tri_ref[...].astype(bf16)
    strict = msk_ref[0]
    incl = msk_ref[1]
    eye = msk_ref[2]
    mdiag = msk_ref[3]

    def chunk(ci, carry):
        ce = jnp.where(d == 0, ci, tb // c - 1 - ci)
        r0 = pl.multiple_of(ce * c, c)
        lg = lg_ref[pl.ds(r0, c), :]
        be = be_ref[pl.ds(r0, c), :]
        q = q_ref[pl.ds(r0, c), :]
        k = k_ref[pl.ds(r0, c), :]
        v = v_ref[pl.ds(r0, c), :]
        gcum = _ldot01(tri, lg)
        gtot = jnp.sum(lg, axis=0, keepdims=True)
        eg = jnp.exp(gcum)
        qdec = q * eg
        kdec = k * jnp.exp(gtot - gcum)
        gend = jnp.exp(gtot)
        for h in range(nh):
            hs = slice(h * GD_DK, (h + 1) * GD_DK)
            gc = gcum[:, hs]
            dec = jnp.exp(jnp.minimum(gc - gc.T, 0.0))
            kh = k[:, hs]
            kkt = _dot_nt(kh, kh)
            qkt = _dot_nt(q[:, hs], kh)
            n = be[:, hs] * kkt * dec * strict
            tinv = _tri_inverse(n, eye, mdiag, [msk_ref[4 + m] for m in range(nmerge)])
            bek = be[:, hs] * eg[:, hs] * kh
            bev = be[:, hs] * v[:, hs]
            w = _dotp(tinv, bek)
            u = _dotp(tinv, bev)
            p = qkt * dec * incl
            s = st_ref[h]
            vnew = u - _dot(w, s)
            o_ref[pl.ds(r0, c), hs] = _dot(qdec[:, hs], s) + _dot(p, vnew)
            st_ref[h] = gend[:, hs] * s + _dot_tn(kdec[:, hs], vnew)
        return carry

    lax.fori_loop(0, tb // c, chunk, 0)

    @pl.when(jnp.logical_not(is_lat))
    def _():
        sfin_ref[...] = st_ref[...]


def _gdn_scan(qkv, proj, misc_col, exa, exb, alog, dtb, s0, layer, tables, n_ctx, tb, bw):
    t = proj.shape[0]
    nh = bw // GD_DK
    blk, first, seq = tables
    nb = blk.shape[1]
    c = GD_C
    tri = np.stack([_tri_np(c, False), _tri_np(c, True)]).astype(np.float32)
    inv = _inverse_masks_np(c, c)
    msk = np.stack([np.stack(list(_causal_np(c, c, rev)) + inv) for rev in (False, True)]).astype(np.float32)
    nmerge = len(inv) - 2
    kern = functools.partial(_gdn_kernel, n_ctx=n_ctx, tb=tb, bw=bw, nmerge=nmerge)
    row = lambda col: (lambda d, j, b, f, s: (b[d, j], col))
    grid_spec = pltpu.PrefetchScalarGridSpec(
        num_scalar_prefetch=3,
        grid=(2, nb),
        in_specs=[pl.BlockSpec((tb, bw), row(0)),
                  pl.BlockSpec((tb, bw), row(1)),
                  pl.BlockSpec((tb, bw), row(2)),
                  pl.BlockSpec((tb, MISC_W), row(misc_col)),
                  pl.BlockSpec((None, LANES, bw), lambda d, j, b, f, s: (d, 0, 0)),
                  pl.BlockSpec((None, LANES, bw), lambda d, j, b, f, s: (d, 0, 0)),
                  pl.BlockSpec((None, 1, bw), lambda d, j, b, f, s: (d, 0, 0)),
                  pl.BlockSpec((None, 1, bw), lambda d, j, b, f, s: (d, 0, 0)),
                  pl.BlockSpec((None, None, None, nh, GD_DK, GD_DK),
                               lambda d, j, b, f, s: (jnp.maximum(s[d, j] - n_ctx, 0), layer, d, 0, 0, 0)),
                  pl.BlockSpec((None, c, c), lambda d, j, b, f, s: (d, 0, 0)),
                  pl.BlockSpec((None,) + msk.shape[1:], lambda d, j, b, f, s: (d, 0, 0, 0))],
        out_specs=[pl.BlockSpec((None, tb, bw), lambda d, j, b, f, s: (d, b[d, j], 0)),
                   pl.BlockSpec((None, None, nh, GD_DK, GD_DK),
                                lambda d, j, b, f, s: (jnp.minimum(s[d, j], n_ctx - 1), d, 0, 0, 0))],
        scratch_shapes=[pltpu.VMEM((nh, GD_DK, GD_DK), f32), pltpu.VMEM((tb, bw), f32),
                        pltpu.VMEM((tb, bw), f32)],
    )
    return pl.pallas_call(
        kern, grid_spec=grid_spec,
        out_shape=[jax.ShapeDtypeStruct((2, t, bw), f32),
                   jax.ShapeDtypeStruct((n_ctx, 2, nh, GD_DK, GD_DK), f32)],
        compiler_params=_cparams(("arbitrary", "arbitrary")),
        name="gdn_scan",
    )(jnp.asarray(blk), jnp.asarray(first), jnp.asarray(seq), qkv, qkv, qkv, proj, exa, exb, alog, dtb, s0,
      jnp.asarray(tri), jnp.asarray(msk))


def _rwkv_prep_kernel(r_ref, k_ref, v_ref, misc_ref, w0_ref, w2_ref, a0_ref, a2_ref, g2_ref, kkp_ref,
                      ka_ref, rk_ref, b64_ref, kk_ref, lw_ref, icl_ref, bonus_ref, og_ref):
    rr = r_ref[...]
    rk = k_ref[...]
    wl = jnp.tanh(misc_ref[:, 0:LANES])
    al = misc_ref[:, LANES:2 * LANES]
    gl = _sigmoid(misc_ref[:, 2 * LANES:3 * LANES])
    ka = ka_ref[...]
    bon = jnp.zeros_like(rr)
    for d in range(2):
        logw = -_softplus(-(w0_ref[d:d + 1, :] + _dot(wl, w2_ref[d]))) - 0.5
        lw_ref[d] = -jnp.exp(logw)
        icl = _sigmoid(a0_ref[d:d + 1, :] + _dot(al, a2_ref[d]))
        icl_ref[d] = icl
        bon = bon + rr * (rk * (1.0 + (icl - 1.0) * ka)) * rk_ref[...]
    og_ref[...] = _dot(gl, g2_ref[...])
    b64 = b64_ref[...]
    kx = rk * kkp_ref[...]
    kk_ref[...] = kx * lax.rsqrt(_rdot01(kx * kx, b64) + NORM_EPS)
    bonus_ref[...] = _rdot01(bon, b64) * v_ref[...]


def _rwkv_prep(proj, rkv_col, misc_col, w0, w2p, a0, a2p, g2p, kkp, ka, rkp, b64, tb, bw):
    t = proj.shape[0]
    full = lambda shape: pl.BlockSpec(shape, lambda r: (0,) * len(shape))
    return pl.pallas_call(
        _rwkv_prep_kernel,
        grid=(t // tb,),
        in_specs=[pl.BlockSpec((tb, bw), lambda r: (r, rkv_col)),
                  pl.BlockSpec((tb, bw), lambda r: (r, rkv_col + 1)),
                  pl.BlockSpec((tb, bw), lambda r: (r, rkv_col + 2)),
                  pl.BlockSpec((tb, MISC_W), lambda r: (r, misc_col)),
                  full((2, bw)), full((2, LANES, bw)), full((2, bw)), full((2, LANES, bw)),
                  full((LANES, bw)), full((1, bw)), full((1, bw)), full((1, bw)), full((bw, bw))],
        out_specs=[pl.BlockSpec((tb, bw), lambda r: (r, 0)),
                   pl.BlockSpec((2, tb, bw), lambda r: (0, r, 0)),
                   pl.BlockSpec((2, tb, bw), lambda r: (0, r, 0)),
                   pl.BlockSpec((tb, bw), lambda r: (r, 0)),
                   pl.BlockSpec((tb, bw), lambda r: (r, 0))],
        out_shape=[jax.ShapeDtypeStruct((t, bw), f32), jax.ShapeDtypeStruct((2, t, bw), f32),
                   jax.ShapeDtypeStruct((2, t, bw), f32), jax.ShapeDtypeStruct((t, bw), f32),
                   jax.ShapeDtypeStruct((t, bw), f32)],
        compiler_params=_cparams(("arbitrary",)),
        name="rwkv_prep",
    )(proj, proj, proj, proj, w0, w2p, a0, a2p, g2p, kkp, ka, rkp, b64)


def _rwkv_kernel(blk_ref, first_ref, seq_ref, r_ref, k_ref, v_ref, kk_ref, lw_ref, icl_ref, ka_ref, s0_ref,
                 tri_ref, msk_ref, o_ref, sfin_ref, st_ref, *, n_ctx, tb, bw, nmerge):
    d = pl.program_id(0)
    j = pl.program_id(1)
    c = RW_C
    npair = bw // LANES
    is_lat = seq_ref[d, j] >= n_ctx

    @pl.when(first_ref[d, j] == 1)
    def _():
        st_ref[...] = jnp.where(is_lat, s0_ref[...], 0.0)

    tri = tri_ref[...].astype(bf16)
    strict = msk_ref[0]
    incl = msk_ref[1]
    eye = msk_ref[2]
    mdiag = msk_ref[3]
    lane = lax.broadcasted_iota(jnp.int32, (c, LANES), 1)
    m_even = jnp.where(lane < RW_DK, 1.0, 0.0)
    m_odd = 1.0 - m_even
    ka = ka_ref[...]

    def stack2(x):
        return jnp.concatenate([x * m_even, x * m_odd], axis=0)

    def chunk(ci, carry):
        ce = jnp.where(d == 0, ci, tb // c - 1 - ci)
        r0 = pl.multiple_of(ce * c, c)
        lw = lw_ref[pl.ds(r0, c), :]
        icl = icl_ref[pl.ds(r0, c), :]
        kk = kk_ref[pl.ds(r0, c), :]
        rk = k_ref[pl.ds(r0, c), :]
        rr = r_ref[pl.ds(r0, c), :]
        rv = v_ref[pl.ds(r0, c), :]
        cum = _ldot01(tri, lw)
        ctot = jnp.sum(lw, axis=0, keepdims=True)
        kdir = rk * (1.0 + (icl - 1.0) * ka)
        bvec = kk * icl
        einv = jnp.exp(-cum)
        at = (-kk) * jnp.exp(cum - lw)
        kt = kdir * einv
        bt = bvec * einv
        rt = rr * jnp.exp(cum)
        edec = jnp.exp(ctot - cum)
        kdec = kdir * edec
        bdec = bvec * edec
        etot = jnp.exp(ctot)
        for p in range(npair):
            ps = slice(p * LANES, (p + 1) * LANES)
            a2, k2, b2, r2 = stack2(at[:, ps]), stack2(kt[:, ps]), stack2(bt[:, ps]), stack2(rt[:, ps])
            v2, kd2, bd2 = stack2(rv[:, ps]), stack2(kdec[:, ps]), stack2(bdec[:, ps])
            aab = _dot_nt(a2, b2) * strict
            aak = _dot_nt(a2, k2) * strict
            ark = _dot_nt(r2, k2) * incl
            arb = _dot_nt(r2, b2) * incl
            tinv = _tri_inverse(-aab, eye, mdiag, [msk_ref[4 + m] for m in range(nmerge)])
            ta = _dotp(tinv, a2)
            tv = _dotp(tinv, _dot(aak, v2))
            st = st_ref[p]
            u2 = _dot_nt(ta, st) + tv
            y2 = _dot_nt(r2, st) + _dot(ark, v2) + _dot(arb, u2)
            st_ref[p] = st * etot[:, ps] + _dot_tn(v2, kd2) + _dot_tn(u2, bd2)
            o_ref[pl.ds(r0, c), ps] = y2[0:c] + y2[c:2 * c]
        return carry

    lax.fori_loop(0, tb // c, chunk, 0)

    @pl.when(jnp.logical_not(is_lat))
    def _():
        sfin_ref[...] = st_ref[...]


def _rwkv_scan(proj, rkv_col, kk, lw, icl, ka, s0p, layer, tables, n_ctx, tb, bw):
    t = proj.shape[0]
    npair = bw // LANES
    blk, first, seq = tables
    nb = blk.shape[1]
    c = RW_C
    n = 2 * c
    tri = np.stack([_tri_np(c, False), _tri_np(c, True)]).astype(np.float32)
    inv = _inverse_masks_np(n, c)
    msk = np.stack([np.stack(list(_causal_np(n, c, rev)) + inv) for rev in (False, True)]).astype(np.float32)
    nmerge = len(inv) - 2
    kern = functools.partial(_rwkv_kernel, n_ctx=n_ctx, tb=tb, bw=bw, nmerge=nmerge)
    row = lambda col: (lambda d, j, b, f, s: (b[d, j], col))
    drow = lambda d, j, b, f, s: (d, b[d, j], 0)
    grid_spec = pltpu.PrefetchScalarGridSpec(
        num_scalar_prefetch=3,
        grid=(2, nb),
        in_specs=[pl.BlockSpec((tb, bw), row(rkv_col)),
                  pl.BlockSpec((tb, bw), row(rkv_col + 1)),
                  pl.BlockSpec((tb, bw), row(rkv_col + 2)),
                  pl.BlockSpec((tb, bw), row(0)),
                  pl.BlockSpec((None, tb, bw), drow),
                  pl.BlockSpec((None, tb, bw), drow),
                  pl.BlockSpec((1, bw), lambda d, j, b, f, s: (0, 0)),
                  pl.BlockSpec((None, None, None, npair, LANES, LANES),
                               lambda d, j, b, f, s: (jnp.maximum(s[d, j] - n_ctx, 0), layer, d, 0, 0, 0)),
                  pl.BlockSpec((None, c, c), lambda d, j, b, f, s: (d, 0, 0)),
                  pl.BlockSpec((None,) + msk.shape[1:], lambda d, j, b, f, s: (d, 0, 0, 0))],
        out_specs=[pl.BlockSpec((None, tb, bw), drow),
                   pl.BlockSpec((None, None, npair, LANES, LANES),
                                lambda d, j, b, f, s: (jnp.minimum(s[d, j], n_ctx - 1), d, 0, 0, 0))],
        scratch_shapes=[pltpu.VMEM((npair, LANES, LANES), f32)],
    )
    return pl.pallas_call(
        kern, grid_spec=grid_spec,
        out_shape=[jax.ShapeDtypeStruct((2, t, bw), f32),
                   jax.ShapeDtypeStruct((n_ctx, 2, npair, LANES, LANES), f32)],
        compiler_params=_cparams(("arbitrary", "arbitrary")),
        name="rwkv_scan",
    )(jnp.asarray(blk), jnp.asarray(first), jnp.asarray(seq), proj, proj, proj, kk, lw, icl, ka, s0p,
      jnp.asarray(tri), jnp.asarray(msk))


def _merge_kernel(x_ref, mod_ref, ohg_ref, ogd_ref, orw_ref, hgg_ref, gdz_ref, bonus_ref, og_ref,
                  g0_ref, g1_ref, g2_ref, hgn_ref, gdn_ref, lng_ref, lnb_ref, wb_ref, wo_ref, o_ref, *, bw):
    def head_mean(x, width):
        outs = []
        for g in range(bw // LANES):
            xg = x[:, g * LANES:(g + 1) * LANES]
            if width == LANES:
                outs.append(jnp.broadcast_to(jnp.mean(xg, axis=-1, keepdims=True), xg.shape))
            else:
                lane = lax.broadcasted_iota(jnp.int32, xg.shape, 1)
                lo = lane < width
                s_lo = jnp.sum(jnp.where(lo, xg, 0.0), axis=-1, keepdims=True)
                s_hi = jnp.sum(jnp.where(lo, 0.0, xg), axis=-1, keepdims=True)
                outs.append(jnp.where(lo, s_lo, s_hi) * (1.0 / width))
        return jnp.concatenate(outs, axis=-1)

    ohg = ohg_ref[0] + ohg_ref[1]
    ohg = ohg * lax.rsqrt(head_mean(ohg * ohg, HG_DK) + NORM_EPS) * hgn_ref[...]
    ohg = ohg * _silu(hgg_ref[...])
    ogd = ogd_ref[0] + ogd_ref[1]
    ogd = ogd * lax.rsqrt(head_mean(ogd * ogd, GD_DK) + NORM_EPS) * gdn_ref[...]
    ogd = ogd * _silu(gdz_ref[...])
    orw = orw_ref[0] + orw_ref[1]
    mu = head_mean(orw, RW_DK)
    cen = orw - mu
    var = head_mean(cen * cen, RW_DK)
    orw = cen * lax.rsqrt(var + RW_GN_EPS) * lng_ref[...] + lnb_ref[...]
    orw = (orw + bonus_ref[...]) * og_ref[...]
    merged = (_sigmoid(g0_ref[...]) * _dot(ohg, wb_ref[0])
              + _sigmoid(g1_ref[...]) * _dot(ogd, wb_ref[1])
              + _sigmoid(g2_ref[...]) * _dot(orw, wb_ref[2]))
    o_ref[...] = x_ref[...] + mod_ref[5:6, :] * _dot(merged, wo_ref[...])


def _merge(x, mod, proj, ohg, ogd, orw, bonus, og, hgn, gdn, lng, lnb, wb, wo, cond_of_tile, tm, bw,
           hgg_col, gdz_col, gate_col):
    t, d = x.shape
    row = lambda col: (lambda i: (i, col))
    drow = pl.BlockSpec((2, tm, bw), lambda i: (0, i, 0))
    vec = pl.BlockSpec((1, bw), lambda i: (0, 0))
    once = pl.Buffered(1)
    return pl.pallas_call(
        functools.partial(_merge_kernel, bw=bw),
        grid=(t // tm,),
        in_specs=[pl.BlockSpec((tm, d), row(0)),
                  pl.BlockSpec((None, N_MOD, d), lambda i: (cond_of_tile(i), 0, 0)),
                  drow, drow, drow,
                  pl.BlockSpec((tm, bw), row(hgg_col)),
                  pl.BlockSpec((tm, bw), row(gdz_col)),
                  pl.BlockSpec((tm, bw), row(0)),
                  pl.BlockSpec((tm, bw), row(0)),
                  pl.BlockSpec((tm, d), row(gate_col)),
                  pl.BlockSpec((tm, d), row(gate_col + 1)),
                  pl.BlockSpec((tm, d), row(gate_col + 2)),
                  vec, vec, vec, vec,
                  pl.BlockSpec((3, bw, d), lambda i: (0, 0, 0), pipeline_mode=once),
                  pl.BlockSpec((d, d), lambda i: (0, 0), pipeline_mode=once)],
        out_specs=pl.BlockSpec((tm, d), row(0)),
        out_shape=jax.ShapeDtypeStruct((t, d), f32),
        compiler_params=_cparams(("arbitrary",)),
        name="merge",
    )(x, mod, ohg, ogd, orw, proj, proj, bonus, og, proj, proj, proj, hgn, gdn, lng, lnb, wb, wo)


def _final_norm_kernel(x_ref, g_ref, o_ref):
    x = x_ref[...]
    ms = jnp.mean(x * x, axis=-1, keepdims=True)
    o_ref[...] = x * lax.rsqrt(ms + NORM_EPS) * g_ref[...]


def _final_norm(x, g, tm):
    t, d = x.shape
    return pl.pallas_call(
        _final_norm_kernel,
        grid=(t // tm,),
        in_specs=[pl.BlockSpec((tm, d), lambda i: (i, 0)), pl.BlockSpec((1, d), lambda i: (0, 0))],
        out_specs=pl.BlockSpec((tm, d), lambda i: (i, 0)),
        out_shape=jax.ShapeDtypeStruct((t, d), f32),
        compiler_params=_cparams(("arbitrary",)),
        name="final_norm",
    )(x, g.reshape(1, d))


def _pad_rows(w, rows):
    return jnp.concatenate([w, jnp.zeros((rows - w.shape[0],) + w.shape[1:], w.dtype)], axis=0)


def kernel(x_prompt, x_sample, state_hgrn, state_gdn, state_rwkv, c, c_ctx, mod_w, mod_b, norm_g, ffn_up,
           ffn_down, mix_in, hgrn_lb, hgrn_norm, gdn_conv, gdn_a_log, gdn_dt_bias, gdn_norm, rwkv_w0, rwkv_w2,
           rwkv_a0, rwkv_a2, rwkv_g2, rwkv_kk, rwkv_ka, rwkv_rk, rwkv_ln_g, rwkv_ln_b, mix_branch, mix_out,
           final_norm):
    bc, lc, d = x_prompt.shape
    bs, ls, _ = x_sample.shape
    depth = mod_w.shape[0]
    bw = d // 4
    tb = lc
    assert tb % GD_C == 0 and ls % tb == 0 and bw % LANES == 0 and tb % GRID_W == 0
    n_ctx = bc
    bps = ls // tb
    tc = bc * lc
    t = tc + bs * ls
    hg_h, gd_h, rw_h = bw // HG_DK, bw // GD_DK, bw // RW_DK
    lw_w, la_w, lg_w = rwkv_w2.shape[2], rwkv_a2.shape[2], rwkv_g2.shape[1]
    assert max(2 * lw_w, 2 * la_w, lg_w, 4 * gd_h) <= LANES

    tm = 512 if (tc % 512 == 0 and ls % 512 == 0) else tb
    cond_of_tile = lambda tile_rows: (lambda i: jnp.where(i * tile_rows < tc, 0, 1 + (i * tile_rows - tc) // ls))

    x = jnp.concatenate([x_prompt.reshape(tc, d), x_sample.reshape(bs * ls, d)], axis=0)
    cond = jnp.concatenate([c_ctx[None, :], c, jnp.zeros((N_COND_PAD - 1 - bs, d), f32)], axis=0)
    mod_all = _modulation(cond, mod_w, mod_b).reshape(depth, N_COND_PAD, N_MOD, d)

    lb_p = jax.nn.softmax(hgrn_lb.astype(f32), axis=0)
    lb_all = jnp.cumsum(lb_p, axis=0) - lb_p[0:1]

    o_hg, o_gdq, o_gdz, o_gda, o_rw, o_lo, o_mg = 0, 5 * bw, 8 * bw, 9 * bw, 9 * bw + 4 * gd_h, 0, 0
    o_rw = 9 * bw + 4 * gd_h
    o_lo = o_rw + 3 * bw
    o_mg = o_lo + 2 * lw_w + 2 * la_w + lg_w
    zpad = lambda n: jnp.zeros((depth, d, n), mix_in.dtype)
    misc = jnp.concatenate([
        mix_in[:, :, o_lo:o_lo + 2 * lw_w], zpad(LANES - 2 * lw_w),
        mix_in[:, :, o_lo + 2 * lw_w:o_lo + 2 * lw_w + 2 * la_w], zpad(LANES - 2 * la_w),
        mix_in[:, :, o_lo + 2 * lw_w + 2 * la_w:o_mg], zpad(LANES - lg_w),
        mix_in[:, :, o_gda:o_gda + 4 * gd_h], zpad(LANES - 4 * gd_h)], axis=2)
    w_in = jnp.concatenate([mix_in[:, :, 0:9 * bw], mix_in[:, :, o_rw:o_rw + 3 * bw],
                            mix_in[:, :, o_mg:o_mg + 3 * d], misc], axis=2).astype(bf16)
    col_gdq, col_gdz, col_rw, col_gate = 5, 8, 9, 3
    col_misc = (24 * bw) // MISC_W
    assert (24 * bw) % MISC_W == 0

    up16 = ffn_up.astype(bf16)
    down16 = ffn_down.astype(bf16)
    wb16 = mix_branch.astype(bf16)
    wo16 = mix_out.astype(bf16)

    exa = np.zeros((2, LANES, bw), np.float32)
    exb = np.zeros((2, LANES, bw), np.float32)
    for dd in range(2):
        for h in range(gd_h):
            exa[dd, dd * gd_h + h, h * GD_DK:(h + 1) * GD_DK] = 1.0
            exb[dd, 2 * gd_h + dd * gd_h + h, h * GD_DK:(h + 1) * GD_DK] = 1.0
    exa, exb = jnp.asarray(exa), jnp.asarray(exb)
    b64 = jnp.asarray(np.kron(np.eye(rw_h), np.ones((RW_DK, RW_DK))).astype(np.float32))

    tables = _block_tables_np(n_ctx, bs, bps)

    s0_hg = jnp.swapaxes(state_hgrn, -1, -2)
    s0_gd = state_gdn
    eye2 = jnp.eye(2, dtype=f32)
    s0_rw = state_rwkv.reshape(bs, depth, 2, rw_h // 2, 2, RW_DK, RW_DK)
    s0_rw = jnp.einsum('bldpeij,ef->bldpeifj', s0_rw, eye2).reshape(bs, depth, 2, rw_h // 2, LANES, LANES)

    new_hg, new_gd, new_rw = [], [], []
    for l in range(depth):
        mod = mod_all[l]
        x = _ffn(x, mod, norm_g[l], up16[l, 0], down16[l, 0], 0, cond_of_tile(tm), tm)
        proj = _mixin(x, mod, norm_g[l], w_in[l], cond_of_tile(tm), tm)

        ohg, s_hg = _hgrn_scan(proj, lb_all[l].reshape(2, 1, bw), s0_hg, l, tables, n_ctx, tb, bw)

        conv_w = gdn_conv[l].reshape(CONV_K * CONV_K, 3 * bw)
        qkv = _gdn_prep(proj, conv_w, n_ctx, bps, tb, bw, col_gdq)
        alog = jnp.repeat(gdn_a_log[l], GD_DK, axis=-1).reshape(2, 1, bw)
        dtb = jnp.repeat(gdn_dt_bias[l], GD_DK, axis=-1).reshape(2, 1, bw)
        ogd, s_gd = _gdn_scan(qkv, proj, col_misc, exa, exb, alog, dtb, s0_gd, l, tables, n_ctx, tb, bw)

        w2p = jnp.stack([_pad_rows(jnp.concatenate([jnp.zeros((dd * lw_w, bw), f32), rwkv_w2[l, dd]], 0), LANES)
                         for dd in range(2)])
        a2p = jnp.stack([_pad_rows(jnp.concatenate([jnp.zeros((dd * la_w, bw), f32), rwkv_a2[l, dd]], 0), LANES)
                         for dd in range(2)])
        g2p = _pad_rows(rwkv_g2[l], LANES)
        ka = rwkv_ka[l].reshape(1, bw)
        kk, lw, icl, bonus, og = _rwkv_prep(proj, col_rw, col_misc, rwkv_w0[l], w2p, rwkv_a0[l], a2p, g2p,
                                            rwkv_kk[l].reshape(1, bw), ka, rwkv_rk[l].reshape(1, bw), b64,
                                            tb, bw)
        orw, s_rw = _rwkv_scan(proj, col_rw, kk, lw, icl, ka, s0_rw, l, tables, n_ctx, tb, bw)

        x = _merge(x, mod, proj, ohg, ogd, orw, bonus, og, hgrn_norm[l].reshape(1, bw),
                   jnp.tile(gdn_norm[l], gd_h).reshape(1, bw), rwkv_ln_g[l].reshape(1, bw),
                   rwkv_ln_b[l].reshape(1, bw), wb16[l], wo16[l], cond_of_tile(tb), tb, bw,
                   4, col_gdz, col_gate)
        x = _ffn(x, mod, norm_g[l], up16[l, 1], down16[l, 1], 2, cond_of_tile(tm), tm)

        new_hg.append(jnp.swapaxes(s_hg, -1, -2))
        new_gd.append(s_gd)
        s_rw = s_rw.reshape(n_ctx, 2, rw_h // 2, 2, RW_DK, 2, RW_DK)
        new_rw.append(jnp.einsum('bdpeifj,ef->bdpeij', s_rw, eye2).reshape(n_ctx, 2, rw_h, RW_DK, RW_DK))

    y = _final_norm(x, final_norm, tm)
    return (y[:tc].reshape(bc, lc, d), y[tc:].reshape(bs, ls, d),
            jnp.stack(new_hg, axis=1), jnp.stack(new_gd, axis=1), jnp.stack(new_rw, axis=1))
```

```python
import functools

import numpy as np
import jax
import jax.numpy as jnp
from jax import lax
from jax.experimental import pallas as pl
from jax.experimental.pallas import tpu as pltpu

f32 = jnp.float32
bf16 = jnp.bfloat16

LANES = 128
SUBLANES = 8
VMEM_LIMIT = 56 * 1024 * 1024

HG_DK = 128
GD_DK = 128
RW_DK = 64
N_MOD = 9
GRID_W = 64
CONV_K = 3
NORM_EPS = 1e-6
RW_GN_EPS = 64e-5
GATE_FLOOR = 1e-30
N_COND_PAD = 16

HG_C = 64
GD_C = 128
RW_C = 64
SUB = 8
INV_PASSES = 1
MISC_W = 4 * LANES


def _sigmoid(x):
    return 1.0 / (1.0 + jnp.exp(-x))


def _silu(x):
    return x * _sigmoid(x)


def _softplus(x):
    return jnp.maximum(x, 0.0) + jnp.log(1.0 + jnp.exp(-jnp.abs(x)))


def _dot(a, b):
    return jnp.dot(a.astype(bf16), b.astype(bf16), preferred_element_type=f32)


def _dot_nt(a, b):
    return lax.dot_general(a.astype(bf16), b.astype(bf16), (((1,), (1,)), ((), ())),
                           preferred_element_type=f32)


def _dot_tn(a, b):
    return lax.dot_general(a.astype(bf16), b.astype(bf16), (((0,), (0,)), ((), ())),
                           preferred_element_type=f32)


def _split3(x):
    hi = x.astype(bf16)
    r = x - hi.astype(f32)
    mid = r.astype(bf16)
    lo = (r - mid.astype(f32)).astype(bf16)
    return hi, mid, lo


def _ldot01(m, x):
    hi, mid, lo = _split3(x)
    m = m.astype(bf16)
    d = functools.partial(jnp.dot, preferred_element_type=f32)
    return d(m, hi) + d(m, mid) + d(m, lo)


def _rdot01(x, m):
    hi, mid, lo = _split3(x)
    m = m.astype(bf16)
    d = functools.partial(jnp.dot, preferred_element_type=f32)
    return d(hi, m) + d(mid, m) + d(lo, m)


def _dotp(a, b):
    if INV_PASSES == 1:
        return _dot(a, b)
    ah = a.astype(bf16)
    al = (a - ah.astype(f32)).astype(bf16)
    bh = b.astype(bf16)
    bl = (b - bh.astype(f32)).astype(bf16)
    d = functools.partial(jnp.dot, preferred_element_type=f32)
    return d(ah, bh) + d(ah, bl) + d(al, bh)


def _tri_inverse_y(ns, mdiag, moffs):
    nd = [n * mdiag for n in ns]
    n2 = [_dotp(a, a) for a in nd]
    n4 = [_dotp(a, a) for a in n2]
    y = [b - a - _dotp(a, b) for a, b in zip(nd, n2)]
    y = [a + b + _dotp(a, b) for a, b in zip(y, n4)]
    for m in moffs:
        xl = [n * m for n in ns]
        xl = [l + _dotp(a, l) for a, l in zip(y, xl)]
        y = [a - l - _dotp(l, a) for a, l in zip(y, xl)]
    return y


def _normmod(x, g, scale, shift):
    ms = jnp.mean(x * x, axis=-1, keepdims=True)
    return (x * lax.rsqrt(ms + NORM_EPS) * g) * (1.0 + scale) + shift


def _tri_np(c, rev):
    i = np.arange(c)
    return (i[None, :] >= i[:, None]) if rev else (i[None, :] <= i[:, None])


def _inverse_masks_np(n, c):
    i = np.arange(n)
    same = (i[:, None] // c) == (i[None, :] // c)
    eye = np.eye(n, dtype=bool)
    mdiag = (i[:, None] // SUB) == (i[None, :] // SUB)
    out = [eye, mdiag]
    m = SUB
    while m < c:
        out.append(same & ((i[:, None] // (2 * m)) == (i[None, :] // (2 * m)))
                   & ((i[:, None] // m) != (i[None, :] // m)))
        m *= 2
    return out


def _causal_np(n, c, rev):
    i = np.arange(n)
    same = (i[:, None] // c) == (i[None, :] // c)
    il, jl = (i % c)[:, None], (i % c)[None, :]
    strict = same & ((jl > il) if rev else (jl < il))
    incl = same & ((jl >= il) if rev else (jl <= il))
    return strict, incl


def _hgrn_tables_np(c):
    levels = []
    m = SUB
    while m < c:
        levels.append(m)
        m *= 2
    stacks, pmasks = [], []
    t = np.arange(c)
    for rev in (False, True):
        mats = [_tri_np(c, rev)]
        pms = []
        for m in levels:
            pair = t // (2 * m)
            half = (t // m) % 2
            mat = np.zeros((c, c), bool)
            tau = t[None, :]
            if not rev:
                bd = (pair * 2 * m + m - 1)[:, None]
                qside = (half == 1)[:, None]
                mat = np.where(qside, (tau > bd) & (tau <= t[:, None]), (tau > t[:, None]) & (tau <= bd))
                pm = (pair[:, None] == pair[None, :]) & (half == 1)[:, None] & (half == 0)[None, :]
            else:
                bd = (pair * 2 * m + m)[:, None]
                qside = (half == 0)[:, None]
                mat = np.where(qside, (tau >= t[:, None]) & (tau < bd), (tau >= bd) & (tau < t[:, None]))
                pm = (pair[:, None] == pair[None, :]) & (half == 0)[:, None] & (half == 1)[None, :]
            mats.append(mat)
            pms.append(pm)
        stacks.append(np.concatenate(mats, axis=0))
        pmasks.append(np.stack(pms, axis=0))
    return (np.stack(stacks).astype(np.float32), np.stack(pmasks).astype(np.float32), len(levels))


def _block_tables_np(n_ctx, n_lat_seq, bps):
    nb = n_ctx + n_lat_seq * bps
    blk = np.zeros((2, nb), np.int32)
    first = np.zeros((2, nb), np.int32)
    seq = np.zeros((2, nb), np.int32)
    for d in range(2):
        j = 0
        for s in range(n_ctx):
            blk[d, j], first[d, j], seq[d, j] = s, 1, s
            j += 1
        for s in range(n_lat_seq):
            for p in range(bps):
                pos = p if d == 0 else bps - 1 - p
                blk[d, j], first[d, j], seq[d, j] = n_ctx + s * bps + pos, int(p == 0), n_ctx + s
                j += 1
    return blk, first, seq


def _pick(n, cap):
    best = LANES
    t = LANES
    while t <= min(n, cap):
        if n % t == 0:
            best = t
        t += LANES
    return best


def _cparams(sem):
    return pltpu.CompilerParams(dimension_semantics=sem, vmem_limit_bytes=VMEM_LIMIT)


def _mod_kernel(cond_ref, w_ref, b_ref, o_ref):
    x = cond_ref[...]
    o_ref[...] = _dot(_silu(x), w_ref[...]) + b_ref[...]


def _modulation(cond, mod_w, mod_b):
    depth, d, nd = mod_w.shape
    tn = _pick(nd, 1024)
    return pl.pallas_call(
        _mod_kernel,
        grid=(depth, nd // tn),
        in_specs=[pl.BlockSpec((N_COND_PAD, d), lambda l, j: (0, 0)),
                  pl.BlockSpec((None, d, tn), lambda l, j: (l, 0, j)),
                  pl.BlockSpec((None, 1, tn), lambda l, j: (l, 0, j))],
        out_specs=pl.BlockSpec((None, N_COND_PAD, tn), lambda l, j: (l, 0, j)),
        out_shape=jax.ShapeDtypeStruct((depth, N_COND_PAD, nd), f32),
        compiler_params=_cparams(("arbitrary", "arbitrary")),
        name="modulation",
    )(cond, mod_w, mod_b.reshape(depth, 1, nd))


def _ffn_kernel(x_ref, mod_ref, g_ref, wa_ref, wb_ref, wd_ref, o_ref, h_ref, acc_ref, *, slot, nf):
    f = pl.program_id(1)

    @pl.when(f == 0)
    def _():
        h = _normmod(x_ref[...], g_ref[slot:slot + 1, :], mod_ref[3 * slot + 1:3 * slot + 2, :],
                     mod_ref[3 * slot:3 * slot + 1, :])
        h_ref[...] = h.astype(bf16)
        acc_ref[...] = jnp.zeros_like(acc_ref)

    h = h_ref[...]
    a = jnp.dot(h, wa_ref[...], preferred_element_type=f32)
    b = jnp.dot(h, wb_ref[...], preferred_element_type=f32)
    act = (_silu(a) * b).astype(bf16)
    acc_ref[...] += jnp.dot(act, wd_ref[...], preferred_element_type=f32)

    @pl.when(f == nf - 1)
    def _():
        o_ref[...] = x_ref[...] + 0.5 * mod_ref[3 * slot + 2:3 * slot + 3, :] * acc_ref[...]


def _ffn(x, mod, norm_g, w_up, w_down, slot, cond_of_tile, tm):
    t, d = x.shape
    dff = w_down.shape[0]
    tf = _pick(dff, 512)
    nf = dff // tf
    return pl.pallas_call(
        functools.partial(_ffn_kernel, slot=slot, nf=nf),
        grid=(t // tm, nf),
        in_specs=[pl.BlockSpec((tm, d), lambda i, f: (i, 0)),
                  pl.BlockSpec((None, N_MOD, d), lambda i, f: (cond_of_tile(i), 0, 0)),
                  pl.BlockSpec((3, d), lambda i, f: (0, 0)),
                  pl.BlockSpec((d, tf), lambda i, f: (0, f)),
                  pl.BlockSpec((d, tf), lambda i, f: (0, f + nf)),
                  pl.BlockSpec((tf, d), lambda i, f: (f, 0))],
        out_specs=pl.BlockSpec((tm, d), lambda i, f: (i, 0)),
        out_shape=jax.ShapeDtypeStruct((t, d), f32),
        scratch_shapes=[pltpu.VMEM((tm, d), bf16), pltpu.VMEM((tm, d), f32)],
        compiler_params=_cparams(("arbitrary", "arbitrary")),
        name="ffn%d" % slot,
    )(x, mod, norm_g, w_up, w_up, w_down)


def _mixin_kernel(x_ref, mod_ref, g_ref, w_ref, o_ref, h_ref):
    @pl.when(pl.program_id(1) == 0)
    def _():
        h = _normmod(x_ref[...], g_ref[1:2, :], mod_ref[4:5, :], mod_ref[3:4, :])
        h_ref[...] = h.astype(bf16)

    o_ref[...] = jnp.dot(h_ref[...], w_ref[...], preferred_element_type=f32)


def _mixin(x, mod, norm_g, w, cond_of_tile, tm):
    t, d = x.shape
    nw = w.shape[1]
    tn = _pick(nw, 1280)
    return pl.pallas_call(
        _mixin_kernel,
        grid=(t // tm, nw // tn),
        in_specs=[pl.BlockSpec((tm, d), lambda i, j: (i, 0)),
                  pl.BlockSpec((None, N_MOD, d), lambda i, j: (cond_of_tile(i), 0, 0)),
                  pl.BlockSpec((3, d), lambda i, j: (0, 0)),
                  pl.BlockSpec((d, tn), lambda i, j: (0, j))],
        out_specs=pl.BlockSpec((tm, tn), lambda i, j: (i, j)),
        out_shape=jax.ShapeDtypeStruct((t, nw), f32),
        scratch_shapes=[pltpu.VMEM((tm, d), bf16)],
        compiler_params=_cparams(("arbitrary", "arbitrary")),
        name="mix_in",
    )(x, mod, norm_g, w)


def _hgrn_kernel(blk_ref, first_ref, seq_ref, q_ref, v_ref, f_ref, lb_ref, s0_ref, mst_ref, pm_ref,
                 o_ref, sfin_ref, st_ref, qs_ref, ks_ref, lf_ref, *, n_ctx, tb, bw, nlev):
    d = pl.program_id(0)
    j = pl.program_id(1)
    c = HG_C
    nh = bw // HG_DK
    is_lat = seq_ref[d, j] >= n_ctx

    @pl.when(first_ref[d, j] == 1)
    def _():
        st_ref[...] = jnp.where(is_lat, s0_ref[...], 0.0)

    lb = lb_ref[...]
    fl = f_ref[...]
    lf_ref[...] = jnp.log(jnp.maximum(lb + (1.0 - lb) * _sigmoid(fl), GATE_FLOOR))
    ks_ref[...] = (1.0 - lb) * _sigmoid(-fl)
    qs_ref[...] = _silu(q_ref[...]) * HG_DK ** -0.5

    mst = mst_ref[...].astype(bf16)
    sgn = 1 - 2 * d
    tl = (lax.broadcasted_iota(jnp.int32, (c, bw), 0) % SUB) * sgn

    def chunk(ci, carry):
        ce = jnp.where(d == 0, ci, tb // c - 1 - ci)
        r0 = pl.multiple_of(ce * c, c)
        lf = lf_ref[pl.ds(r0, c), :]
        q = qs_ref[pl.ds(r0, c), :]
        k = ks_ref[pl.ds(r0, c), :]
        v = v_ref[pl.ds(r0, c), :]
        xs = _ldot01(mst, lf)
        b = xs[0:c]
        btot = jnp.sum(lf, axis=0, keepdims=True)
        qe = q * jnp.exp(b)
        kd = k * jnp.exp(btot - b)
        etot = jnp.exp(btot)
        qm, km = [], []
        for lv in range(nlev):
            e = jnp.exp(xs[(lv + 1) * c:(lv + 2) * c])
            qm.append(q * e)
            km.append(k * e)
        b3 = b.reshape(c // SUB, SUB, bw)
        k3 = k.reshape(c // SUB, SUB, bw)
        v3 = v.reshape(c // SUB, SUB, bw)
        accs = [jnp.zeros((c, HG_DK), f32) for _ in range(nh)]
        for sl in range(SUB):
            bs = jnp.broadcast_to(b3[:, sl:sl + 1, :], b3.shape).reshape(c, bw)
            kk = jnp.broadcast_to(k3[:, sl:sl + 1, :], k3.shape).reshape(c, bw)
            vv = jnp.broadcast_to(v3[:, sl:sl + 1, :], v3.shape).reshape(c, bw)
            p = jnp.where(tl >= sl * sgn, q * kk * jnp.exp(jnp.minimum(b - bs, 0.0)), 0.0)
            for h in range(nh):
                hs = slice(h * HG_DK, (h + 1) * HG_DK)
                accs[h] = accs[h] + jnp.sum(p[:, hs], axis=-1, keepdims=True) * vv[:, hs]
        for h in range(nh):
            hs = slice(h * HG_DK, (h + 1) * HG_DK)
            st = st_ref[h]
            o = _dot_nt(qe[:, hs], st) + accs[h]
            sc = jnp.zeros((c, c), f32)
            for lv in range(nlev):
                sc = sc + _dot_nt(qm[lv][:, hs], km[lv][:, hs]) * pm_ref[lv]
            o = o + _dot(sc, v[:, hs])
            o_ref[pl.ds(r0, c), hs] = o
            st_ref[h] = st * etot[:, hs] + _dot_tn(v[:, hs], kd[:, hs])
        return carry

    lax.fori_loop(0, tb // c, chunk, 0)

    @pl.when(jnp.logical_not(is_lat))
    def _():
        sfin_ref[...] = st_ref[...]


def _hgrn_scan(proj, lb, s0t, layer, tables, n_ctx, tb, bw):
    t = proj.shape[0]
    nh = bw // HG_DK
    blk, first, seq = tables
    nb = blk.shape[1]
    mst, pm, nlev = _hgrn_tables_np(HG_C)
    kern = functools.partial(_hgrn_kernel, n_ctx=n_ctx, tb=tb, bw=bw, nlev=nlev)
    row = lambda col: (lambda d, j, b, f, s: (b[d, j], col))
    grid_spec = pltpu.PrefetchScalarGridSpec(
        num_scalar_prefetch=3,
        grid=(2, nb),
        in_specs=[pl.BlockSpec((tb, bw), row(0)),
                  pl.BlockSpec((tb, bw), row(1)),
                  pl.BlockSpec((tb, bw), lambda d, j, b, f, s: (b[d, j], 2 + d)),
                  pl.BlockSpec((None, 1, bw), lambda d, j, b, f, s: (d, 0, 0)),
                  pl.BlockSpec((None, None, None, nh, HG_DK, HG_DK),
                               lambda d, j, b, f, s: (jnp.maximum(s[d, j] - n_ctx, 0), layer, d, 0, 0, 0)),
                  pl.BlockSpec((None,) + mst.shape[1:], lambda d, j, b, f, s: (d, 0, 0)),
                  pl.BlockSpec((None,) + pm.shape[1:], lambda d, j, b, f, s: (d, 0, 0, 0))],
        out_specs=[pl.BlockSpec((None, tb, bw), lambda d, j, b, f, s: (d, b[d, j], 0)),
                   pl.BlockSpec((None, None, nh, HG_DK, HG_DK),
                                lambda d, j, b, f, s: (jnp.minimum(s[d, j], n_ctx - 1), d, 0, 0, 0))],
        scratch_shapes=[pltpu.VMEM((nh, HG_DK, HG_DK), f32), pltpu.VMEM((tb, bw), f32),
                        pltpu.VMEM((tb, bw), f32), pltpu.VMEM((tb, bw), f32)],
    )
    return pl.pallas_call(
        kern, grid_spec=grid_spec,
        out_shape=[jax.ShapeDtypeStruct((2, t, bw), f32),
                   jax.ShapeDtypeStruct((n_ctx, 2, nh, HG_DK, HG_DK), f32)],
        compiler_params=_cparams(("arbitrary", "arbitrary")),
        name="hgrn_scan",
    )(jnp.asarray(blk), jnp.asarray(first), jnp.asarray(seq), proj, proj, proj, lb, s0t,
      jnp.asarray(mst), jnp.asarray(pm))


def _gdn_prep_kernel(prev_ref, cur_ref, nxt_ref, w_ref, o_ref, xe_ref, *, n_ctx, bps, tb, bw, halo):
    r = pl.program_id(0)
    part = pl.program_id(1)
    is_lat = r >= n_ctx
    pos = jnp.where(is_lat, (r - n_ctx) % bps, 0)
    pv = jnp.where(jnp.logical_and(is_lat, pos > 0), 1.0, 0.0)
    nv = jnp.where(jnp.logical_and(is_lat, pos < bps - 1), 1.0, 0.0)
    lat = jnp.where(is_lat, 1.0, 0.0)
    xe_ref[0:halo, :] = prev_ref[...] * pv
    xe_ref[halo:halo + tb, :] = cur_ref[...]
    xe_ref[halo + tb:halo + tb + halo, :] = nxt_ref[...] * nv
    col = lax.broadcasted_iota(jnp.int32, (tb, LANES), 0) % GRID_W
    m_lo = jnp.where(col >= 1, 1.0, 1.0 - lat)
    m_hi = jnp.where(col <= GRID_W - 2, 1.0, 1.0 - lat)
    for g in range(bw // LANES):
        gs = slice(g * LANES, (g + 1) * LANES)
        acc = jnp.zeros((tb, LANES), f32)
        for dr in (-1, 0, 1):
            for dc in (-1, 0, 1):
                off = halo + GRID_W * dr + dc
                tap = (dr + 1) * CONV_K + (dc + 1)
                w = w_ref[tap:tap + 1, gs]
                if dr != 0:
                    w = w * lat
                term = xe_ref[off:off + tb, gs] * w
                if dc == -1:
                    term = term * m_lo
                elif dc == 1:
                    term = term * m_hi
                acc = acc + term
        y = _silu(acc)
        ss = jnp.sum(y * y, axis=-1, keepdims=True)
        nrm = y * lax.rsqrt(ss + NORM_EPS)
        scale = jnp.where(part == 0, GD_DK ** -0.5, 1.0)
        o_ref[:, gs] = jnp.where(part == 2, y, nrm * scale)


def _gdn_prep(proj, conv_w, n_ctx, bps, tb, bw, qkv_col):
    t = proj.shape[0]
    nb = t // tb
    halo = LANES
    hb = tb // halo
    nhb = t // halo
    kern = functools.partial(_gdn_prep_kernel, n_ctx=n_ctx, bps=bps, tb=tb, bw=bw, halo=halo)
    return pl.pallas_call(
        kern,
        grid=(nb, 3),
        in_specs=[pl.BlockSpec((halo, bw), lambda r, p: (jnp.maximum(r * hb - 1, 0), qkv_col + p)),
                  pl.BlockSpec((tb, bw), lambda r, p: (r, qkv_col + p)),
                  pl.BlockSpec((halo, bw), lambda r, p: (jnp.minimum((r + 1) * hb, nhb - 1), qkv_col + p)),
                  pl.BlockSpec((CONV_K * CONV_K, bw), lambda r, p: (0, p))],
        out_specs=pl.BlockSpec((tb, bw), lambda r, p: (r, p)),
        out_shape=jax.ShapeDtypeStruct((t, 3 * bw), f32),
        scratch_shapes=[pltpu.VMEM((tb + 2 * halo, bw), f32)],
        compiler_params=_cparams(("arbitrary", "arbitrary")),
        name="gdn_prep",
    )(proj, proj, proj, conv_w)


def _gdn_kernel(blk_ref, first_ref, seq_ref, q_ref, k_ref, v_ref, misc_ref, exa_ref, exb_ref, alog_ref,
                dtb_ref, s0_ref, tri_ref, msk_ref, o_ref, sfin_ref, st_ref, lg_ref, be_ref,
                *, n_ctx, tb, bw, nmerge):
    d = pl.program_id(0)
    j = pl.program_id(1)
    c = GD_C
    nh = bw // GD_DK
    is_lat = seq_ref[d, j] >= n_ctx

    @pl.when(first_ref[d, j] == 1)
    def _():
        st_ref[...] = jnp.where(is_lat, s0_ref[...], 0.0)

    ab = misc_ref[:, MISC_W - LANES:MISC_W]
    a_in = _rdot01(ab, exa_ref[...])
    b_in = _rdot01(ab, exb_ref[...])
    lg_ref[...] = -jnp.exp(alog_ref[...]) * _softplus(a_in + dtb_ref[...])
    be_ref[...] = _sigmoid(b_in)

    tri = tri_ref[...].astype(bf16)
    strict = msk_ref[0]
    incl = msk_ref[1]
    mdiag = msk_ref[3]

    def chunk(ci, carry):
        ce = jnp.where(d == 0, ci, tb // c - 1 - ci)
        r0 = pl.multiple_of(ce * c, c)
        lg = lg_ref[pl.ds(r0, c), :]
        be = be_ref[pl.ds(r0, c), :]
        q = q_ref[pl.ds(r0, c), :]
        k = k_ref[pl.ds(r0, c), :]
        v = v_ref[pl.ds(r0, c), :]
        gcum = _ldot01(tri, lg)
        gtot = jnp.sum(lg, axis=0, keepdims=True)
        eg = jnp.exp(gcum)
        qdec = q * eg
        kdec = k * jnp.exp(gtot - gcum)
        gend = jnp.exp(gtot)
        heads = [slice(h * GD_DK, (h + 1) * GD_DK) for h in range(nh)]
        qk = [_dot_nt(jnp.concatenate([q[:, hs], k[:, hs]], axis=0), k[:, hs]) for hs in heads]
        dec = []
        for hs in heads:
            gc = gcum[:, hs]
            dec.append(jnp.exp(jnp.minimum(gc - gc.T, 0.0)))
        ns = [be[:, hs] * a[c:2 * c] * e * strict for hs, a, e in zip(heads, qk, dec)]
        ys = _tri_inverse_y(ns, mdiag, [msk_ref[4 + m] for m in range(nmerge)])
        rhs = [jnp.concatenate([be[:, hs] * eg[:, hs] * k[:, hs], be[:, hs] * v[:, hs]], axis=1) for hs in heads]
        wu = [r + _dotp(y, r) for y, r in zip(ys, rhs)]
        ps = [a[0:c] * e * incl for a, e in zip(qk, dec)]
        ss = [st_ref[h] for h in range(nh)]
        ws = [_dot(jnp.concatenate([x[:, 0:GD_DK], qdec[:, hs]], axis=0), s)
              for x, hs, s in zip(wu, heads, ss)]
        vnew = [x[:, GD_DK:2 * GD_DK] - y[0:c] for x, y in zip(wu, ws)]
        outs = [y[c:2 * c] + _dot(p, vn) for y, p, vn in zip(ws, ps, vnew)]
        snew = [gend[:, hs] * s + _dot_tn(kdec[:, hs], vn) for hs, s, vn in zip(heads, ss, vnew)]
        for h, hs in enumerate(heads):
            o_ref[pl.ds(r0, c), hs] = outs[h]
            st_ref[h] = snew[h]
        return carry

    lax.fori_loop(0, tb // c, chunk, 0)

    @pl.when(jnp.logical_not(is_lat))
    def _():
        sfin_ref[...] = st_ref[...]


def _gdn_scan(qkv, proj, misc_col, exa, exb, alog, dtb, s0, layer, tables, n_ctx, tb, bw):
    t = proj.shape[0]
    nh = bw // GD_DK
    blk, first, seq = tables
    nb = blk.shape[1]
    c = GD_C
    tri = np.stack([_tri_np(c, False), _tri_np(c, True)]).astype(np.float32)
    inv = _inverse_masks_np(c, c)
    msk = np.stack([np.stack(list(_causal_np(c, c, rev)) + inv) for rev in (False, True)]).astype(np.float32)
    nmerge = len(inv) - 2
    kern = functools.partial(_gdn_kernel, n_ctx=n_ctx, tb=tb, bw=bw, nmerge=nmerge)
    row = lambda col: (lambda d, j, b, f, s: (b[d, j], col))
    grid_spec = pltpu.PrefetchScalarGridSpec(
        num_scalar_prefetch=3,
        grid=(2, nb),
        in_specs=[pl.BlockSpec((tb, bw), row(0)),
                  pl.BlockSpec((tb, bw), row(1)),
                  pl.BlockSpec((tb, bw), row(2)),
                  pl.BlockSpec((tb, MISC_W), row(misc_col)),
                  pl.BlockSpec((None, LANES, bw), lambda d, j, b, f, s: (d, 0, 0)),
                  pl.BlockSpec((None, LANES, bw), lambda d, j, b, f, s: (d, 0, 0)),
                  pl.BlockSpec((None, 1, bw), lambda d, j, b, f, s: (d, 0, 0)),
                  pl.BlockSpec((None, 1, bw), lambda d, j, b, f, s: (d, 0, 0)),
                  pl.BlockSpec((None, None, None, nh, GD_DK, GD_DK),
                               lambda d, j, b, f, s: (jnp.maximum(s[d, j] - n_ctx, 0), layer, d, 0, 0, 0)),
                  pl.BlockSpec((None, c, c), lambda d, j, b, f, s: (d, 0, 0)),
                  pl.BlockSpec((None,) + msk.shape[1:], lambda d, j, b, f, s: (d, 0, 0, 0))],
        out_specs=[pl.BlockSpec((None, tb, bw), lambda d, j, b, f, s: (d, b[d, j], 0)),
                   pl.BlockSpec((None, None, nh, GD_DK, GD_DK),
                                lambda d, j, b, f, s: (jnp.minimum(s[d, j], n_ctx - 1), d, 0, 0, 0))],
        scratch_shapes=[pltpu.VMEM((nh, GD_DK, GD_DK), f32), pltpu.VMEM((tb, bw), f32),
                        pltpu.VMEM((tb, bw), f32)],
    )
    return pl.pallas_call(
        kern, grid_spec=grid_spec,
        out_shape=[jax.ShapeDtypeStruct((2, t, bw), f32),
                   jax.ShapeDtypeStruct((n_ctx, 2, nh, GD_DK, GD_DK), f32)],
        compiler_params=_cparams(("arbitrary", "arbitrary")),
        name="gdn_scan",
    )(jnp.asarray(blk), jnp.asarray(first), jnp.asarray(seq), qkv, qkv, qkv, proj, exa, exb, alog, dtb, s0,
      jnp.asarray(tri), jnp.asarray(msk))


def _rwkv_prep_kernel(r_ref, k_ref, v_ref, misc_ref, w0_ref, w2_ref, a0_ref, a2_ref, g2_ref, kkp_ref,
                      ka_ref, rk_ref, b64_ref, kk_ref, lw_ref, icl_ref, bonus_ref, og_ref):
    rr = r_ref[...]
    rk = k_ref[...]
    wl = jnp.tanh(misc_ref[:, 0:LANES])
    al = misc_ref[:, LANES:2 * LANES]
    gl = _sigmoid(misc_ref[:, 2 * LANES:3 * LANES])
    ka = ka_ref[...]
    bon = jnp.zeros_like(rr)
    for d in range(2):
        logw = -_softplus(-(w0_ref[d:d + 1, :] + _dot(wl, w2_ref[d]))) - 0.5
        lw_ref[d] = -jnp.exp(logw)
        icl = _sigmoid(a0_ref[d:d + 1, :] + _dot(al, a2_ref[d]))
        icl_ref[d] = icl
        bon = bon + rr * (rk * (1.0 + (icl - 1.0) * ka)) * rk_ref[...]
    og_ref[...] = _dot(gl, g2_ref[...])
    b64 = b64_ref[...]
    kx = rk * kkp_ref[...]
    kk_ref[...] = kx * lax.rsqrt(_rdot01(kx * kx, b64) + NORM_EPS)
    bonus_ref[...] = _rdot01(bon, b64) * v_ref[...]


def _rwkv_prep(proj, rkv_col, misc_col, w0, w2p, a0, a2p, g2p, kkp, ka, rkp, b64, tb, bw):
    t = proj.shape[0]
    full = lambda shape: pl.BlockSpec(shape, lambda r: (0,) * len(shape))
    return pl.pallas_call(
        _rwkv_prep_kernel,
        grid=(t // tb,),
        in_specs=[pl.BlockSpec((tb, bw), lambda r: (r, rkv_col)),
                  pl.BlockSpec((tb, bw), lambda r: (r, rkv_col + 1)),
                  pl.BlockSpec((tb, bw), lambda r: (r, rkv_col + 2)),
                  pl.BlockSpec((tb, MISC_W), lambda r: (r, misc_col)),
                  full((2, bw)), full((2, LANES, bw)), full((2, bw)), full((2, LANES, bw)),
                  full((LANES, bw)), full((1, bw)), full((1, bw)), full((1, bw)), full((bw, bw))],
        out_specs=[pl.BlockSpec((tb, bw), lambda r: (r, 0)),
                   pl.BlockSpec((2, tb, bw), lambda r: (0, r, 0)),
                   pl.BlockSpec((2, tb, bw), lambda r: (0, r, 0)),
                   pl.BlockSpec((tb, bw), lambda r: (r, 0)),
                   pl.BlockSpec((tb, bw), lambda r: (r, 0))],
        out_shape=[jax.ShapeDtypeStruct((t, bw), f32), jax.ShapeDtypeStruct((2, t, bw), f32),
                   jax.ShapeDtypeStruct((2, t, bw), f32), jax.ShapeDtypeStruct((t, bw), f32),
                   jax.ShapeDtypeStruct((t, bw), f32)],
        compiler_params=_cparams(("arbitrary",)),
        name="rwkv_prep",
    )(proj, proj, proj, proj, w0, w2p, a0, a2p, g2p, kkp, ka, rkp, b64)


def _rwkv_kernel(blk_ref, first_ref, seq_ref, r_ref, k_ref, v_ref, kk_ref, lw_ref, icl_ref, ka_ref, s0_ref,
                 tri_ref, msk_ref, o_ref, sfin_ref, st_ref, *, n_ctx, tb, bw, nmerge):
    d = pl.program_id(0)
    j = pl.program_id(1)
    c = RW_C
    npair = bw // LANES
    is_lat = seq_ref[d, j] >= n_ctx

    @pl.when(first_ref[d, j] == 1)
    def _():
        st_ref[...] = jnp.where(is_lat, s0_ref[...], 0.0)

    tri = tri_ref[...].astype(bf16)
    strict = msk_ref[0]
    incl2 = jnp.concatenate([msk_ref[1], msk_ref[1]], axis=1)
    mdiag = msk_ref[3]
    lane = lax.broadcasted_iota(jnp.int32, (c, LANES), 1)
    m_even = jnp.where(lane < RW_DK, 1.0, 0.0)
    m_odd = 1.0 - m_even
    ka = ka_ref[...]

    def stack2(x):
        return jnp.concatenate([x * m_even, x * m_odd], axis=0)

    def chunk(ci, carry):
        ce = jnp.where(d == 0, ci, tb // c - 1 - ci)
        r0 = pl.multiple_of(ce * c, c)
        lw = lw_ref[pl.ds(r0, c), :]
        icl = icl_ref[pl.ds(r0, c), :]
        kk = kk_ref[pl.ds(r0, c), :]
        rk = k_ref[pl.ds(r0, c), :]
        rr = r_ref[pl.ds(r0, c), :]
        rv = v_ref[pl.ds(r0, c), :]
        cum = _ldot01(tri, lw)
        ctot = jnp.sum(lw, axis=0, keepdims=True)
        kdir = rk * (1.0 + (icl - 1.0) * ka)
        bvec = kk * icl
        einv = jnp.exp(-cum)
        at = (-kk) * jnp.exp(cum - lw)
        kt = kdir * einv
        bt = bvec * einv
        rt = rr * jnp.exp(cum)
        edec = jnp.exp(ctot - cum)
        kdec = kdir * edec
        bdec = bvec * edec
        etot = jnp.exp(ctot)
        pairs = [slice(p * LANES, (p + 1) * LANES) for p in range(npair)]
        n2c = 2 * c
        a2 = [stack2(at[:, ps]) for ps in pairs]
        r2 = [stack2(rt[:, ps]) for ps in pairs]
        v2 = [stack2(rv[:, ps]) for ps in pairs]
        kb2 = [jnp.concatenate([stack2(kt[:, ps]), stack2(bt[:, ps])], axis=0) for ps in pairs]
        kbd2 = [jnp.concatenate([stack2(kdec[:, ps]), stack2(bdec[:, ps])], axis=0) for ps in pairs]
        big = [_dot_nt(jnp.concatenate([a, r], axis=0), kb) for a, r, kb in zip(a2, r2, kb2)]
        aak = [x[0:n2c, 0:n2c] * strict for x in big]
        ns = [-(x[0:n2c, n2c:2 * n2c] * strict) for x in big]
        arkb = [x[n2c:2 * n2c] * incl2 for x in big]
        ys = _tri_inverse_y(ns, mdiag, [msk_ref[4 + m] for m in range(nmerge)])
        rhs = [jnp.concatenate([a, _dot(m, v)], axis=1) for a, m, v in zip(a2, aak, v2)]
        tat = [x + _dotp(y, x) for y, x in zip(ys, rhs)]
        sts = [st_ref[p] for p in range(npair)]
        ur = [_dot_nt(jnp.concatenate([x[:, 0:LANES], r], axis=0), st) for x, r, st in zip(tat, r2, sts)]
        u2 = [y[0:n2c] + x[:, LANES:2 * LANES] for y, x in zip(ur, tat)]
        vu = [jnp.concatenate([v, u], axis=0) for v, u in zip(v2, u2)]
        y2 = [y[n2c:2 * n2c] + _dot(m, x) for y, m, x in zip(ur, arkb, vu)]
        snew = [st * etot[:, ps] + _dot_tn(x, kbd) for st, ps, x, kbd in zip(sts, pairs, vu, kbd2)]
        for p, ps in enumerate(pairs):
            o_ref[pl.ds(r0, c), ps] = y2[p][0:c] + y2[p][c:n2c]
            st_ref[p] = snew[p]
        return carry

    lax.fori_loop(0, tb // c, chunk, 0)

    @pl.when(jnp.logical_not(is_lat))
    def _():
        sfin_ref[...] = st_ref[...]


def _rwkv_scan(proj, rkv_col, kk, lw, icl, ka, s0p, layer, tables, n_ctx, tb, bw):
    t = proj.shape[0]
    npair = bw // LANES
    blk, first, seq = tables
    nb = blk.shape[1]
    c = RW_C
    n = 2 * c
    tri = np.stack([_tri_np(c, False), _tri_np(c, True)]).astype(np.float32)
    inv = _inverse_masks_np(n, c)
    msk = np.stack([np.stack(list(_causal_np(n, c, rev)) + inv) for rev in (False, True)]).astype(np.float32)
    nmerge = len(inv) - 2
    kern = functools.partial(_rwkv_kernel, n_ctx=n_ctx, tb=tb, bw=bw, nmerge=nmerge)
    row = lambda col: (lambda d, j, b, f, s: (b[d, j], col))
    drow = lambda d, j, b, f, s: (d, b[d, j], 0)
    grid_spec = pltpu.PrefetchScalarGridSpec(
        num_scalar_prefetch=3,
        grid=(2, nb),
        in_specs=[pl.BlockSpec((tb, bw), row(rkv_col)),
                  pl.BlockSpec((tb, bw), row(rkv_col + 1)),
                  pl.BlockSpec((tb, bw), row(rkv_col + 2)),
                  pl.BlockSpec((tb, bw), row(0)),
                  pl.BlockSpec((None, tb, bw), drow),
                  pl.BlockSpec((None, tb, bw), drow),
                  pl.BlockSpec((1, bw), lambda d, j, b, f, s: (0, 0)),
                  pl.BlockSpec((None, None, None, npair, LANES, LANES),
                               lambda d, j, b, f, s: (jnp.maximum(s[d, j] - n_ctx, 0), layer, d, 0, 0, 0)),
                  pl.BlockSpec((None, c, c), lambda d, j, b, f, s: (d, 0, 0)),
                  pl.BlockSpec((None,) + msk.shape[1:], lambda d, j, b, f, s: (d, 0, 0, 0))],
        out_specs=[pl.BlockSpec((None, tb, bw), drow),
                   pl.BlockSpec((None, None, npair, LANES, LANES),
                                lambda d, j, b, f, s: (jnp.minimum(s[d, j], n_ctx - 1), d, 0, 0, 0))],
        scratch_shapes=[pltpu.VMEM((npair, LANES, LANES), f32)],
    )
    return pl.pallas_call(
        kern, grid_spec=grid_spec,
        out_shape=[jax.ShapeDtypeStruct((2, t, bw), f32),
                   jax.ShapeDtypeStruct((n_ctx, 2, npair, LANES, LANES), f32)],
        compiler_params=_cparams(("arbitrary", "arbitrary")),
        name="rwkv_scan",
    )(jnp.asarray(blk), jnp.asarray(first), jnp.asarray(seq), proj, proj, proj, kk, lw, icl, ka, s0p,
      jnp.asarray(tri), jnp.asarray(msk))


def _merge_kernel(x_ref, mod_ref, ohg_ref, ogd_ref, orw_ref, hgg_ref, gdz_ref, bonus_ref, og_ref,
                  g0_ref, g1_ref, g2_ref, hgn_ref, gdn_ref, lng_ref, lnb_ref, wb_ref, wo_ref, o_ref, *, bw):
    def head_mean(x, width):
        outs = []
        for g in range(bw // LANES):
            xg = x[:, g * LANES:(g + 1) * LANES]
            if width == LANES:
                outs.append(jnp.broadcast_to(jnp.mean(xg, axis=-1, keepdims=True), xg.shape))
            else:
                lane = lax.broadcasted_iota(jnp.int32, xg.shape, 1)
                lo = lane < width
                s_lo = jnp.sum(jnp.where(lo, xg, 0.0), axis=-1, keepdims=True)
                s_hi = jnp.sum(jnp.where(lo, 0.0, xg), axis=-1, keepdims=True)
                outs.append(jnp.where(lo, s_lo, s_hi) * (1.0 / width))
        return jnp.concatenate(outs, axis=-1)

    ohg = ohg_ref[0] + ohg_ref[1]
    ohg = ohg * lax.rsqrt(head_mean(ohg * ohg, HG_DK) + NORM_EPS) * hgn_ref[...]
    ohg = ohg * _silu(hgg_ref[...])
    ogd = ogd_ref[0] + ogd_ref[1]
    ogd = ogd * lax.rsqrt(head_mean(ogd * ogd, GD_DK) + NORM_EPS) * gdn_ref[...]
    ogd = ogd * _silu(gdz_ref[...])
    orw = orw_ref[0] + orw_ref[1]
    mu = head_mean(orw, RW_DK)
    cen = orw - mu
    var = head_mean(cen * cen, RW_DK)
    orw = cen * lax.rsqrt(var + RW_GN_EPS) * lng_ref[...] + lnb_ref[...]
    orw = (orw + bonus_ref[...]) * og_ref[...]
    merged = (_sigmoid(g0_ref[...]) * _dot(ohg, wb_ref[0])
              + _sigmoid(g1_ref[...]) * _dot(ogd, wb_ref[1])
              + _sigmoid(g2_ref[...]) * _dot(orw, wb_ref[2]))
    o_ref[...] = x_ref[...] + mod_ref[5:6, :] * _dot(merged, wo_ref[...])


def _merge(x, mod, proj, ohg, ogd, orw, bonus, og, hgn, gdn, lng, lnb, wb, wo, cond_of_tile, tm, bw,
           hgg_col, gdz_col, gate_col):
    t, d = x.shape
    row = lambda col: (lambda i: (i, col))
    drow = pl.BlockSpec((2, tm, bw), lambda i: (0, i, 0))
    vec = pl.BlockSpec((1, bw), lambda i: (0, 0))
    once = pl.Buffered(1)
    return pl.pallas_call(
        functools.partial(_merge_kernel, bw=bw),
        grid=(t // tm,),
        in_specs=[pl.BlockSpec((tm, d), row(0)),
                  pl.BlockSpec((None, N_MOD, d), lambda i: (cond_of_tile(i), 0, 0)),
                  drow, drow, drow,
                  pl.BlockSpec((tm, bw), row(hgg_col)),
                  pl.BlockSpec((tm, bw), row(gdz_col)),
                  pl.BlockSpec((tm, bw), row(0)),
                  pl.BlockSpec((tm, bw), row(0)),
                  pl.BlockSpec((tm, d), row(gate_col)),
                  pl.BlockSpec((tm, d), row(gate_col + 1)),
                  pl.BlockSpec((tm, d), row(gate_col + 2)),
                  vec, vec, vec, vec,
                  pl.BlockSpec((3, bw, d), lambda i: (0, 0, 0), pipeline_mode=once),
                  pl.BlockSpec((d, d), lambda i: (0, 0), pipeline_mode=once)],
        out_specs=pl.BlockSpec((tm, d), row(0)),
        out_shape=jax.ShapeDtypeStruct((t, d), f32),
        compiler_params=_cparams(("arbitrary",)),
        name="merge",
    )(x, mod, ohg, ogd, orw, proj, proj, bonus, og, proj, proj, proj, hgn, gdn, lng, lnb, wb, wo)


def _final_norm_kernel(x_ref, g_ref, o_ref):
    x = x_ref[...]
    ms = jnp.mean(x * x, axis=-1, keepdims=True)
    o_ref[...] = x * lax.rsqrt(ms + NORM_EPS) * g_ref[...]


def _final_norm(x, g, tm):
    t, d = x.shape
    return pl.pallas_call(
        _final_norm_kernel,
        grid=(t // tm,),
        in_specs=[pl.BlockSpec((tm, d), lambda i: (i, 0)), pl.BlockSpec((1, d), lambda i: (0, 0))],
        out_specs=pl.BlockSpec((tm, d), lambda i: (i, 0)),
        out_shape=jax.ShapeDtypeStruct((t, d), f32),
        compiler_params=_cparams(("arbitrary",)),
        name="final_norm",
    )(x, g.reshape(1, d))


def _pad_rows(w, rows):
    return jnp.concatenate([w, jnp.zeros((rows - w.shape[0],) + w.shape[1:], w.dtype)], axis=0)


def kernel(x_prompt, x_sample, state_hgrn, state_gdn, state_rwkv, c, c_ctx, mod_w, mod_b, norm_g, ffn_up,
           ffn_down, mix_in, hgrn_lb, hgrn_norm, gdn_conv, gdn_a_log, gdn_dt_bias, gdn_norm, rwkv_w0, rwkv_w2,
           rwkv_a0, rwkv_a2, rwkv_g2, rwkv_kk, rwkv_ka, rwkv_rk, rwkv_ln_g, rwkv_ln_b, mix_branch, mix_out,
           final_norm):
    bc, lc, d = x_prompt.shape
    bs, ls, _ = x_sample.shape
    depth = mod_w.shape[0]
    bw = d // 4
    tb = lc
    assert tb % GD_C == 0 and ls % tb == 0 and bw % LANES == 0 and tb % GRID_W == 0
    n_ctx = bc
    bps = ls // tb
    tc = bc * lc
    t = tc + bs * ls
    hg_h, gd_h, rw_h = bw // HG_DK, bw // GD_DK, bw // RW_DK
    lw_w, la_w, lg_w = rwkv_w2.shape[2], rwkv_a2.shape[2], rwkv_g2.shape[1]
    assert max(2 * lw_w, 2 * la_w, lg_w, 4 * gd_h) <= LANES

    tm = 512 if (tc % 512 == 0 and ls % 512 == 0) else tb
    cond_of_tile = lambda tile_rows: (lambda i: jnp.where(i * tile_rows < tc, 0, 1 + (i * tile_rows - tc) // ls))

    x = jnp.concatenate([x_prompt.reshape(tc, d), x_sample.reshape(bs * ls, d)], axis=0)
    cond = jnp.concatenate([c_ctx[None, :], c, jnp.zeros((N_COND_PAD - 1 - bs, d), f32)], axis=0)
    mod_all = _modulation(cond, mod_w, mod_b).reshape(depth, N_COND_PAD, N_MOD, d)

    lb_p = jax.nn.softmax(hgrn_lb.astype(f32), axis=0)
    lb_all = jnp.cumsum(lb_p, axis=0) - lb_p[0:1]

    o_hg, o_gdq, o_gdz, o_gda, o_rw, o_lo, o_mg = 0, 5 * bw, 8 * bw, 9 * bw, 9 * bw + 4 * gd_h, 0, 0
    o_rw = 9 * bw + 4 * gd_h
    o_lo = o_rw + 3 * bw
    o_mg = o_lo + 2 * lw_w + 2 * la_w + lg_w
    zpad = lambda n: jnp.zeros((depth, d, n), mix_in.dtype)
    misc = jnp.concatenate([
        mix_in[:, :, o_lo:o_lo + 2 * lw_w], zpad(LANES - 2 * lw_w),
        mix_in[:, :, o_lo + 2 * lw_w:o_lo + 2 * lw_w + 2 * la_w], zpad(LANES - 2 * la_w),
        mix_in[:, :, o_lo + 2 * lw_w + 2 * la_w:o_mg], zpad(LANES - lg_w),
        mix_in[:, :, o_gda:o_gda + 4 * gd_h], zpad(LANES - 4 * gd_h)], axis=2)
    w_in = jnp.concatenate([mix_in[:, :, 0:9 * bw], mix_in[:, :, o_rw:o_rw + 3 * bw],
                            mix_in[:, :, o_mg:o_mg + 3 * d], misc], axis=2).astype(bf16)
    col_gdq, col_gdz, col_rw, col_gate = 5, 8, 9, 3
    col_misc = (24 * bw) // MISC_W
    assert (24 * bw) % MISC_W == 0

    up16 = ffn_up.astype(bf16)
    down16 = ffn_down.astype(bf16)
    wb16 = mix_branch.astype(bf16)
    wo16 = mix_out.astype(bf16)

    exa = np.zeros((2, LANES, bw), np.float32)
    exb = np.zeros((2, LANES, bw), np.float32)
    for dd in range(2):
        for h in range(gd_h):
            exa[dd, dd * gd_h + h, h * GD_DK:(h + 1) * GD_DK] = 1.0
            exb[dd, 2 * gd_h + dd * gd_h + h, h * GD_DK:(h + 1) * GD_DK] = 1.0
    exa, exb = jnp.asarray(exa), jnp.asarray(exb)
    b64 = jnp.asarray(np.kron(np.eye(rw_h), np.ones((RW_DK, RW_DK))).astype(np.float32))

    tables = _block_tables_np(n_ctx, bs, bps)

    s0_hg = jnp.swapaxes(state_hgrn, -1, -2)
    s0_gd = state_gdn
    eye2 = jnp.eye(2, dtype=f32)
    s0_rw = state_rwkv.reshape(bs, depth, 2, rw_h // 2, 2, RW_DK, RW_DK)
    s0_rw = jnp.einsum('bldpeij,ef->bldpeifj', s0_rw, eye2).reshape(bs, depth, 2, rw_h // 2, LANES, LANES)

    new_hg, new_gd, new_rw = [], [], []
    for l in range(depth):
        mod = mod_all[l]
        x = _ffn(x, mod, norm_g[l], up16[l, 0], down16[l, 0], 0, cond_of_tile(tm), tm)
        proj = _mixin(x, mod, norm_g[l], w_in[l], cond_of_tile(tm), tm)

        ohg, s_hg = _hgrn_scan(proj, lb_all[l].reshape(2, 1, bw), s0_hg, l, tables, n_ctx, tb, bw)

        conv_w = gdn_conv[l].reshape(CONV_K * CONV_K, 3 * bw)
        qkv = _gdn_prep(proj, conv_w, n_ctx, bps, tb, bw, col_gdq)
        alog = jnp.repeat(gdn_a_log[l], GD_DK, axis=-1).reshape(2, 1, bw)
        dtb = jnp.repeat(gdn_dt_bias[l], GD_DK, axis=-1).reshape(2, 1, bw)
        ogd, s_gd = _gdn_scan(qkv, proj, col_misc, exa, exb, alog, dtb, s0_gd, l, tables, n_ctx, tb, bw)

        w2p = jnp.stack([_pad_rows(jnp.concatenate([jnp.zeros((dd * lw_w, bw), f32), rwkv_w2[l, dd]], 0), LANES)
                         for dd in range(2)])
        a2p = jnp.stack([_pad_rows(jnp.concatenate([jnp.zeros((dd * la_w, bw), f32), rwkv_a2[l, dd]], 0), LANES)
                         for dd in range(2)])
        g2p = _pad_rows(rwkv_g2[l], LANES)
        ka = rwkv_ka[l].reshape(1, bw)
        kk, lw, icl, bonus, og = _rwkv_prep(proj, col_rw, col_misc, rwkv_w0[l], w2p, rwkv_a0[l], a2p, g2p,
                                            rwkv_kk[l].reshape(1, bw), ka, rwkv_rk[l].reshape(1, bw), b64,
                                            tb, bw)
        orw, s_rw = _rwkv_scan(proj, col_rw, kk, lw, icl, ka, s0_rw, l, tables, n_ctx, tb, bw)

        x = _merge(x, mod, proj, ohg, ogd, orw, bonus, og, hgrn_norm[l].reshape(1, bw),
                   jnp.tile(gdn_norm[l], gd_h).reshape(1, bw), rwkv_ln_g[l].reshape(1, bw),
                   rwkv_ln_b[l].reshape(1, bw), wb16[l], wo16[l], cond_of_tile(tb), tb, bw,
                   4, col_gdz, col_gate)
        x = _ffn(x, mod, norm_g[l], up16[l, 1], down16[l, 1], 2, cond_of_tile(tm), tm)

        new_hg.append(jnp.swapaxes(s_hg, -1, -2))
        new_gd.append(s_gd)
        s_rw = s_rw.reshape(n_ctx, 2, rw_h // 2, 2, RW_DK, 2, RW_DK)
        new_rw.append(jnp.einsum('bdpeifj,ef->bdpeij', s_rw, eye2).reshape(n_ctx, 2, rw_h, RW_DK, RW_DK))

    y = _final_norm(x, final_norm, tm)
    return (y[:tc].reshape(bc, lc, d), y[tc:].reshape(bs, ls, d),
            jnp.stack(new_hg, axis=1), jnp.stack(new_gd, axis=1), jnp.stack(new_rw, axis=1))
```

```python
import functools

import numpy as np
import jax
import jax.numpy as jnp
from jax import lax
from jax.experimental import pallas as pl
from jax.experimental.pallas import tpu as pltpu

f32 = jnp.float32
bf16 = jnp.bfloat16

LANES = 128
SUBLANES = 8
VMEM_LIMIT = 60 * 1024 * 1024

HG_DK = 128
GD_DK = 128
RW_DK = 64
N_MOD = 9
GRID_W = 64
CONV_K = 3
NORM_EPS = 1e-6
RW_GN_EPS = 64e-5
GATE_FLOOR = 1e-30
N_COND_PAD = 16

HG_C = 64
GD_C = 128
RW_C = 64
SUB = 8
INV_PASSES = 1
MISC_W = 4 * LANES
TM_FFN = 512
TM_PROJ = 1024


def _sigmoid(x):
    return 1.0 / (1.0 + jnp.exp(-x))


def _silu(x):
    return x * _sigmoid(x)


def _softplus(x):
    return jnp.maximum(x, 0.0) + jnp.log(1.0 + jnp.exp(-jnp.abs(x)))


def _dot(a, b):
    return jnp.dot(a.astype(bf16), b.astype(bf16), preferred_element_type=f32)


def _dot_nt(a, b):
    return lax.dot_general(a.astype(bf16), b.astype(bf16), (((1,), (1,)), ((), ())),
                           preferred_element_type=f32)


def _dot_tn(a, b):
    return lax.dot_general(a.astype(bf16), b.astype(bf16), (((0,), (0,)), ((), ())),
                           preferred_element_type=f32)


def _split3(x):
    hi = x.astype(bf16)
    r = x - hi.astype(f32)
    mid = r.astype(bf16)
    lo = (r - mid.astype(f32)).astype(bf16)
    return hi, mid, lo


def _ldot01(m, x):
    hi, mid, lo = _split3(x)
    m = m.astype(bf16)
    d = functools.partial(jnp.dot, preferred_element_type=f32)
    return d(m, hi) + d(m, mid) + d(m, lo)


def _rdot01(x, m):
    hi, mid, lo = _split3(x)
    m = m.astype(bf16)
    d = functools.partial(jnp.dot, preferred_element_type=f32)
    return d(hi, m) + d(mid, m) + d(lo, m)


def _dotp(a, b):
    if INV_PASSES == 1:
        return _dot(a, b)
    ah = a.astype(bf16)
    al = (a - ah.astype(f32)).astype(bf16)
    bh = b.astype(bf16)
    bl = (b - bh.astype(f32)).astype(bf16)
    d = functools.partial(jnp.dot, preferred_element_type=f32)
    return d(ah, bh) + d(ah, bl) + d(al, bh)


def _tri_inverse_y(ns, mdiag, moffs):
    nd = [n * m for n, m in zip(ns, mdiag)]
    n2 = [_dotp(a, a) for a in nd]
    n4 = [_dotp(a, a) for a in n2]
    y = [b - a - _dotp(a, b) for a, b in zip(nd, n2)]
    y = [a + b + _dotp(a, b) for a, b in zip(y, n4)]
    for lvl in range(len(moffs[0])):
        xl = [n * m[lvl] for n, m in zip(ns, moffs)]
        xl = [l + _dotp(a, l) for a, l in zip(y, xl)]
        y = [a - l - _dotp(l, a) for a, l in zip(y, xl)]
    return y


def _normmod(x, g, scale, shift):
    ms = jnp.mean(x * x, axis=-1, keepdims=True)
    return (x * lax.rsqrt(ms + NORM_EPS) * g) * (1.0 + scale) + shift


def _tri_np(c, rev):
    i = np.arange(c)
    return (i[None, :] >= i[:, None]) if rev else (i[None, :] <= i[:, None])


def _inverse_masks_np(n, c):
    i = np.arange(n)
    same = (i[:, None] // c) == (i[None, :] // c)
    eye = np.eye(n, dtype=bool)
    mdiag = (i[:, None] // SUB) == (i[None, :] // SUB)
    out = [eye, mdiag]
    m = SUB
    while m < c:
        out.append(same & ((i[:, None] // (2 * m)) == (i[None, :] // (2 * m)))
                   & ((i[:, None] // m) != (i[None, :] // m)))
        m *= 2
    return out


def _causal_np(n, c, rev):
    i = np.arange(n)
    same = (i[:, None] // c) == (i[None, :] // c)
    il, jl = (i % c)[:, None], (i % c)[None, :]
    strict = same & ((jl > il) if rev else (jl < il))
    incl = same & ((jl >= il) if rev else (jl <= il))
    return strict, incl


def _hgrn_tables_np(c):
    levels = []
    m = SUB
    while m < c:
        levels.append(m)
        m *= 2
    stacks, pmasks = [], []
    t = np.arange(c)
    for rev in (False, True):
        mats = [_tri_np(c, rev)]
        pms = []
        for m in levels:
            pair = t // (2 * m)
            half = (t // m) % 2
            tau = t[None, :]
            if not rev:
                bd = (pair * 2 * m + m - 1)[:, None]
                qside = (half == 1)[:, None]
                mat = np.where(qside, (tau > bd) & (tau <= t[:, None]), (tau > t[:, None]) & (tau <= bd))
                pm = (pair[:, None] == pair[None, :]) & (half == 1)[:, None] & (half == 0)[None, :]
            else:
                bd = (pair * 2 * m + m)[:, None]
                qside = (half == 0)[:, None]
                mat = np.where(qside, (tau >= t[:, None]) & (tau < bd), (tau >= bd) & (tau < t[:, None]))
                pm = (pair[:, None] == pair[None, :]) & (half == 0)[:, None] & (half == 1)[None, :]
            mats.append(mat)
            pms.append(pm)
        stacks.append(np.concatenate(mats, axis=0))
        pmasks.append(np.stack(pms, axis=0))
    return (np.stack(stacks).astype(np.float32), np.stack(pmasks).astype(np.float32), len(levels))


def _block_tables_np(n_ctx, n_lat_seq, bps):
    nb = n_ctx + n_lat_seq * bps
    blk = np.zeros((2, nb), np.int32)
    first = np.zeros((nb,), np.int32)
    seq = np.zeros((nb,), np.int32)
    j = 0
    for s in range(n_ctx):
        blk[0, j] = blk[1, j] = s
        first[j], seq[j] = 1, s
        j += 1
    for s in range(n_lat_seq):
        for p in range(bps):
            blk[0, j] = n_ctx + s * bps + p
            blk[1, j] = n_ctx + s * bps + bps - 1 - p
            first[j], seq[j] = int(p == 0), n_ctx + s
            j += 1
    return blk, first, seq


def _pick(n, cap):
    best = LANES
    t = LANES
    while t <= min(n, cap):
        if n % t == 0:
            best = t
        t += LANES
    return best


def _cparams(sem):
    return pltpu.CompilerParams(dimension_semantics=sem, vmem_limit_bytes=VMEM_LIMIT)


def _mod_kernel(cond_ref, w_ref, b_ref, o_ref):
    x = cond_ref[...]
    o_ref[...] = _dot(_silu(x), w_ref[...]) + b_ref[...]


def _modulation(cond, mod_w, mod_b):
    depth, d, nd = mod_w.shape
    tn = _pick(nd, 1024)
    return pl.pallas_call(
        _mod_kernel,
        grid=(depth, nd // tn),
        in_specs=[pl.BlockSpec((N_COND_PAD, d), lambda l, j: (0, 0)),
                  pl.BlockSpec((None, d, tn), lambda l, j: (l, 0, j)),
                  pl.BlockSpec((None, 1, tn), lambda l, j: (l, 0, j))],
        out_specs=pl.BlockSpec((None, N_COND_PAD, tn), lambda l, j: (l, 0, j)),
        out_shape=jax.ShapeDtypeStruct((depth, N_COND_PAD, nd), f32),
        compiler_params=_cparams(("arbitrary", "arbitrary")),
        name="modulation",
    )(cond, mod_w, mod_b.reshape(depth, 1, nd))


def _ffn_kernel(x_ref, mod_ref, g_ref, wa_ref, wb_ref, wd_ref, o_ref, h_ref, *, slot, nf):
    f = pl.program_id(1)

    @pl.when(f == 0)
    def _():
        h = _normmod(x_ref[...], g_ref[slot:slot + 1, :], mod_ref[3 * slot + 1:3 * slot + 2, :],
                     mod_ref[3 * slot:3 * slot + 1, :])
        h_ref[...] = h.astype(bf16)
        o_ref[...] = jnp.zeros_like(o_ref)

    h = h_ref[...]
    a = jnp.dot(h, wa_ref[...], preferred_element_type=f32)
    b = jnp.dot(h, wb_ref[...], preferred_element_type=f32)
    act = (_silu(a) * b).astype(bf16)
    o_ref[...] += jnp.dot(act, wd_ref[...], preferred_element_type=f32)

    @pl.when(f == nf - 1)
    def _():
        o_ref[...] = x_ref[...] + 0.5 * mod_ref[3 * slot + 2:3 * slot + 3, :] * o_ref[...]


def _ffn(x, mod, norm_g, w_up, w_down, layer, slot, cond_of_tile, tm):
    t, d = x.shape
    dff = w_down.shape[2]
    tf = _pick(dff, 512)
    nf = dff // tf
    which = slot // 2
    return pl.pallas_call(
        functools.partial(_ffn_kernel, slot=slot, nf=nf),
        grid=(t // tm, nf),
        in_specs=[pl.BlockSpec((tm, d), lambda i, f: (i, 0)),
                  pl.BlockSpec((None, None, N_MOD, d), lambda i, f: (layer, cond_of_tile(i), 0, 0)),
                  pl.BlockSpec((None, 3, d), lambda i, f: (layer, 0, 0)),
                  pl.BlockSpec((None, None, d, tf), lambda i, f: (layer, which, 0, f)),
                  pl.BlockSpec((None, None, d, tf), lambda i, f: (layer, which, 0, f + nf)),
                  pl.BlockSpec((None, None, tf, d), lambda i, f: (layer, which, f, 0))],
        out_specs=pl.BlockSpec((tm, d), lambda i, f: (i, 0)),
        out_shape=jax.ShapeDtypeStruct((t, d), f32),
        scratch_shapes=[pltpu.VMEM((tm, d), bf16)],
        compiler_params=_cparams(("arbitrary", "arbitrary")),
        name="ffn%d" % slot,
    )(x, mod, norm_g, w_up, w_up, w_down)


def _mixin_kernel(x_ref, mod_ref, g_ref, w_ref, o_ref, h_ref):
    @pl.when(pl.program_id(1) == 0)
    def _():
        h = _normmod(x_ref[...], g_ref[1:2, :], mod_ref[4:5, :], mod_ref[3:4, :])
        h_ref[...] = h.astype(bf16)

    o_ref[...] = jnp.dot(h_ref[...], w_ref[...], preferred_element_type=f32)


def _mixin(x, mod, norm_g, w, layer, cond_of_tile, tm):
    t, d = x.shape
    nw = w.shape[2]
    tn = _pick(nw, 640)
    return pl.pallas_call(
        _mixin_kernel,
        grid=(t // tm, nw // tn),
        in_specs=[pl.BlockSpec((tm, d), lambda i, j: (i, 0)),
                  pl.BlockSpec((None, None, N_MOD, d), lambda i, j: (layer, cond_of_tile(i), 0, 0)),
                  pl.BlockSpec((None, 3, d), lambda i, j: (layer, 0, 0)),
                  pl.BlockSpec((None, d, tn), lambda i, j: (layer, 0, j))],
        out_specs=pl.BlockSpec((tm, tn), lambda i, j: (i, j)),
        out_shape=jax.ShapeDtypeStruct((t, nw), f32),
        scratch_shapes=[pltpu.VMEM((tm, d), bf16)],
        compiler_params=_cparams(("arbitrary", "arbitrary")),
        name="mix_in",
    )(x, mod, norm_g, w)


def _dir_row(d, col):
    return lambda j, b, f, s: (b[d, j], col)


def _dir_slab(d):
    return lambda j, b, f, s: (d, b[d, j], 0)


def _const_spec(shape):
    return pl.BlockSpec(shape, lambda j, b, f, s: (0,) * len(shape))


def _state_in_spec(tail, n_ctx, layer):
    return pl.BlockSpec((None, None) + tail,
                        lambda j, b, f, s: (jnp.maximum(s[j] - n_ctx, 0), layer) + (0,) * len(tail))


def _state_out_spec(tail, n_ctx):
    return pl.BlockSpec((None,) + tail, lambda j, b, f, s: (jnp.minimum(s[j], n_ctx - 1),) + (0,) * len(tail))


def _chunk_rows(ci, nc, c):
    return (pl.multiple_of(ci * c, c), pl.multiple_of((nc - 1 - ci) * c, c))


def _hgrn_kernel(blk_ref, first_ref, seq_ref, q0_ref, v0_ref, f0_ref, q1_ref, v1_ref, f1_ref, lb_ref, s0_ref,
                 mst_ref, pm_ref, o0_ref, o1_ref, sfin_ref, st_ref, qs_ref, ks_ref, lf_ref,
                 *, n_ctx, tb, bw, nlev):
    j = pl.program_id(0)
    c = HG_C
    nc = tb // c
    nh = bw // HG_DK
    is_lat = seq_ref[j] >= n_ctx
    q_refs, v_refs, f_refs, o_refs = (q0_ref, q1_ref), (v0_ref, v1_ref), (f0_ref, f1_ref), (o0_ref, o1_ref)

    @pl.when(first_ref[j] == 1)
    def _():
        st_ref[...] = jnp.where(is_lat, s0_ref[...], 0.0)

    for d in range(2):
        lb = lb_ref[d]
        fl = f_refs[d][...]
        lf_ref[d] = jnp.log(jnp.maximum(lb + (1.0 - lb) * _sigmoid(fl), GATE_FLOOR))
        ks_ref[d] = (1.0 - lb) * _sigmoid(-fl)
        qs_ref[d] = _silu(q_refs[d][...]) * HG_DK ** -0.5

    mst = [mst_ref[d].astype(bf16) for d in range(2)]
    tl = lax.broadcasted_iota(jnp.int32, (c, bw), 0) % SUB
    heads = [slice(h * HG_DK, (h + 1) * HG_DK) for h in range(nh)]
    items = [(d, h) for d in range(2) for h in range(nh)]

    def chunk(ci, carry):
        r0 = _chunk_rows(ci, nc, c)
        qe, kd, etot, qm, km, vs, accs = [], [], [], [], [], [], []
        for d in range(2):
            rows = pl.ds(r0[d], c)
            lf = lf_ref[d, rows, :]
            q = qs_ref[d, rows, :]
            k = ks_ref[d, rows, :]
            v = v_refs[d][rows, :]
            xs = _ldot01(mst[d], lf)
            b = xs[0:c]
            btot = jnp.sum(lf, axis=0, keepdims=True)
            qe.append(q * jnp.exp(b))
            kd.append(k * jnp.exp(btot - b))
            etot.append(jnp.exp(btot))
            es = [jnp.exp(xs[(lv + 1) * c:(lv + 2) * c]) for lv in range(nlev)]
            qm.append([q * e for e in es])
            km.append([k * e for e in es])
            vs.append(v)
            b3 = b.reshape(c // SUB, SUB, bw)
            k3 = k.reshape(c // SUB, SUB, bw)
            v3 = v.reshape(c // SUB, SUB, bw)
            acc = [jnp.zeros((c, HG_DK), f32) for _ in range(nh)]
            for sl in range(SUB):
                bs = jnp.broadcast_to(b3[:, sl:sl + 1, :], b3.shape).reshape(c, bw)
                kk = jnp.broadcast_to(k3[:, sl:sl + 1, :], k3.shape).reshape(c, bw)
                vv = jnp.broadcast_to(v3[:, sl:sl + 1, :], v3.shape).reshape(c, bw)
                valid = (tl >= sl) if d == 0 else (tl <= sl)
                p = jnp.where(valid, q * kk * jnp.exp(jnp.minimum(b - bs, 0.0)), 0.0)
                for h, hs in enumerate(heads):
                    acc[h] = acc[h] + jnp.sum(p[:, hs], axis=-1, keepdims=True) * vv[:, hs]
            accs.append(acc)
        sts = [st_ref[d, h] for d, h in items]
        inter = [_dot_nt(qe[d][:, heads[h]], st) for (d, h), st in zip(items, sts)]
        sc = [jnp.zeros((c, c), f32) for _ in items]
        for lv in range(nlev):
            sc = [s + _dot_nt(qm[d][lv][:, heads[h]], km[d][lv][:, heads[h]]) * pm_ref[d, lv]
                  for (d, h), s in zip(items, sc)]
        outs = [a + accs[d][h] + _dot(s, vs[d][:, heads[h]]) for (d, h), a, s in zip(items, inter, sc)]
        snew = [st * etot[d][:, heads[h]] + _dot_tn(vs[d][:, heads[h]], kd[d][:, heads[h]])
                for (d, h), st in zip(items, sts)]
        for (d, h), o, s in zip(items, outs, snew):
            o_refs[d][pl.ds(r0[d], c), heads[h]] = o
            st_ref[d, h] = s
        return carry

    lax.fori_loop(0, nc, chunk, 0)

    @pl.when(jnp.logical_not(is_lat))
    def _():
        sfin_ref[...] = st_ref[...]


def _hgrn_scan(proj, lb, s0t, layer, tables, n_ctx, tb, bw):
    t = proj.shape[0]
    nh = bw // HG_DK
    blk, first, seq = tables
    nb = blk.shape[1]
    mst, pm, nlev = _hgrn_tables_np(HG_C)
    kern = functools.partial(_hgrn_kernel, n_ctx=n_ctx, tb=tb, bw=bw, nlev=nlev)
    tail = (2, nh, HG_DK, HG_DK)
    dir_specs = [pl.BlockSpec((tb, bw), _dir_row(d, col)) for d in range(2) for col in (0, 1, 2 + d)]
    grid_spec = pltpu.PrefetchScalarGridSpec(
        num_scalar_prefetch=3,
        grid=(nb,),
        in_specs=dir_specs + [_const_spec((2, 1, bw)), _state_in_spec(tail, n_ctx, layer),
                              _const_spec(mst.shape), _const_spec(pm.shape)],
        out_specs=[pl.BlockSpec((tb, bw), _dir_row(0, 0)), pl.BlockSpec((tb, bw), _dir_row(1, 0)),
                   _state_out_spec(tail, n_ctx)],
        scratch_shapes=[pltpu.VMEM(tail, f32), pltpu.VMEM((2, tb, bw), f32),
                        pltpu.VMEM((2, tb, bw), f32), pltpu.VMEM((2, tb, bw), f32)],
    )
    return pl.pallas_call(
        kern, grid_spec=grid_spec,
        out_shape=[jax.ShapeDtypeStruct((t, bw), f32), jax.ShapeDtypeStruct((t, bw), f32),
                   jax.ShapeDtypeStruct((n_ctx,) + tail, f32)],
        compiler_params=_cparams(("arbitrary",)),
        name="hgrn_scan",
    )(jnp.asarray(blk), jnp.asarray(first), jnp.asarray(seq), *([proj] * 6), lb, s0t,
      jnp.asarray(mst), jnp.asarray(pm))


def _gdn_prep_kernel(prev_ref, cur_ref, nxt_ref, w_ref, o_ref, xe_ref, *, n_ctx, bps, tb, bw, halo):
    r = pl.program_id(0)
    part = pl.program_id(1)
    is_lat = r >= n_ctx
    pos = jnp.where(is_lat, (r - n_ctx) % bps, 0)
    pv = jnp.where(jnp.logical_and(is_lat, pos > 0), 1.0, 0.0)
    nv = jnp.where(jnp.logical_and(is_lat, pos < bps - 1), 1.0, 0.0)
    lat = jnp.where(is_lat, 1.0, 0.0)
    xe_ref[0:halo, :] = prev_ref[...] * pv
    xe_ref[halo:halo + tb, :] = cur_ref[...]
    xe_ref[halo + tb:halo + tb + halo, :] = nxt_ref[...] * nv
    col = lax.broadcasted_iota(jnp.int32, (tb, LANES), 0) % GRID_W
    m_lo = jnp.where(col >= 1, 1.0, 1.0 - lat)
    m_hi = jnp.where(col <= GRID_W - 2, 1.0, 1.0 - lat)
    for g in range(bw // LANES):
        gs = slice(g * LANES, (g + 1) * LANES)
        acc = None
        for dc in (-1, 0, 1):
            col_sum = None
            for dr in (-1, 0, 1):
                off = halo + GRID_W * dr + dc
                tap = (dr + 1) * CONV_K + (dc + 1)
                w = w_ref[tap:tap + 1, gs]
                if dr != 0:
                    w = w * lat
                term = xe_ref[off:off + tb, gs] * w
                col_sum = term if col_sum is None else col_sum + term
            if dc == -1:
                col_sum = col_sum * m_lo
            elif dc == 1:
                col_sum = col_sum * m_hi
            acc = col_sum if acc is None else acc + col_sum
        y = _silu(acc)
        ss = jnp.sum(y * y, axis=-1, keepdims=True)
        nrm = y * lax.rsqrt(ss + NORM_EPS)
        scale = jnp.where(part == 0, GD_DK ** -0.5, 1.0)
        o_ref[:, gs] = jnp.where(part == 2, y, nrm * scale)


def _gdn_prep(proj, conv_w, n_ctx, bps, tb, bw, qkv_col):
    t = proj.shape[0]
    nb = t // tb
    halo = LANES
    hb = tb // halo
    nhb = t // halo
    kern = functools.partial(_gdn_prep_kernel, n_ctx=n_ctx, bps=bps, tb=tb, bw=bw, halo=halo)
    return pl.pallas_call(
        kern,
        grid=(nb, 3),
        in_specs=[pl.BlockSpec((halo, bw), lambda r, p: (jnp.maximum(r * hb - 1, 0), qkv_col + p)),
                  pl.BlockSpec((tb, bw), lambda r, p: (r, qkv_col + p)),
                  pl.BlockSpec((halo, bw), lambda r, p: (jnp.minimum((r + 1) * hb, nhb - 1), qkv_col + p)),
                  pl.BlockSpec((CONV_K * CONV_K, bw), lambda r, p: (0, p))],
        out_specs=pl.BlockSpec((tb, bw), lambda r, p: (r, p)),
        out_shape=jax.ShapeDtypeStruct((t, 3 * bw), f32),
        scratch_shapes=[pltpu.VMEM((tb + 2 * halo, bw), f32)],
        compiler_params=_cparams(("arbitrary", "arbitrary")),
        name="gdn_prep",
    )(proj, proj, proj, conv_w)


def _gdn_kernel(blk_ref, first_ref, seq_ref, q0_ref, k0_ref, v0_ref, m0_ref, q1_ref, k1_ref, v1_ref, m1_ref,
                exa_ref, exb_ref, alog_ref, dtb_ref, s0_ref, tri_ref, msk_ref, o0_ref, o1_ref, sfin_ref,
                st_ref, lg_ref, be_ref, *, n_ctx, tb, bw, nmerge):
    j = pl.program_id(0)
    c = GD_C
    nc = tb // c
    nh = bw // GD_DK
    is_lat = seq_ref[j] >= n_ctx
    q_refs, k_refs, v_refs = (q0_ref, q1_ref), (k0_ref, k1_ref), (v0_ref, v1_ref)
    m_refs, o_refs = (m0_ref, m1_ref), (o0_ref, o1_ref)

    @pl.when(first_ref[j] == 1)
    def _():
        st_ref[...] = jnp.where(is_lat, s0_ref[...], 0.0)

    for d in range(2):
        ab = m_refs[d][:, MISC_W - LANES:MISC_W]
        a_in = _rdot01(ab, exa_ref[d])
        b_in = _rdot01(ab, exb_ref[d])
        lg_ref[d] = -jnp.exp(alog_ref[d]) * _softplus(a_in + dtb_ref[d])
        be_ref[d] = _sigmoid(b_in)

    tri = [tri_ref[d].astype(bf16) for d in range(2)]
    heads = [slice(h * GD_DK, (h + 1) * GD_DK) for h in range(nh)]
    items = [(d, h) for d in range(2) for h in range(nh)]
    strict = [msk_ref[d, 0] for d, _ in items]
    incl = [msk_ref[d, 1] for d, _ in items]
    mdiag = [msk_ref[d, 3] for d, _ in items]
    moffs = [[msk_ref[d, 4 + m] for m in range(nmerge)] for d, _ in items]

    def chunk(ci, carry):
        r0 = _chunk_rows(ci, nc, c)
        be, q, k, v, gcum, eg, qdec, kdec, gend = [], [], [], [], [], [], [], [], []
        for d in range(2):
            rows = pl.ds(r0[d], c)
            lg = lg_ref[d, rows, :]
            be.append(be_ref[d, rows, :])
            q.append(q_refs[d][rows, :])
            k.append(k_refs[d][rows, :])
            v.append(v_refs[d][rows, :])
            g = _ldot01(tri[d], lg)
            gtot = jnp.sum(lg, axis=0, keepdims=True)
            gcum.append(g)
            eg.append(jnp.exp(g))
            qdec.append(q[d] * eg[d])
            kdec.append(k[d] * jnp.exp(gtot - g))
            gend.append(jnp.exp(gtot))
        qk = [_dot_nt(jnp.concatenate([q[d][:, heads[h]], k[d][:, heads[h]]], axis=0), k[d][:, heads[h]])
              for d, h in items]
        dec = []
        for d, h in items:
            gc = gcum[d][:, heads[h]]
            dec.append(jnp.exp(jnp.minimum(gc - gc.T, 0.0)))
        ns = [be[d][:, heads[h]] * a[c:2 * c] * e * m for (d, h), a, e, m in zip(items, qk, dec, strict)]
        ys = _tri_inverse_y(ns, mdiag, moffs)
        rhs = [jnp.concatenate([be[d][:, heads[h]] * eg[d][:, heads[h]] * k[d][:, heads[h]],
                                be[d][:, heads[h]] * v[d][:, heads[h]]], axis=1) for d, h in items]
        wu = [r + _dotp(y, r) for y, r in zip(ys, rhs)]
        ps = [a[0:c] * e * m for a, e, m in zip(qk, dec, incl)]
        ss = [st_ref[d, h] for d, h in items]
        ws = [_dot(jnp.concatenate([x[:, 0:GD_DK], qdec[d][:, heads[h]]], axis=0), s)
              for (d, h), x, s in zip(items, wu, ss)]
        vnew = [x[:, GD_DK:2 * GD_DK] - y[0:c] for x, y in zip(wu, ws)]
        outs = [y[c:2 * c] + _dot(p, vn) for y, p, vn in zip(ws, ps, vnew)]
        snew = [gend[d][:, heads[h]] * s + _dot_tn(kdec[d][:, heads[h]], vn)
                for (d, h), s, vn in zip(items, ss, vnew)]
        for (d, h), o, s in zip(items, outs, snew):
            o_refs[d][pl.ds(r0[d], c), heads[h]] = o
            st_ref[d, h] = s
        return carry

    lax.fori_loop(0, nc, chunk, 0)

    @pl.when(jnp.logical_not(is_lat))
    def _():
        sfin_ref[...] = st_ref[...]


def _gdn_scan(qkv, proj, misc_col, exa, exb, alog, dtb, s0, layer, tables, n_ctx, tb, bw):
    t = proj.shape[0]
    nh = bw // GD_DK
    blk, first, seq = tables
    nb = blk.shape[1]
    c = GD_C
    tri = np.stack([_tri_np(c, False), _tri_np(c, True)]).astype(np.float32)
    inv = _inverse_masks_np(c, c)
    msk = np.stack([np.stack(list(_causal_np(c, c, rev)) + inv) for rev in (False, True)]).astype(np.float32)
    nmerge = len(inv) - 2
    kern = functools.partial(_gdn_kernel, n_ctx=n_ctx, tb=tb, bw=bw, nmerge=nmerge)
    tail = (2, nh, GD_DK, GD_DK)
    dir_specs = []
    for d in range(2):
        dir_specs += [pl.BlockSpec((tb, bw), _dir_row(d, col)) for col in range(3)]
        dir_specs += [pl.BlockSpec((tb, MISC_W), _dir_row(d, misc_col))]
    grid_spec = pltpu.PrefetchScalarGridSpec(
        num_scalar_prefetch=3,
        grid=(nb,),
        in_specs=dir_specs + [_const_spec((2, LANES, bw)), _const_spec((2, LANES, bw)), _const_spec((2, 1, bw)),
                              _const_spec((2, 1, bw)), _state_in_spec(tail, n_ctx, layer),
                              _const_spec(tri.shape), _const_spec(msk.shape)],
        out_specs=[pl.BlockSpec((tb, bw), _dir_row(0, 0)), pl.BlockSpec((tb, bw), _dir_row(1, 0)),
                   _state_out_spec(tail, n_ctx)],
        scratch_shapes=[pltpu.VMEM(tail, f32), pltpu.VMEM((2, tb, bw), f32), pltpu.VMEM((2, tb, bw), f32)],
    )
    return pl.pallas_call(
        kern, grid_spec=grid_spec,
        out_shape=[jax.ShapeDtypeStruct((t, bw), f32), jax.ShapeDtypeStruct((t, bw), f32),
                   jax.ShapeDtypeStruct((n_ctx,) + tail, f32)],
        compiler_params=_cparams(("arbitrary",)),
        name="gdn_scan",
    )(jnp.asarray(blk), jnp.asarray(first), jnp.asarray(seq), qkv, qkv, qkv, proj, qkv, qkv, qkv, proj,
      exa, exb, alog, dtb, s0, jnp.asarray(tri), jnp.asarray(msk))


def _rwkv_prep_kernel(r_ref, k_ref, v_ref, misc_ref, w0_ref, w2_ref, a0_ref, a2_ref, g2_ref, kkp_ref,
                      ka_ref, rk_ref, b64_ref, kk_ref, lw_ref, icl_ref, bonus_ref, og_ref):
    rr = r_ref[...]
    rk = k_ref[...]
    wl = jnp.tanh(misc_ref[:, 0:LANES])
    al = misc_ref[:, LANES:2 * LANES]
    gl = _sigmoid(misc_ref[:, 2 * LANES:3 * LANES])
    ka = ka_ref[...]
    bon = jnp.zeros_like(rr)
    for d in range(2):
        logw = -_softplus(-(w0_ref[d:d + 1, :] + _dot(wl, w2_ref[d]))) - 0.5
        lw_ref[d] = -jnp.exp(logw)
        icl = _sigmoid(a0_ref[d:d + 1, :] + _dot(al, a2_ref[d]))
        icl_ref[d] = icl
        bon = bon + rr * (rk * (1.0 + (icl - 1.0) * ka)) * rk_ref[...]
    og_ref[...] = _dot(gl, g2_ref[...])
    b64 = b64_ref[...]
    kx = rk * kkp_ref[...]
    kk_ref[...] = kx * lax.rsqrt(_rdot01(kx * kx, b64) + NORM_EPS)
    bonus_ref[...] = _rdot01(bon, b64) * v_ref[...]


def _rwkv_prep(proj, rkv_col, misc_col, w0, w2p, a0, a2p, g2p, kkp, ka, rkp, b64, tb, bw):
    t = proj.shape[0]
    full = lambda shape: pl.BlockSpec(shape, lambda r: (0,) * len(shape))
    return pl.pallas_call(
        _rwkv_prep_kernel,
        grid=(t // tb,),
        in_specs=[pl.BlockSpec((tb, bw), lambda r: (r, rkv_col)),
                  pl.BlockSpec((tb, bw), lambda r: (r, rkv_col + 1)),
                  pl.BlockSpec((tb, bw), lambda r: (r, rkv_col + 2)),
                  pl.BlockSpec((tb, MISC_W), lambda r: (r, misc_col)),
                  full((2, bw)), full((2, LANES, bw)), full((2, bw)), full((2, LANES, bw)),
                  full((LANES, bw)), full((1, bw)), full((1, bw)), full((1, bw)), full((bw, bw))],
        out_specs=[pl.BlockSpec((tb, bw), lambda r: (r, 0)),
                   pl.BlockSpec((2, tb, bw), lambda r: (0, r, 0)),
                   pl.BlockSpec((2, tb, bw), lambda r: (0, r, 0)),
                   pl.BlockSpec((tb, bw), lambda r: (r, 0)),
                   pl.BlockSpec((tb, bw), lambda r: (r, 0))],
        out_shape=[jax.ShapeDtypeStruct((t, bw), f32), jax.ShapeDtypeStruct((2, t, bw), f32),
                   jax.ShapeDtypeStruct((2, t, bw), f32), jax.ShapeDtypeStruct((t, bw), f32),
                   jax.ShapeDtypeStruct((t, bw), f32)],
        compiler_params=_cparams(("arbitrary",)),
        name="rwkv_prep",
    )(proj, proj, proj, proj, w0, w2p, a0, a2p, g2p, kkp, ka, rkp, b64)


def _rwkv_kernel(blk_ref, first_ref, seq_ref, r0_ref, k0_ref, v0_ref, kk0_ref, lw0_ref, icl0_ref,
                 r1_ref, k1_ref, v1_ref, kk1_ref, lw1_ref, icl1_ref, ka_ref, s0_ref, tri_ref, msk_ref,
                 o0_ref, o1_ref, sfin_ref, st_ref, *, n_ctx, tb, bw, nmerge):
    j = pl.program_id(0)
    c = RW_C
    nc = tb // c
    n2c = 2 * c
    npair = bw // LANES
    is_lat = seq_ref[j] >= n_ctx
    r_refs, k_refs, v_refs = (r0_ref, r1_ref), (k0_ref, k1_ref), (v0_ref, v1_ref)
    kk_refs, lw_refs, icl_refs, o_refs = (kk0_ref, kk1_ref), (lw0_ref, lw1_ref), (icl0_ref, icl1_ref), (o0_ref, o1_ref)

    @pl.when(first_ref[j] == 1)
    def _():
        st_ref[...] = jnp.where(is_lat, s0_ref[...], 0.0)

    tri = [tri_ref[d].astype(bf16) for d in range(2)]
    pairs = [slice(p * LANES, (p + 1) * LANES) for p in range(npair)]
    items = [(d, p) for d in range(2) for p in range(npair)]
    strict = [msk_ref[d, 0] for d, _ in items]
    incl2 = [jnp.concatenate([msk_ref[d, 1], msk_ref[d, 1]], axis=1) for d in range(2)]
    mdiag = [msk_ref[d, 3] for d, _ in items]
    moffs = [[msk_ref[d, 4 + m] for m in range(nmerge)] for d, _ in items]
    lane = lax.broadcasted_iota(jnp.int32, (c, LANES), 1)
    m_even = jnp.where(lane < RW_DK, 1.0, 0.0)
    m_odd = 1.0 - m_even
    ka = ka_ref[...]

    def stack2(x):
        return jnp.concatenate([x * m_even, x * m_odd], axis=0)

    def chunk(ci, carry):
        r0 = _chunk_rows(ci, nc, c)
        at, kt, bt, rt, rv, kdec, bdec, etot = [], [], [], [], [], [], [], []
        for d in range(2):
            rows = pl.ds(r0[d], c)
            lw = lw_refs[d][rows, :]
            icl = icl_refs[d][rows, :]
            kk = kk_refs[d][rows, :]
            cum = _ldot01(tri[d], lw)
            ctot = jnp.sum(lw, axis=0, keepdims=True)
            kdir = k_refs[d][rows, :] * (1.0 + (icl - 1.0) * ka)
            bvec = kk * icl
            einv = jnp.exp(-cum)
            edec = jnp.exp(ctot - cum)
            at.append((-kk) * jnp.exp(cum - lw))
            kt.append(kdir * einv)
            bt.append(bvec * einv)
            rt.append(r_refs[d][rows, :] * jnp.exp(cum))
            rv.append(v_refs[d][rows, :])
            kdec.append(kdir * edec)
            bdec.append(bvec * edec)
            etot.append(jnp.exp(ctot))
        a2 = [stack2(at[d][:, pairs[p]]) for d, p in items]
        r2 = [stack2(rt[d][:, pairs[p]]) for d, p in items]
        v2 = [stack2(rv[d][:, pairs[p]]) for d, p in items]
        kb2 = [jnp.concatenate([stack2(kt[d][:, pairs[p]]), stack2(bt[d][:, pairs[p]])], axis=0) for d, p in items]
        kbd2 = [jnp.concatenate([stack2(kdec[d][:, pairs[p]]), stack2(bdec[d][:, pairs[p]])], axis=0)
                for d, p in items]
        big = [_dot_nt(jnp.concatenate([a, r], axis=0), kb) for a, r, kb in zip(a2, r2, kb2)]
        aak = [x[0:n2c, 0:n2c] * m for x, m in zip(big, strict)]
        ns = [-(x[0:n2c, n2c:2 * n2c] * m) for x, m in zip(big, strict)]
        arkb = [x[n2c:2 * n2c] * incl2[d] for x, (d, _) in zip(big, items)]
        ys = _tri_inverse_y(ns, mdiag, moffs)
        rhs = [jnp.concatenate([a, _dot(m, v)], axis=1) for a, m, v in zip(a2, aak, v2)]
        tat = [x + _dotp(y, x) for y, x in zip(ys, rhs)]
        sts = [st_ref[d, p] for d, p in items]
        ur = [_dot_nt(jnp.concatenate([x[:, 0:LANES], r], axis=0), st) for x, r, st in zip(tat, r2, sts)]
        u2 = [y[0:n2c] + x[:, LANES:2 * LANES] for y, x in zip(ur, tat)]
        vu = [jnp.concatenate([v, u], axis=0) for v, u in zip(v2, u2)]
        y2 = [y[n2c:2 * n2c] + _dot(m, x) for y, m, x in zip(ur, arkb, vu)]
        snew = [st * etot[d][:, pairs[p]] + _dot_tn(x, kbd) for (d, p), st, x, kbd in zip(items, sts, vu, kbd2)]
        for (d, p), y, s in zip(items, y2, snew):
            o_refs[d][pl.ds(r0[d], c), pairs[p]] = y[0:c] + y[c:n2c]
            st_ref[d, p] = s
        return carry

    lax.fori_loop(0, nc, chunk, 0)

    @pl.when(jnp.logical_not(is_lat))
    def _():
        sfin_ref[...] = st_ref[...]


def _rwkv_scan(proj, rkv_col, kk, lw, icl, ka, s0p, layer, tables, n_ctx, tb, bw):
    t = proj.shape[0]
    npair = bw // LANES
    blk, first, seq = tables
    nb = blk.shape[1]
    c = RW_C
    n = 2 * c
    tri = np.stack([_tri_np(c, False), _tri_np(c, True)]).astype(np.float32)
    inv = _inverse_masks_np(n, c)
    msk = np.stack([np.stack(list(_causal_np(n, c, rev)) + inv) for rev in (False, True)]).astype(np.float32)
    nmerge = len(inv) - 2
    kern = functools.partial(_rwkv_kernel, n_ctx=n_ctx, tb=tb, bw=bw, nmerge=nmerge)
    tail = (2, npair, LANES, LANES)
    dir_specs, dir_args = [], []
    for d in range(2):
        dir_specs += [pl.BlockSpec((tb, bw), _dir_row(d, rkv_col + i)) for i in range(3)]
        dir_specs += [pl.BlockSpec((tb, bw), _dir_row(d, 0)),
                      pl.BlockSpec((None, tb, bw), _dir_slab(d)), pl.BlockSpec((None, tb, bw), _dir_slab(d))]
        dir_args += [proj, proj, proj, kk, lw, icl]
    grid_spec = pltpu.PrefetchScalarGridSpec(
        num_scalar_prefetch=3,
        grid=(nb,),
        in_specs=dir_specs + [_const_spec((1, bw)), _state_in_spec(tail, n_ctx, layer),
                              _const_spec(tri.shape), _const_spec(msk.shape)],
        out_specs=[pl.BlockSpec((tb, bw), _dir_row(0, 0)), pl.BlockSpec((tb, bw), _dir_row(1, 0)),
                   _state_out_spec(tail, n_ctx)],
        scratch_shapes=[pltpu.VMEM(tail, f32)],
    )
    return pl.pallas_call(
        kern, grid_spec=grid_spec,
        out_shape=[jax.ShapeDtypeStruct((t, bw), f32), jax.ShapeDtypeStruct((t, bw), f32),
                   jax.ShapeDtypeStruct((n_ctx,) + tail, f32)],
        compiler_params=_cparams(("arbitrary",)),
        name="rwkv_scan",
    )(jnp.asarray(blk), jnp.asarray(first), jnp.asarray(seq), *dir_args, ka, s0p,
      jnp.asarray(tri), jnp.asarray(msk))


def _merge_kernel(x_ref, mod_ref, ohg0_ref, ohg1_ref, ogd0_ref, ogd1_ref, orw0_ref, orw1_ref, hgg_ref, gdz_ref,
                  bonus_ref, og_ref, g0_ref, g1_ref, g2_ref, hgn_ref, gdn_ref, lng_ref, lnb_ref, wb_ref, wo_ref,
                  o_ref, *, bw):
    def head_mean(x, width):
        outs = []
        for g in range(bw // LANES):
            xg = x[:, g * LANES:(g + 1) * LANES]
            if width == LANES:
                outs.append(jnp.broadcast_to(jnp.mean(xg, axis=-1, keepdims=True), xg.shape))
            else:
                lane = lax.broadcasted_iota(jnp.int32, xg.shape, 1)
                lo = lane < width
                s_lo = jnp.sum(jnp.where(lo, xg, 0.0), axis=-1, keepdims=True)
                s_hi = jnp.sum(jnp.where(lo, 0.0, xg), axis=-1, keepdims=True)
                outs.append(jnp.where(lo, s_lo, s_hi) * (1.0 / width))
        return jnp.concatenate(outs, axis=-1)

    ohg = ohg0_ref[...] + ohg1_ref[...]
    ohg = ohg * lax.rsqrt(head_mean(ohg * ohg, HG_DK) + NORM_EPS) * hgn_ref[...]
    ohg = ohg * _silu(hgg_ref[...])
    ogd = ogd0_ref[...] + ogd1_ref[...]
    ogd = ogd * lax.rsqrt(head_mean(ogd * ogd, GD_DK) + NORM_EPS) * gdn_ref[...]
    ogd = ogd * _silu(gdz_ref[...])
    orw = orw0_ref[...] + orw1_ref[...]
    mu = head_mean(orw, RW_DK)
    cen = orw - mu
    var = head_mean(cen * cen, RW_DK)
    orw = cen * lax.rsqrt(var + RW_GN_EPS) * lng_ref[...] + lnb_ref[...]
    orw = (orw + bonus_ref[...]) * og_ref[...]
    merged = (_sigmoid(g0_ref[...]) * _dot(ohg, wb_ref[0])
              + _sigmoid(g1_ref[...]) * _dot(ogd, wb_ref[1])
              + _sigmoid(g2_ref[...]) * _dot(orw, wb_ref[2]))
    o_ref[...] = x_ref[...] + mod_ref[5:6, :] * _dot(merged, wo_ref[...])


def _merge(x, mod, proj, branch_outs, bonus, og, hgn, gdn, lng, lnb, wb, wo, layer, cond_of_tile, tm, bw,
           hgg_col, gdz_col, gate_col):
    t, d = x.shape
    row = lambda col: (lambda i: (i, col))
    act = pl.BlockSpec((tm, bw), row(0))
    vec = pl.BlockSpec((1, bw), lambda i: (0, 0))
    once = pl.Buffered(1)
    return pl.pallas_call(
        functools.partial(_merge_kernel, bw=bw),
        grid=(t // tm,),
        in_specs=[pl.BlockSpec((tm, d), row(0)),
                  pl.BlockSpec((None, None, N_MOD, d), lambda i: (layer, cond_of_tile(i), 0, 0)),
                  act, act, act, act, act, act,
                  pl.BlockSpec((tm, bw), row(hgg_col)),
                  pl.BlockSpec((tm, bw), row(gdz_col)),
                  act, act,
                  pl.BlockSpec((tm, d), row(gate_col)),
                  pl.BlockSpec((tm, d), row(gate_col + 1)),
                  pl.BlockSpec((tm, d), row(gate_col + 2)),
                  vec, vec, vec, vec,
                  pl.BlockSpec((None, 3, bw, d), lambda i: (layer, 0, 0, 0), pipeline_mode=once),
                  pl.BlockSpec((None, d, d), lambda i: (layer, 0, 0), pipeline_mode=once)],
        out_specs=pl.BlockSpec((tm, d), row(0)),
        out_shape=jax.ShapeDtypeStruct((t, d), f32),
        compiler_params=_cparams(("arbitrary",)),
        name="merge",
    )(x, mod, *branch_outs, proj, proj, bonus, og, proj, proj, proj, hgn, gdn, lng, lnb, wb, wo)


def _final_norm_kernel(x_ref, g_ref, o_ref):
    x = x_ref[...]
    ms = jnp.mean(x * x, axis=-1, keepdims=True)
    o_ref[...] = x * lax.rsqrt(ms + NORM_EPS) * g_ref[...]


def _final_norm(x, g, tm):
    t, d = x.shape
    return pl.pallas_call(
        _final_norm_kernel,
        grid=(t // tm,),
        in_specs=[pl.BlockSpec((tm, d), lambda i: (i, 0)), pl.BlockSpec((1, d), lambda i: (0, 0))],
        out_specs=pl.BlockSpec((tm, d), lambda i: (i, 0)),
        out_shape=jax.ShapeDtypeStruct((t, d), f32),
        compiler_params=_cparams(("arbitrary",)),
        name="final_norm",
    )(x, g.reshape(1, d))


def _pad_rows(w, rows):
    return jnp.concatenate([w, jnp.zeros((rows - w.shape[0],) + w.shape[1:], w.dtype)], axis=0)


def kernel(x_prompt, x_sample, state_hgrn, state_gdn, state_rwkv, c, c_ctx, mod_w, mod_b, norm_g, ffn_up,
           ffn_down, mix_in, hgrn_lb, hgrn_norm, gdn_conv, gdn_a_log, gdn_dt_bias, gdn_norm, rwkv_w0, rwkv_w2,
           rwkv_a0, rwkv_a2, rwkv_g2, rwkv_kk, rwkv_ka, rwkv_rk, rwkv_ln_g, rwkv_ln_b, mix_branch, mix_out,
           final_norm):
    bc, lc, d = x_prompt.shape
    bs, ls, _ = x_sample.shape
    depth = mod_w.shape[0]
    bw = d // 4
    tb = lc
    assert tb % GD_C == 0 and ls % tb == 0 and bw % LANES == 0 and tb % GRID_W == 0
    assert bs + 1 <= N_COND_PAD
    n_ctx = bc
    bps = ls // tb
    tc = bc * lc
    t = tc + bs * ls
    gd_h, rw_h = bw // GD_DK, bw // RW_DK
    lw_w, la_w, lg_w = rwkv_w2.shape[2], rwkv_a2.shape[2], rwkv_g2.shape[1]
    assert max(2 * lw_w, 2 * la_w, lg_w, 4 * gd_h) <= LANES

    def token_tile(cap):
        rows = tb
        while rows * 2 <= cap and tc % (rows * 2) == 0 and ls % (rows * 2) == 0:
            rows *= 2
        return rows

    tm, tp = token_tile(TM_FFN), token_tile(TM_PROJ)
    cond_of_tile = lambda tile_rows: (lambda i: jnp.where(i * tile_rows < tc, 0, 1 + (i * tile_rows - tc) // ls))

    x = jnp.concatenate([x_prompt.reshape(tc, d), x_sample.reshape(bs * ls, d)], axis=0)
    cond = jnp.concatenate([c_ctx[None, :], c, jnp.zeros((N_COND_PAD - 1 - bs, d), f32)], axis=0)
    mod_all = _modulation(cond, mod_w, mod_b).reshape(depth, N_COND_PAD, N_MOD, d)

    lb_p = jax.nn.softmax(hgrn_lb.astype(f32), axis=0)
    lb_all = jnp.cumsum(lb_p, axis=0) - lb_p[0:1]

    o_gda = 9 * bw
    o_rw = 9 * bw + 4 * gd_h
    o_lo = o_rw + 3 * bw
    o_mg = o_lo + 2 * lw_w + 2 * la_w + lg_w
    zpad = lambda n: jnp.zeros((depth, d, n), mix_in.dtype)
    misc = jnp.concatenate([
        mix_in[:, :, o_lo:o_lo + 2 * lw_w], zpad(LANES - 2 * lw_w),
        mix_in[:, :, o_lo + 2 * lw_w:o_lo + 2 * lw_w + 2 * la_w], zpad(LANES - 2 * la_w),
        mix_in[:, :, o_lo + 2 * lw_w + 2 * la_w:o_mg], zpad(LANES - lg_w),
        mix_in[:, :, o_gda:o_gda + 4 * gd_h], zpad(LANES - 4 * gd_h)], axis=2)
    w_in = jnp.concatenate([mix_in[:, :, 0:9 * bw], mix_in[:, :, o_rw:o_rw + 3 * bw],
                            mix_in[:, :, o_mg:o_mg + 3 * d], misc], axis=2).astype(bf16)
    col_gdq, col_gdz, col_rw, col_gate = 5, 8, 9, 3
    col_misc = (24 * bw) // MISC_W
    assert (24 * bw) % MISC_W == 0

    up16 = ffn_up.astype(bf16)
    down16 = ffn_down.astype(bf16)
    wb16 = mix_branch.astype(bf16)
    wo16 = mix_out.astype(bf16)

    exa = np.zeros((2, LANES, bw), np.float32)
    exb = np.zeros((2, LANES, bw), np.float32)
    for dd in range(2):
        for h in range(gd_h):
            exa[dd, dd * gd_h + h, h * GD_DK:(h + 1) * GD_DK] = 1.0
            exb[dd, 2 * gd_h + dd * gd_h + h, h * GD_DK:(h + 1) * GD_DK] = 1.0
    exa, exb = jnp.asarray(exa), jnp.asarray(exb)
    b64 = jnp.asarray(np.kron(np.eye(rw_h), np.ones((RW_DK, RW_DK))).astype(np.float32))

    tables = _block_tables_np(n_ctx, bs, bps)

    s0_hg = jnp.swapaxes(state_hgrn, -1, -2)
    s0_gd = state_gdn
    eye2 = jnp.eye(2, dtype=f32)
    s0_rw = state_rwkv.reshape(bs, depth, 2, rw_h // 2, 2, RW_DK, RW_DK)
    s0_rw = jnp.einsum('bldpeij,ef->bldpeifj', s0_rw, eye2).reshape(bs, depth, 2, rw_h // 2, LANES, LANES)

    new_hg, new_gd, new_rw = [], [], []
    for l in range(depth):
        mod = mod_all
        x = _ffn(x, mod, norm_g, up16, down16, l, 0, cond_of_tile(tm), tm)
        proj = _mixin(x, mod, norm_g, w_in, l, cond_of_tile(tp), tp)

        ohg0, ohg1, s_hg = _hgrn_scan(proj, lb_all[l].reshape(2, 1, bw), s0_hg, l, tables, n_ctx, tb, bw)

        conv_w = gdn_conv[l].reshape(CONV_K * CONV_K, 3 * bw)
        qkv = _gdn_prep(proj, conv_w, n_ctx, bps, tb, bw, col_gdq)
        alog = jnp.repeat(gdn_a_log[l], GD_DK, axis=-1).reshape(2, 1, bw)
        dtb = jnp.repeat(gdn_dt_bias[l], GD_DK, axis=-1).reshape(2, 1, bw)
        ogd0, ogd1, s_gd = _gdn_scan(qkv, proj, col_misc, exa, exb, alog, dtb, s0_gd, l, tables, n_ctx, tb, bw)

        w2p = jnp.stack([_pad_rows(jnp.concatenate([jnp.zeros((dd * lw_w, bw), f32), rwkv_w2[l, dd]], 0), LANES)
                         for dd in range(2)])
        a2p = jnp.stack([_pad_rows(jnp.concatenate([jnp.zeros((dd * la_w, bw), f32), rwkv_a2[l, dd]], 0), LANES)
                         for dd in range(2)])
        g2p = _pad_rows(rwkv_g2[l], LANES)
        ka = rwkv_ka[l].reshape(1, bw)
        kk, lw, icl, bonus, og = _rwkv_prep(proj, col_rw, col_misc, rwkv_w0[l], w2p, rwkv_a0[l], a2p, g2p,
                                            rwkv_kk[l].reshape(1, bw), ka, rwkv_rk[l].reshape(1, bw), b64,
                                            tb, bw)
        orw0, orw1, s_rw = _rwkv_scan(proj, col_rw, kk, lw, icl, ka, s0_rw, l, tables, n_ctx, tb, bw)

        x = _merge(x, mod, proj, (ohg0, ohg1, ogd0, ogd1, orw0, orw1), bonus, og, hgrn_norm[l].reshape(1, bw),
                   jnp.tile(gdn_norm[l], gd_h).reshape(1, bw), rwkv_ln_g[l].reshape(1, bw),
                   rwkv_ln_b[l].reshape(1, bw), wb16, wo16, l, cond_of_tile(tb), tb, bw,
                   4, col_gdz, col_gate)
        x = _ffn(x, mod, norm_g, up16, down16, l, 2, cond_of_tile(tm), tm)

        new_hg.append(jnp.swapaxes(s_hg, -1, -2))
        new_gd.append(s_gd)
        s_rw = s_rw.reshape(n_ctx, 2, rw_h // 2, 2, RW_DK, 2, RW_DK)
        new_rw.append(jnp.einsum('bdpeifj,ef->bdpeij', s_rw, eye2).reshape(n_ctx, 2, rw_h, RW_DK, RW_DK))

    y = _final_norm(x, final_norm, tp)
    return (y[:tc].reshape(bc, lc, d), y[tc:].reshape(bs, ls, d),
            jnp.stack(new_hg, axis=1), jnp.stack(new_gd, axis=1), jnp.stack(new_rw, axis=1))
```

```python
import functools

import numpy as np
import jax
import jax.numpy as jnp
from jax import lax
from jax.experimental import pallas as pl
from jax.experimental.pallas import tpu as pltpu

f32 = jnp.float32
bf16 = jnp.bfloat16

LANES = 128
SUBLANES = 8
MXU_W = 256
VMEM_LIMIT = 60 * 1024 * 1024

HG_DK = 128
GD_DK = 128
RW_DK = 64
N_MOD = 9
GRID_W = 64
CONV_K = 3
NORM_EPS = 1e-6
RW_GN_EPS = 64e-5
GATE_FLOOR = 1e-30
N_COND_PAD = 16

HG_C = 64
GD_C = 128
RW_C = 64
SUB = 8
INV_PASSES = 1
MISC_W = 4 * LANES
TM_FFN = 512
TM_PROJ = 1024


def _sigmoid(x):
    return 1.0 / (1.0 + jnp.exp(-x))


def _silu(x):
    return x * _sigmoid(x)


def _softplus(x):
    return jnp.maximum(x, 0.0) + jnp.log(1.0 + jnp.exp(-jnp.abs(x)))


def _dot(a, b):
    return jnp.dot(a.astype(bf16), b.astype(bf16), preferred_element_type=f32)


def _dot_nt(a, b):
    return lax.dot_general(a.astype(bf16), b.astype(bf16), (((1,), (1,)), ((), ())),
                           preferred_element_type=f32)


def _dot_tn(a, b):
    return lax.dot_general(a.astype(bf16), b.astype(bf16), (((0,), (0,)), ((), ())),
                           preferred_element_type=f32)


def _split3(x):
    hi = x.astype(bf16)
    r = x - hi.astype(f32)
    mid = r.astype(bf16)
    lo = (r - mid.astype(f32)).astype(bf16)
    return hi, mid, lo


def _ldot01(m, x):
    hi, mid, lo = _split3(x)
    m = m.astype(bf16)
    d = functools.partial(jnp.dot, preferred_element_type=f32)
    return d(m, hi) + d(m, mid) + d(m, lo)


def _rdot01(x, m):
    hi, mid, lo = _split3(x)
    m = m.astype(bf16)
    d = functools.partial(jnp.dot, preferred_element_type=f32)
    return d(hi, m) + d(mid, m) + d(lo, m)


def _dotp(a, b):
    if INV_PASSES == 1:
        return _dot(a, b)
    ah = a.astype(bf16)
    al = (a - ah.astype(f32)).astype(bf16)
    bh = b.astype(bf16)
    bl = (b - bh.astype(f32)).astype(bf16)
    d = functools.partial(jnp.dot, preferred_element_type=f32)
    return d(ah, bh) + d(ah, bl) + d(al, bh)


def _tri_inverse_y(ns, mdiag, moffs):
    nd = [n * m for n, m in zip(ns, mdiag)]
    n2 = [_dotp(a, a) for a in nd]
    n4 = [_dotp(a, a) for a in n2]
    y = [b - a - _dotp(a, b) for a, b in zip(nd, n2)]
    y = [a + b + _dotp(a, b) for a, b in zip(y, n4)]
    for lvl in range(len(moffs[0])):
        xl = [n * m[lvl] for n, m in zip(ns, moffs)]
        xl = [l + _dotp(a, l) for a, l in zip(y, xl)]
        y = [a - l - _dotp(l, a) for a, l in zip(y, xl)]
    return y


def _normmod(x, g, scale, shift):
    ms = jnp.mean(x * x, axis=-1, keepdims=True)
    return (x * lax.rsqrt(ms + NORM_EPS) * g) * (1.0 + scale) + shift


def _tri_np(c, rev):
    i = np.arange(c)
    return (i[None, :] >= i[:, None]) if rev else (i[None, :] <= i[:, None])


def _inverse_masks_np(n, c):
    i = np.arange(n)
    same = (i[:, None] // c) == (i[None, :] // c)
    eye = np.eye(n, dtype=bool)
    mdiag = (i[:, None] // SUB) == (i[None, :] // SUB)
    out = [eye, mdiag]
    m = SUB
    while m < c:
        out.append(same & ((i[:, None] // (2 * m)) == (i[None, :] // (2 * m)))
                   & ((i[:, None] // m) != (i[None, :] // m)))
        m *= 2
    return out


def _causal_np(n, c, rev):
    i = np.arange(n)
    same = (i[:, None] // c) == (i[None, :] // c)
    il, jl = (i % c)[:, None], (i % c)[None, :]
    strict = same & ((jl > il) if rev else (jl < il))
    incl = same & ((jl >= il) if rev else (jl <= il))
    return strict, incl


def _hgrn_tables_np(c):
    levels = []
    m = SUB
    while m < c:
        levels.append(m)
        m *= 2
    stacks, pmasks = [], []
    t = np.arange(c)
    for rev in (False, True):
        mats = [_tri_np(c, rev)]
        pms = []
        for m in levels:
            pair = t // (2 * m)
            half = (t // m) % 2
            tau = t[None, :]
            if not rev:
                bd = (pair * 2 * m + m - 1)[:, None]
                qside = (half == 1)[:, None]
                mat = np.where(qside, (tau > bd) & (tau <= t[:, None]), (tau > t[:, None]) & (tau <= bd))
                pm = (pair[:, None] == pair[None, :]) & (half == 1)[:, None] & (half == 0)[None, :]
            else:
                bd = (pair * 2 * m + m)[:, None]
                qside = (half == 0)[:, None]
                mat = np.where(qside, (tau >= t[:, None]) & (tau < bd), (tau >= bd) & (tau < t[:, None]))
                pm = (pair[:, None] == pair[None, :]) & (half == 0)[:, None] & (half == 1)[None, :]
            mats.append(mat)
            pms.append(pm)
        stacks.append(np.concatenate(mats, axis=0))
        pmasks.append(np.stack(pms, axis=0))
    return (np.stack(stacks).astype(np.float32), np.stack(pmasks).astype(np.float32), len(levels))


def _block_tables_np(n_ctx, n_lat_seq, bps):
    nb = n_ctx + n_lat_seq * bps
    blk = np.zeros((2, nb), np.int32)
    first = np.zeros((nb,), np.int32)
    seq = np.zeros((nb,), np.int32)
    j = 0
    for s in range(n_ctx):
        blk[0, j] = blk[1, j] = s
        first[j], seq[j] = 1, s
        j += 1
    for s in range(n_lat_seq):
        for p in range(bps):
            blk[0, j] = n_ctx + s * bps + p
            blk[1, j] = n_ctx + s * bps + bps - 1 - p
            first[j], seq[j] = int(p == 0), n_ctx + s
            j += 1
    return blk, first, seq


def _pick(n, cap):
    best = LANES
    t = LANES
    while t <= min(n, cap):
        if n % t == 0:
            best = t
        t += LANES
    return best


def _pick_mxu(n, cap):
    best = 0
    t = MXU_W
    while t <= min(n, cap):
        if n % t == 0:
            best = t
        t += MXU_W
    return best or _pick(n, cap)


def _cparams(sem):
    return pltpu.CompilerParams(dimension_semantics=sem, vmem_limit_bytes=VMEM_LIMIT)


def _mod_kernel(cond_ref, w_ref, b_ref, o_ref):
    x = cond_ref[...]
    o_ref[...] = _dot(_silu(x), w_ref[...]) + b_ref[...]


def _modulation(cond, mod_w, mod_b):
    depth, d, nd = mod_w.shape
    tn = _pick(nd, 1024)
    return pl.pallas_call(
        _mod_kernel,
        grid=(depth, nd // tn),
        in_specs=[pl.BlockSpec((N_COND_PAD, d), lambda l, j: (0, 0)),
                  pl.BlockSpec((None, d, tn), lambda l, j: (l, 0, j)),
                  pl.BlockSpec((None, 1, tn), lambda l, j: (l, 0, j))],
        out_specs=pl.BlockSpec((None, N_COND_PAD, tn), lambda l, j: (l, 0, j)),
        out_shape=jax.ShapeDtypeStruct((depth, N_COND_PAD, nd), f32),
        compiler_params=_cparams(("arbitrary", "arbitrary")),
        name="modulation",
    )(cond, mod_w, mod_b.reshape(depth, 1, nd))


def _ffn_kernel(x_ref, mod_ref, g_ref, wa_ref, wb_ref, wd_ref, o_ref, h_ref, *, slot, nf):
    f = pl.program_id(1)

    @pl.when(f == 0)
    def _():
        h = _normmod(x_ref[...], g_ref[slot:slot + 1, :], mod_ref[3 * slot + 1:3 * slot + 2, :],
                     mod_ref[3 * slot:3 * slot + 1, :])
        h_ref[...] = h.astype(bf16)
        o_ref[...] = jnp.zeros_like(o_ref)

    h = h_ref[...]
    a = jnp.dot(h, wa_ref[...], preferred_element_type=f32)
    b = jnp.dot(h, wb_ref[...], preferred_element_type=f32)
    act = (_silu(a) * b).astype(bf16)
    o_ref[...] += jnp.dot(act, wd_ref[...], preferred_element_type=f32)

    @pl.when(f == nf - 1)
    def _():
        o_ref[...] = x_ref[...] + 0.5 * mod_ref[3 * slot + 2:3 * slot + 3, :] * o_ref[...]


def _ffn(x, mod, norm_g, w_up, w_down, layer, slot, cond_of_tile, tm):
    t, d = x.shape
    dff = w_down.shape[2]
    tf = _pick(dff, 512)
    nf = dff // tf
    which = slot // 2
    return pl.pallas_call(
        functools.partial(_ffn_kernel, slot=slot, nf=nf),
        grid=(t // tm, nf),
        in_specs=[pl.BlockSpec((tm, d), lambda i, f: (i, 0)),
                  pl.BlockSpec((None, None, N_MOD, d), lambda i, f: (layer, cond_of_tile(i), 0, 0)),
                  pl.BlockSpec((None, 3, d), lambda i, f: (layer, 0, 0)),
                  pl.BlockSpec((None, None, d, tf), lambda i, f: (layer, which, 0, f)),
                  pl.BlockSpec((None, None, d, tf), lambda i, f: (layer, which, 0, f + nf)),
                  pl.BlockSpec((None, None, tf, d), lambda i, f: (layer, which, f, 0))],
        out_specs=pl.BlockSpec((tm, d), lambda i, f: (i, 0)),
        out_shape=jax.ShapeDtypeStruct((t, d), f32),
        scratch_shapes=[pltpu.VMEM((tm, d), bf16)],
        compiler_params=_cparams(("arbitrary", "arbitrary")),
        name="ffn%d" % slot,
    )(x, mod, norm_g, w_up, w_up, w_down)


def _mixin_kernel(x_ref, mod_ref, g_ref, w_ref, o_ref, h_ref):
    @pl.when(pl.program_id(1) == 0)
    def _():
        h = _normmod(x_ref[...], g_ref[1:2, :], mod_ref[4:5, :], mod_ref[3:4, :])
        h_ref[...] = h.astype(bf16)

    o_ref[...] = jnp.dot(h_ref[...], w_ref[...], preferred_element_type=f32)


def _mixin(x, mod, norm_g, w, layer, cond_of_tile, tm):
    t, d = x.shape
    nw = w.shape[2]
    tn = _pick_mxu(nw, 1280)
    return pl.pallas_call(
        _mixin_kernel,
        grid=(t // tm, nw // tn),
        in_specs=[pl.BlockSpec((tm, d), lambda i, j: (i, 0)),
                  pl.BlockSpec((None, None, N_MOD, d), lambda i, j: (layer, cond_of_tile(i), 0, 0)),
                  pl.BlockSpec((None, 3, d), lambda i, j: (layer, 0, 0)),
                  pl.BlockSpec((None, d, tn), lambda i, j: (layer, 0, j))],
        out_specs=pl.BlockSpec((tm, tn), lambda i, j: (i, j)),
        out_shape=jax.ShapeDtypeStruct((t, nw), f32),
        scratch_shapes=[pltpu.VMEM((tm, d), bf16)],
        compiler_params=_cparams(("arbitrary", "arbitrary")),
        name="mix_in",
    )(x, mod, norm_g, w)


def _dir_row(d, col):
    return lambda j, b, f, s: (b[d, j], col)


def _dir_slab(d):
    return lambda j, b, f, s: (d, b[d, j], 0)


def _const_spec(shape):
    return pl.BlockSpec(shape, lambda j, b, f, s: (0,) * len(shape))


def _state_in_spec(tail, n_ctx, layer):
    return pl.BlockSpec((None, None) + tail,
                        lambda j, b, f, s: (jnp.maximum(s[j] - n_ctx, 0), layer) + (0,) * len(tail))


def _state_out_spec(tail, n_ctx):
    return pl.BlockSpec((None,) + tail, lambda j, b, f, s: (jnp.minimum(s[j], n_ctx - 1),) + (0,) * len(tail))


def _chunk_rows(ci, nc, c):
    return (pl.multiple_of(ci * c, c), pl.multiple_of((nc - 1 - ci) * c, c))


def _hgrn_kernel(blk_ref, first_ref, seq_ref, q0_ref, v0_ref, f0_ref, q1_ref, v1_ref, f1_ref, lb_ref, s0_ref,
                 mst_ref, pm_ref, o0_ref, o1_ref, sfin_ref, st_ref, qs_ref, ks_ref, lf_ref,
                 *, n_ctx, tb, bw, nlev):
    j = pl.program_id(0)
    c = HG_C
    nc = tb // c
    nh = bw // HG_DK
    is_lat = seq_ref[j] >= n_ctx
    q_refs, v_refs, f_refs, o_refs = (q0_ref, q1_ref), (v0_ref, v1_ref), (f0_ref, f1_ref), (o0_ref, o1_ref)

    @pl.when(first_ref[j] == 1)
    def _():
        st_ref[...] = jnp.where(is_lat, s0_ref[...], 0.0)

    for d in range(2):
        lb = lb_ref[d]
        fl = f_refs[d][...]
        lf_ref[d] = jnp.log(jnp.maximum(lb + (1.0 - lb) * _sigmoid(fl), GATE_FLOOR))
        ks_ref[d] = (1.0 - lb) * _sigmoid(-fl)
        qs_ref[d] = _silu(q_refs[d][...]) * HG_DK ** -0.5

    mst = [mst_ref[d].astype(bf16) for d in range(2)]
    tl = lax.broadcasted_iota(jnp.int32, (c, bw), 0) % SUB
    heads = [slice(h * HG_DK, (h + 1) * HG_DK) for h in range(nh)]
    items = [(d, h) for d in range(2) for h in range(nh)]

    def chunk(ci, carry):
        r0 = _chunk_rows(ci, nc, c)
        qe, kd, etot, qm, km, vs, accs = [], [], [], [], [], [], []
        for d in range(2):
            rows = pl.ds(r0[d], c)
            lf = lf_ref[d, rows, :]
            q = qs_ref[d, rows, :]
            k = ks_ref[d, rows, :]
            v = v_refs[d][rows, :]
            xs = _ldot01(mst[d], lf)
            b = xs[0:c]
            btot = jnp.sum(lf, axis=0, keepdims=True)
            qe.append(q * jnp.exp(b))
            kd.append(k * jnp.exp(btot - b))
            etot.append(jnp.exp(btot))
            es = [jnp.exp(xs[(lv + 1) * c:(lv + 2) * c]) for lv in range(nlev)]
            qm.append([q * e for e in es])
            km.append([k * e for e in es])
            vs.append(v)
            b3 = b.reshape(c // SUB, SUB, bw)
            k3 = k.reshape(c // SUB, SUB, bw)
            v3 = v.reshape(c // SUB, SUB, bw)
            acc = [jnp.zeros((c, HG_DK), f32) for _ in range(nh)]
            for sl in range(SUB):
                bs = jnp.broadcast_to(b3[:, sl:sl + 1, :], b3.shape).reshape(c, bw)
                kk = jnp.broadcast_to(k3[:, sl:sl + 1, :], k3.shape).reshape(c, bw)
                vv = jnp.broadcast_to(v3[:, sl:sl + 1, :], v3.shape).reshape(c, bw)
                valid = (tl >= sl) if d == 0 else (tl <= sl)
                p = jnp.where(valid, q * kk * jnp.exp(jnp.minimum(b - bs, 0.0)), 0.0)
                for h, hs in enumerate(heads):
                    acc[h] = acc[h] + jnp.sum(p[:, hs], axis=-1, keepdims=True) * vv[:, hs]
            accs.append(acc)
        sts = [st_ref[d, h] for d, h in items]
        inter = [_dot_nt(qe[d][:, heads[h]], st) for (d, h), st in zip(items, sts)]
        sc = [jnp.zeros((c, c), f32) for _ in items]
        for lv in range(nlev):
            sc = [s + _dot_nt(qm[d][lv][:, heads[h]], km[d][lv][:, heads[h]]) * pm_ref[d, lv]
                  for (d, h), s in zip(items, sc)]
        outs = [a + accs[d][h] + _dot(s, vs[d][:, heads[h]]) for (d, h), a, s in zip(items, inter, sc)]
        snew = [st * etot[d][:, heads[h]] + _dot_tn(vs[d][:, heads[h]], kd[d][:, heads[h]])
                for (d, h), st in zip(items, sts)]
        for (d, h), o, s in zip(items, outs, snew):
            o_refs[d][pl.ds(r0[d], c), heads[h]] = o
            st_ref[d, h] = s
        return carry

    lax.fori_loop(0, nc, chunk, 0)

    @pl.when(jnp.logical_not(is_lat))
    def _():
        sfin_ref[...] = st_ref[...]


def _hgrn_scan(proj, lb, s0t, layer, tables, n_ctx, tb, bw):
    t = proj.shape[0]
    nh = bw // HG_DK
    blk, first, seq = tables
    nb = blk.shape[1]
    mst, pm, nlev = _hgrn_tables_np(HG_C)
    kern = functools.partial(_hgrn_kernel, n_ctx=n_ctx, tb=tb, bw=bw, nlev=nlev)
    tail = (2, nh, HG_DK, HG_DK)
    dir_specs = [pl.BlockSpec((tb, bw), _dir_row(d, col)) for d in range(2) for col in (0, 1, 2 + d)]
    grid_spec = pltpu.PrefetchScalarGridSpec(
        num_scalar_prefetch=3,
        grid=(nb,),
        in_specs=dir_specs + [_const_spec((2, 1, bw)), _state_in_spec(tail, n_ctx, layer),
                              _const_spec(mst.shape), _const_spec(pm.shape)],
        out_specs=[pl.BlockSpec((tb, bw), _dir_row(0, 0)), pl.BlockSpec((tb, bw), _dir_row(1, 0)),
                   _state_out_spec(tail, n_ctx)],
        scratch_shapes=[pltpu.VMEM(tail, f32), pltpu.VMEM((2, tb, bw), f32),
                        pltpu.VMEM((2, tb, bw), f32), pltpu.VMEM((2, tb, bw), f32)],
    )
    return pl.pallas_call(
        kern, grid_spec=grid_spec,
        out_shape=[jax.ShapeDtypeStruct((t, bw), f32), jax.ShapeDtypeStruct((t, bw), f32),
                   jax.ShapeDtypeStruct((n_ctx,) + tail, f32)],
        compiler_params=_cparams(("arbitrary",)),
        name="hgrn_scan",
    )(jnp.asarray(blk), jnp.asarray(first), jnp.asarray(seq), *([proj] * 6), lb, s0t,
      jnp.asarray(mst), jnp.asarray(pm))


def _gdn_prep_kernel(prev_ref, cur_ref, nxt_ref, w_ref, o_ref, xe_ref, *, n_ctx, bps, tb, bw, halo):
    r = pl.program_id(0)
    part = pl.program_id(1)
    is_lat = r >= n_ctx
    pos = jnp.where(is_lat, (r - n_ctx) % bps, 0)
    pv = jnp.where(jnp.logical_and(is_lat, pos > 0), 1.0, 0.0)
    nv = jnp.where(jnp.logical_and(is_lat, pos < bps - 1), 1.0, 0.0)
    lat = jnp.where(is_lat, 1.0, 0.0)
    xe_ref[0:halo, :] = prev_ref[...] * pv
    xe_ref[halo:halo + tb, :] = cur_ref[...]
    xe_ref[halo + tb:halo + tb + halo, :] = nxt_ref[...] * nv
    col = lax.broadcasted_iota(jnp.int32, (tb, LANES), 0) % GRID_W
    m_lo = jnp.where(col >= 1, 1.0, 1.0 - lat)
    m_hi = jnp.where(col <= GRID_W - 2, 1.0, 1.0 - lat)
    for g in range(bw // LANES):
        gs = slice(g * LANES, (g + 1) * LANES)
        acc = None
        for dc in (-1, 0, 1):
            col_sum = None
            for dr in (-1, 0, 1):
                off = halo + GRID_W * dr + dc
                tap = (dr + 1) * CONV_K + (dc + 1)
                w = w_ref[tap:tap + 1, gs]
                if dr != 0:
                    w = w * lat
                term = xe_ref[off:off + tb, gs] * w
                col_sum = term if col_sum is None else col_sum + term
            if dc == -1:
                col_sum = col_sum * m_lo
            elif dc == 1:
                col_sum = col_sum * m_hi
            acc = col_sum if acc is None else acc + col_sum
        y = _silu(acc)
        ss = jnp.sum(y * y, axis=-1, keepdims=True)
        nrm = y * lax.rsqrt(ss + NORM_EPS)
        scale = jnp.where(part == 0, GD_DK ** -0.5, 1.0)
        o_ref[:, gs] = jnp.where(part == 2, y, nrm * scale)


def _gdn_prep(proj, conv_w, n_ctx, bps, tb, bw, qkv_col):
    t = proj.shape[0]
    nb = t // tb
    halo = LANES
    hb = tb // halo
    nhb = t // halo
    kern = functools.partial(_gdn_prep_kernel, n_ctx=n_ctx, bps=bps, tb=tb, bw=bw, halo=halo)
    return pl.pallas_call(
        kern,
        grid=(nb, 3),
        in_specs=[pl.BlockSpec((halo, bw), lambda r, p: (jnp.maximum(r * hb - 1, 0), qkv_col + p)),
                  pl.BlockSpec((tb, bw), lambda r, p: (r, qkv_col + p)),
                  pl.BlockSpec((halo, bw), lambda r, p: (jnp.minimum((r + 1) * hb, nhb - 1), qkv_col + p)),
                  pl.BlockSpec((CONV_K * CONV_K, bw), lambda r, p: (0, p))],
        out_specs=pl.BlockSpec((tb, bw), lambda r, p: (r, p)),
        out_shape=jax.ShapeDtypeStruct((t, 3 * bw), f32),
        scratch_shapes=[pltpu.VMEM((tb + 2 * halo, bw), f32)],
        compiler_params=_cparams(("arbitrary", "arbitrary")),
        name="gdn_prep",
    )(proj, proj, proj, conv_w)


def _gdn_kernel(blk_ref, first_ref, seq_ref, q0_ref, k0_ref, v0_ref, m0_ref, q1_ref, k1_ref, v1_ref, m1_ref,
                exa_ref, exb_ref, alog_ref, dtb_ref, s0_ref, tri_ref, msk_ref, o0_ref, o1_ref, sfin_ref,
                st_ref, lg_ref, be_ref, *, n_ctx, tb, bw, nmerge):
    j = pl.program_id(0)
    c = GD_C
    nc = tb // c
    nh = bw // GD_DK
    is_lat = seq_ref[j] >= n_ctx
    q_refs, k_refs, v_refs = (q0_ref, q1_ref), (k0_ref, k1_ref), (v0_ref, v1_ref)
    m_refs, o_refs = (m0_ref, m1_ref), (o0_ref, o1_ref)

    @pl.when(first_ref[j] == 1)
    def _():
        st_ref[...] = jnp.where(is_lat, s0_ref[...], 0.0)

    for d in range(2):
        ab = m_refs[d][:, MISC_W - LANES:MISC_W]
        a_in = _rdot01(ab, exa_ref[d])
        b_in = _rdot01(ab, exb_ref[d])
        lg_ref[d] = -jnp.exp(alog_ref[d]) * _softplus(a_in + dtb_ref[d])
        be_ref[d] = _sigmoid(b_in)

    tri = [tri_ref[d].astype(bf16) for d in range(2)]
    heads = [slice(h * GD_DK, (h + 1) * GD_DK) for h in range(nh)]
    items = [(d, h) for d in range(2) for h in range(nh)]
    strict = [msk_ref[d, 0] for d, _ in items]
    incl = [msk_ref[d, 1] for d, _ in items]
    mdiag = [msk_ref[d, 3] for d, _ in items]
    moffs = [[msk_ref[d, 4 + m] for m in range(nmerge)] for d, _ in items]

    def chunk(ci, carry):
        r0 = _chunk_rows(ci, nc, c)
        be, q, k, v, gcum, eg, qdec, kdec, gend = [], [], [], [], [], [], [], [], []
        for d in range(2):
            rows = pl.ds(r0[d], c)
            lg = lg_ref[d, rows, :]
            be.append(be_ref[d, rows, :])
            q.append(q_refs[d][rows, :])
            k.append(k_refs[d][rows, :])
            v.append(v_refs[d][rows, :])
            g = _ldot01(tri[d], lg)
            gtot = jnp.sum(lg, axis=0, keepdims=True)
            gcum.append(g)
            eg.append(jnp.exp(g))
            qdec.append(q[d] * eg[d])
            kdec.append(k[d] * jnp.exp(gtot - g))
            gend.append(jnp.exp(gtot))
        qk = [_dot_nt(jnp.concatenate([q[d][:, heads[h]], k[d][:, heads[h]]], axis=0), k[d][:, heads[h]])
              for d, h in items]
        dec = []
        for d, h in items:
            gc = gcum[d][:, heads[h]]
            dec.append(jnp.exp(jnp.minimum(gc - gc.T, 0.0)))
        ns = [be[d][:, heads[h]] * a[c:2 * c] * e * m for (d, h), a, e, m in zip(items, qk, dec, strict)]
        ys = _tri_inverse_y(ns, mdiag, moffs)
        rhs = [jnp.concatenate([be[d][:, heads[h]] * eg[d][:, heads[h]] * k[d][:, heads[h]],
                                be[d][:, heads[h]] * v[d][:, heads[h]]], axis=1) for d, h in items]
        wu = [r + _dotp(y, r) for y, r in zip(ys, rhs)]
        ps = [a[0:c] * e * m for a, e, m in zip(qk, dec, incl)]
        ss = [st_ref[d, h] for d, h in items]
        ws = [_dot(jnp.concatenate([x[:, 0:GD_DK], qdec[d][:, heads[h]]], axis=0), s)
              for (d, h), x, s in zip(items, wu, ss)]
        vnew = [x[:, GD_DK:2 * GD_DK] - y[0:c] for x, y in zip(wu, ws)]
        outs = [y[c:2 * c] + _dot(p, vn) for y, p, vn in zip(ws, ps, vnew)]
        snew = [gend[d][:, heads[h]] * s + _dot_tn(kdec[d][:, heads[h]], vn)
                for (d, h), s, vn in zip(items, ss, vnew)]
        for (d, h), o, s in zip(items, outs, snew):
            o_refs[d][pl.ds(r0[d], c), heads[h]] = o
            st_ref[d, h] = s
        return carry

    lax.fori_loop(0, nc, chunk, 0, unroll=True)

    @pl.when(jnp.logical_not(is_lat))
    def _():
        sfin_ref[...] = st_ref[...]


def _gdn_scan(qkv, proj, misc_col, exa, exb, alog, dtb, s0, layer, tables, n_ctx, tb, bw):
    t = proj.shape[0]
    nh = bw // GD_DK
    blk, first, seq = tables
    nb = blk.shape[1]
    c = GD_C
    tri = np.stack([_tri_np(c, False), _tri_np(c, True)]).astype(np.float32)
    inv = _inverse_masks_np(c, c)
    msk = np.stack([np.stack(list(_causal_np(c, c, rev)) + inv) for rev in (False, True)]).astype(np.float32)
    nmerge = len(inv) - 2
    kern = functools.partial(_gdn_kernel, n_ctx=n_ctx, tb=tb, bw=bw, nmerge=nmerge)
    tail = (2, nh, GD_DK, GD_DK)
    dir_specs = []
    for d in range(2):
        dir_specs += [pl.BlockSpec((tb, bw), _dir_row(d, col)) for col in range(3)]
        dir_specs += [pl.BlockSpec((tb, MISC_W), _dir_row(d, misc_col))]
    grid_spec = pltpu.PrefetchScalarGridSpec(
        num_scalar_prefetch=3,
        grid=(nb,),
        in_specs=dir_specs + [_const_spec((2, LANES, bw)), _const_spec((2, LANES, bw)), _const_spec((2, 1, bw)),
                              _const_spec((2, 1, bw)), _state_in_spec(tail, n_ctx, layer),
                              _const_spec(tri.shape), _const_spec(msk.shape)],
        out_specs=[pl.BlockSpec((tb, bw), _dir_row(0, 0)), pl.BlockSpec((tb, bw), _dir_row(1, 0)),
                   _state_out_spec(tail, n_ctx)],
        scratch_shapes=[pltpu.VMEM(tail, f32), pltpu.VMEM((2, tb, bw), f32), pltpu.VMEM((2, tb, bw), f32)],
    )
    return pl.pallas_call(
        kern, grid_spec=grid_spec,
        out_shape=[jax.ShapeDtypeStruct((t, bw), f32), jax.ShapeDtypeStruct((t, bw), f32),
                   jax.ShapeDtypeStruct((n_ctx,) + tail, f32)],
        compiler_params=_cparams(("arbitrary",)),
        name="gdn_scan",
    )(jnp.asarray(blk), jnp.asarray(first), jnp.asarray(seq), qkv, qkv, qkv, proj, qkv, qkv, qkv, proj,
      exa, exb, alog, dtb, s0, jnp.asarray(tri), jnp.asarray(msk))


def _rwkv_prep_kernel(r_ref, k_ref, v_ref, misc_ref, w0_ref, w2_ref, a0_ref, a2_ref, g2_ref, kkp_ref,
                      ka_ref, rk_ref, b64_ref, kk_ref, lw_ref, icl_ref, bonus_ref, og_ref):
    rr = r_ref[...]
    rk = k_ref[...]
    wl = jnp.tanh(misc_ref[:, 0:LANES])
    al = misc_ref[:, LANES:2 * LANES]
    gl = _sigmoid(misc_ref[:, 2 * LANES:3 * LANES])
    ka = ka_ref[...]
    bon = jnp.zeros_like(rr)
    for d in range(2):
        logw = -_softplus(-(w0_ref[d:d + 1, :] + _dot(wl, w2_ref[d]))) - 0.5
        lw_ref[d] = -jnp.exp(logw)
        icl = _sigmoid(a0_ref[d:d + 1, :] + _dot(al, a2_ref[d]))
        icl_ref[d] = icl
        bon = bon + rr * (rk * (1.0 + (icl - 1.0) * ka)) * rk_ref[...]
    og_ref[...] = _dot(gl, g2_ref[...])
    b64 = b64_ref[...]
    kx = rk * kkp_ref[...]
    kk_ref[...] = kx * lax.rsqrt(_rdot01(kx * kx, b64) + NORM_EPS)
    bonus_ref[...] = _rdot01(bon, b64) * v_ref[...]


def _rwkv_prep(proj, rkv_col, misc_col, w0, w2p, a0, a2p, g2p, kkp, ka, rkp, b64, tb, bw):
    t = proj.shape[0]
    full = lambda shape: pl.BlockSpec(shape, lambda r: (0,) * len(shape))
    return pl.pallas_call(
        _rwkv_prep_kernel,
        grid=(t // tb,),
        in_specs=[pl.BlockSpec((tb, bw), lambda r: (r, rkv_col)),
                  pl.BlockSpec((tb, bw), lambda r: (r, rkv_col + 1)),
                  pl.BlockSpec((tb, bw), lambda r: (r, rkv_col + 2)),
                  pl.BlockSpec((tb, MISC_W), lambda r: (r, misc_col)),
                  full((2, bw)), full((2, LANES, bw)), full((2, bw)), full((2, LANES, bw)),
                  full((LANES, bw)), full((1, bw)), full((1, bw)), full((1, bw)), full((bw, bw))],
        out_specs=[pl.BlockSpec((tb, bw), lambda r: (r, 0)),
                   pl.BlockSpec((2, tb, bw), lambda r: (0, r, 0)),
                   pl.BlockSpec((2, tb, bw), lambda r: (0, r, 0)),
                   pl.BlockSpec((tb, bw), lambda r: (r, 0)),
                   pl.BlockSpec((tb, bw), lambda r: (r, 0))],
        out_shape=[jax.ShapeDtypeStruct((t, bw), f32), jax.ShapeDtypeStruct((2, t, bw), f32),
                   jax.ShapeDtypeStruct((2, t, bw), f32), jax.ShapeDtypeStruct((t, bw), f32),
                   jax.ShapeDtypeStruct((t, bw), f32)],
        compiler_params=_cparams(("arbitrary",)),
        name="rwkv_prep",
    )(proj, proj, proj, proj, w0, w2p, a0, a2p, g2p, kkp, ka, rkp, b64)


def _rwkv_kernel(blk_ref, first_ref, seq_ref, r0_ref, k0_ref, v0_ref, kk0_ref, lw0_ref, icl0_ref,
                 r1_ref, k1_ref, v1_ref, kk1_ref, lw1_ref, icl1_ref, ka_ref, s0_ref, tri_ref, msk_ref,
                 o0_ref, o1_ref, sfin_ref, st_ref, *, n_ctx, tb, bw, nmerge):
    j = pl.program_id(0)
    c = RW_C
    nc = tb // c
    n2c = 2 * c
    npair = bw // LANES
    is_lat = seq_ref[j] >= n_ctx
    r_refs, k_refs, v_refs = (r0_ref, r1_ref), (k0_ref, k1_ref), (v0_ref, v1_ref)
    kk_refs, lw_refs, icl_refs, o_refs = (kk0_ref, kk1_ref), (lw0_ref, lw1_ref), (icl0_ref, icl1_ref), (o0_ref, o1_ref)

    @pl.when(first_ref[j] == 1)
    def _():
        st_ref[...] = jnp.where(is_lat, s0_ref[...], 0.0)

    tri = [tri_ref[d].astype(bf16) for d in range(2)]
    pairs = [slice(p * LANES, (p + 1) * LANES) for p in range(npair)]
    items = [(d, p) for d in range(2) for p in range(npair)]
    strict = [msk_ref[d, 0] for d, _ in items]
    incl2 = [jnp.concatenate([msk_ref[d, 1], msk_ref[d, 1]], axis=1) for d in range(2)]
    mdiag = [msk_ref[d, 3] for d, _ in items]
    moffs = [[msk_ref[d, 4 + m] for m in range(nmerge)] for d, _ in items]
    lane = lax.broadcasted_iota(jnp.int32, (c, LANES), 1)
    m_even = jnp.where(lane < RW_DK, 1.0, 0.0)
    m_odd = 1.0 - m_even
    ka = ka_ref[...]

    def stack2(x):
        return jnp.concatenate([x * m_even, x * m_odd], axis=0)

    def chunk(ci, carry):
        r0 = _chunk_rows(ci, nc, c)
        at, kt, bt, rt, rv, kdec, bdec, etot = [], [], [], [], [], [], [], []
        for d in range(2):
            rows = pl.ds(r0[d], c)
            lw = lw_refs[d][rows, :]
            icl = icl_refs[d][rows, :]
            kk = kk_refs[d][rows, :]
            cum = _ldot01(tri[d], lw)
            ctot = jnp.sum(lw, axis=0, keepdims=True)
            kdir = k_refs[d][rows, :] * (1.0 + (icl - 1.0) * ka)
            bvec = kk * icl
            einv = jnp.exp(-cum)
            edec = jnp.exp(ctot - cum)
            at.append((-kk) * jnp.exp(cum - lw))
            kt.append(kdir * einv)
            bt.append(bvec * einv)
            rt.append(r_refs[d][rows, :] * jnp.exp(cum))
            rv.append(v_refs[d][rows, :])
            kdec.append(kdir * edec)
            bdec.append(bvec * edec)
            etot.append(jnp.exp(ctot))
        a2 = [stack2(at[d][:, pairs[p]]) for d, p in items]
        r2 = [stack2(rt[d][:, pairs[p]]) for d, p in items]
        v2 = [stack2(rv[d][:, pairs[p]]) for d, p in items]
        kb2 = [jnp.concatenate([stack2(kt[d][:, pairs[p]]), stack2(bt[d][:, pairs[p]])], axis=0) for d, p in items]
        kbd2 = [jnp.concatenate([stack2(kdec[d][:, pairs[p]]), stack2(bdec[d][:, pairs[p]])], axis=0)
                for d, p in items]
        big = [_dot_nt(jnp.concatenate([a, r], axis=0), kb) for a, r, kb in zip(a2, r2, kb2)]
        aak = [x[0:n2c, 0:n2c] * m for x, m in zip(big, strict)]
        ns = [-(x[0:n2c, n2c:2 * n2c] * m) for x, m in zip(big, strict)]
        arkb = [x[n2c:2 * n2c] * incl2[d] for x, (d, _) in zip(big, items)]
        ys = _tri_inverse_y(ns, mdiag, moffs)
        rhs = [jnp.concatenate([a, _dot(m, v)], axis=1) for a, m, v in zip(a2, aak, v2)]
        tat = [x + _dotp(y, x) for y, x in zip(ys, rhs)]
        sts = [st_ref[d, p] for d, p in items]
        ur = [_dot_nt(jnp.concatenate([x[:, 0:LANES], r], axis=0), st) for x, r, st in zip(tat, r2, sts)]
        u2 = [y[0:n2c] + x[:, LANES:2 * LANES] for y, x in zip(ur, tat)]
        vu = [jnp.concatenate([v, u], axis=0) for v, u in zip(v2, u2)]
        y2 = [y[n2c:2 * n2c] + _dot(m, x) for y, m, x in zip(ur, arkb, vu)]
        snew = [st * etot[d][:, pairs[p]] + _dot_tn(x, kbd) for (d, p), st, x, kbd in zip(items, sts, vu, kbd2)]
        for (d, p), y, s in zip(items, y2, snew):
            o_refs[d][pl.ds(r0[d], c), pairs[p]] = y[0:c] + y[c:n2c]
            st_ref[d, p] = s
        return carry

    lax.fori_loop(0, nc, chunk, 0, unroll=True)

    @pl.when(jnp.logical_not(is_lat))
    def _():
        sfin_ref[...] = st_ref[...]


def _rwkv_scan(proj, rkv_col, kk, lw, icl, ka, s0p, layer, tables, n_ctx, tb, bw):
    t = proj.shape[0]
    npair = bw // LANES
    blk, first, seq = tables
    nb = blk.shape[1]
    c = RW_C
    n = 2 * c
    tri = np.stack([_tri_np(c, False), _tri_np(c, True)]).astype(np.float32)
    inv = _inverse_masks_np(n, c)
    msk = np.stack([np.stack(list(_causal_np(n, c, rev)) + inv) for rev in (False, True)]).astype(np.float32)
    nmerge = len(inv) - 2
    kern = functools.partial(_rwkv_kernel, n_ctx=n_ctx, tb=tb, bw=bw, nmerge=nmerge)
    tail = (2, npair, LANES, LANES)
    dir_specs, dir_args = [], []
    for d in range(2):
        dir_specs += [pl.BlockSpec((tb, bw), _dir_row(d, rkv_col + i)) for i in range(3)]
        dir_specs += [pl.BlockSpec((tb, bw), _dir_row(d, 0)),
                      pl.BlockSpec((None, tb, bw), _dir_slab(d)), pl.BlockSpec((None, tb, bw), _dir_slab(d))]
        dir_args += [proj, proj, proj, kk, lw, icl]
    grid_spec = pltpu.PrefetchScalarGridSpec(
        num_scalar_prefetch=3,
        grid=(nb,),
        in_specs=dir_specs + [_const_spec((1, bw)), _state_in_spec(tail, n_ctx, layer),
                              _const_spec(tri.shape), _const_spec(msk.shape)],
        out_specs=[pl.BlockSpec((tb, bw), _dir_row(0, 0)), pl.BlockSpec((tb, bw), _dir_row(1, 0)),
                   _state_out_spec(tail, n_ctx)],
        scratch_shapes=[pltpu.VMEM(tail, f32)],
    )
    return pl.pallas_call(
        kern, grid_spec=grid_spec,
        out_shape=[jax.ShapeDtypeStruct((t, bw), f32), jax.ShapeDtypeStruct((t, bw), f32),
                   jax.ShapeDtypeStruct((n_ctx,) + tail, f32)],
        compiler_params=_cparams(("arbitrary",)),
        name="rwkv_scan",
    )(jnp.asarray(blk), jnp.asarray(first), jnp.asarray(seq), *dir_args, ka, s0p,
      jnp.asarray(tri), jnp.asarray(msk))


def _merge_kernel(x_ref, mod_ref, ohg0_ref, ohg1_ref, ogd0_ref, ogd1_ref, orw0_ref, orw1_ref, hgg_ref, gdz_ref,
                  bonus_ref, og_ref, g0_ref, g1_ref, g2_ref, hgn_ref, gdn_ref, lng_ref, lnb_ref, wb_ref, wo_ref,
                  o_ref, *, bw):
    def head_mean(x, width):
        outs = []
        for g in range(bw // LANES):
            xg = x[:, g * LANES:(g + 1) * LANES]
            if width == LANES:
                outs.append(jnp.broadcast_to(jnp.mean(xg, axis=-1, keepdims=True), xg.shape))
            else:
                lane = lax.broadcasted_iota(jnp.int32, xg.shape, 1)
                lo = lane < width
                s_lo = jnp.sum(jnp.where(lo, xg, 0.0), axis=-1, keepdims=True)
                s_hi = jnp.sum(jnp.where(lo, 0.0, xg), axis=-1, keepdims=True)
                outs.append(jnp.where(lo, s_lo, s_hi) * (1.0 / width))
        return jnp.concatenate(outs, axis=-1)

    ohg = ohg0_ref[...] + ohg1_ref[...]
    ohg = ohg * lax.rsqrt(head_mean(ohg * ohg, HG_DK) + NORM_EPS) * hgn_ref[...]
    ohg = ohg * _silu(hgg_ref[...])
    ogd = ogd0_ref[...] + ogd1_ref[...]
    ogd = ogd * lax.rsqrt(head_mean(ogd * ogd, GD_DK) + NORM_EPS) * gdn_ref[...]
    ogd = ogd * _silu(gdz_ref[...])
    orw = orw0_ref[...] + orw1_ref[...]
    mu = head_mean(orw, RW_DK)
    cen = orw - mu
    var = head_mean(cen * cen, RW_DK)
    orw = cen * lax.rsqrt(var + RW_GN_EPS) * lng_ref[...] + lnb_ref[...]
    orw = (orw + bonus_ref[...]) * og_ref[...]
    merged = (_sigmoid(g0_ref[...]) * _dot(ohg, wb_ref[0])
              + _sigmoid(g1_ref[...]) * _dot(ogd, wb_ref[1])
              + _sigmoid(g2_ref[...]) * _dot(orw, wb_ref[2]))
    o_ref[...] = x_ref[...] + mod_ref[5:6, :] * _dot(merged, wo_ref[...])


def _merge(x, mod, proj, branch_outs, bonus, og, hgn, gdn, lng, lnb, wb, wo, layer, cond_of_tile, tm, bw,
           hgg_col, gdz_col, gate_col):
    t, d = x.shape
    row = lambda col: (lambda i: (i, col))
    act = pl.BlockSpec((tm, bw), row(0))
    vec = pl.BlockSpec((1, bw), lambda i: (0, 0))
    once = pl.Buffered(1)
    return pl.pallas_call(
        functools.partial(_merge_kernel, bw=bw),
        grid=(t // tm,),
        in_specs=[pl.BlockSpec((tm, d), row(0)),
                  pl.BlockSpec((None, None, N_MOD, d), lambda i: (layer, cond_of_tile(i), 0, 0)),
                  act, act, act, act, act, act,
                  pl.BlockSpec((tm, bw), row(hgg_col)),
                  pl.BlockSpec((tm, bw), row(gdz_col)),
                  act, act,
                  pl.BlockSpec((tm, d), row(gate_col)),
                  pl.BlockSpec((tm, d), row(gate_col + 1)),
                  pl.BlockSpec((tm, d), row(gate_col + 2)),
                  vec, vec, vec, vec,
                  pl.BlockSpec((None, 3, bw, d), lambda i: (layer, 0, 0, 0), pipeline_mode=once),
                  pl.BlockSpec((None, d, d), lambda i: (layer, 0, 0), pipeline_mode=once)],
        out_specs=pl.BlockSpec((tm, d), row(0)),
        out_shape=jax.ShapeDtypeStruct((t, d), f32),
        compiler_params=_cparams(("arbitrary",)),
        name="merge",
    )(x, mod, *branch_outs, proj, proj, bonus, og, proj, proj, proj, hgn, gdn, lng, lnb, wb, wo)


def _final_norm_kernel(x_ref, g_ref, o_ref):
    x = x_ref[...]
    ms = jnp.mean(x * x, axis=-1, keepdims=True)
    o_ref[...] = x * lax.rsqrt(ms + NORM_EPS) * g_ref[...]


def _final_norm(x, g, tm, row0, rows):
    d = x.shape[1]
    first = row0 // tm
    return pl.pallas_call(
        _final_norm_kernel,
        grid=(rows // tm,),
        in_specs=[pl.BlockSpec((tm, d), lambda i: (first + i, 0)), pl.BlockSpec((1, d), lambda i: (0, 0))],
        out_specs=pl.BlockSpec((tm, d), lambda i: (i, 0)),
        out_shape=jax.ShapeDtypeStruct((rows, d), f32),
        compiler_params=_cparams(("arbitrary",)),
        name="final_norm",
    )(x, g.reshape(1, d))


def _pad_rows(w, rows):
    return jnp.concatenate([w, jnp.zeros((rows - w.shape[0],) + w.shape[1:], w.dtype)], axis=0)


def kernel(x_prompt, x_sample, state_hgrn, state_gdn, state_rwkv, c, c_ctx, mod_w, mod_b, norm_g, ffn_up,
           ffn_down, mix_in, hgrn_lb, hgrn_norm, gdn_conv, gdn_a_log, gdn_dt_bias, gdn_norm, rwkv_w0, rwkv_w2,
           rwkv_a0, rwkv_a2, rwkv_g2, rwkv_kk, rwkv_ka, rwkv_rk, rwkv_ln_g, rwkv_ln_b, mix_branch, mix_out,
           final_norm):
    bc, lc, d = x_prompt.shape
    bs, ls, _ = x_sample.shape
    depth = mod_w.shape[0]
    bw = d // 4
    tb = lc
    assert tb % GD_C == 0 and ls % tb == 0 and bw % LANES == 0 and tb % GRID_W == 0
    assert bs + 1 <= N_COND_PAD
    n_ctx = bc
    bps = ls // tb
    tc = bc * lc
    t = tc + bs * ls
    gd_h, rw_h = bw // GD_DK, bw // RW_DK
    lw_w, la_w, lg_w = rwkv_w2.shape[2], rwkv_a2.shape[2], rwkv_g2.shape[1]
    assert max(2 * lw_w, 2 * la_w, lg_w, 4 * gd_h) <= LANES

    def token_tile(cap):
        rows = tb
        while rows * 2 <= cap and tc % (rows * 2) == 0 and ls % (rows * 2) == 0:
            rows *= 2
        return rows

    tm, tp = token_tile(TM_FFN), token_tile(TM_PROJ)
    cond_of_tile = lambda tile_rows: (lambda i: jnp.where(i * tile_rows < tc, 0, 1 + (i * tile_rows - tc) // ls))

    x = jnp.concatenate([x_prompt.reshape(tc, d), x_sample.reshape(bs * ls, d)], axis=0)
    cond = jnp.concatenate([c_ctx[None, :], c, jnp.zeros((N_COND_PAD - 1 - bs, d), f32)], axis=0)
    mod_all = _modulation(cond, mod_w, mod_b).reshape(depth, N_COND_PAD, N_MOD, d)

    lb_p = jax.nn.softmax(hgrn_lb.astype(f32), axis=0)
    lb_all = jnp.cumsum(lb_p, axis=0) - lb_p[0:1]

    o_gda = 9 * bw
    o_rw = 9 * bw + 4 * gd_h
    o_lo = o_rw + 3 * bw
    o_mg = o_lo + 2 * lw_w + 2 * la_w + lg_w
    zpad = lambda n: jnp.zeros((depth, d, n), mix_in.dtype)
    misc = jnp.concatenate([
        mix_in[:, :, o_lo:o_lo + 2 * lw_w], zpad(LANES - 2 * lw_w),
        mix_in[:, :, o_lo + 2 * lw_w:o_lo + 2 * lw_w + 2 * la_w], zpad(LANES - 2 * la_w),
        mix_in[:, :, o_lo + 2 * lw_w + 2 * la_w:o_mg], zpad(LANES - lg_w),
        mix_in[:, :, o_gda:o_gda + 4 * gd_h], zpad(LANES - 4 * gd_h)], axis=2)
    w_in = jnp.concatenate([mix_in[:, :, 0:9 * bw], mix_in[:, :, o_rw:o_rw + 3 * bw],
                            mix_in[:, :, o_mg:o_mg + 3 * d], misc], axis=2).astype(bf16)
    col_gdq, col_gdz, col_rw, col_gate = 5, 8, 9, 3
    col_misc = (24 * bw) // MISC_W
    assert (24 * bw) % MISC_W == 0

    up16 = ffn_up.astype(bf16)
    down16 = ffn_down.astype(bf16)
    wb16 = mix_branch.astype(bf16)
    wo16 = mix_out.astype(bf16)

    exa = np.zeros((2, LANES, bw), np.float32)
    exb = np.zeros((2, LANES, bw), np.float32)
    for dd in range(2):
        for h in range(gd_h):
            exa[dd, dd * gd_h + h, h * GD_DK:(h + 1) * GD_DK] = 1.0
            exb[dd, 2 * gd_h + dd * gd_h + h, h * GD_DK:(h + 1) * GD_DK] = 1.0
    exa, exb = jnp.asarray(exa), jnp.asarray(exb)
    b64 = jnp.asarray(np.kron(np.eye(rw_h), np.ones((RW_DK, RW_DK))).astype(np.float32))

    tables = _block_tables_np(n_ctx, bs, bps)

    s0_hg = jnp.swapaxes(state_hgrn, -1, -2)
    s0_gd = state_gdn
    eye2 = jnp.eye(2, dtype=f32)
    s0_rw = state_rwkv.reshape(bs, depth, 2, rw_h // 2, 2, RW_DK, RW_DK)
    s0_rw = jnp.einsum('bldpeij,ef->bldpeifj', s0_rw, eye2).reshape(bs, depth, 2, rw_h // 2, LANES, LANES)

    new_hg, new_gd, new_rw = [], [], []
    for l in range(depth):
        mod = mod_all
        x = _ffn(x, mod, norm_g, up16, down16, l, 0, cond_of_tile(tm), tm)
        proj = _mixin(x, mod, norm_g, w_in, l, cond_of_tile(tp), tp)

        ohg0, ohg1, s_hg = _hgrn_scan(proj, lb_all[l].reshape(2, 1, bw), s0_hg, l, tables, n_ctx, tb, bw)

        conv_w = gdn_conv[l].reshape(CONV_K * CONV_K, 3 * bw)
        qkv = _gdn_prep(proj, conv_w, n_ctx, bps, tb, bw, col_gdq)
        alog = jnp.repeat(gdn_a_log[l], GD_DK, axis=-1).reshape(2, 1, bw)
        dtb = jnp.repeat(gdn_dt_bias[l], GD_DK, axis=-1).reshape(2, 1, bw)
        ogd0, ogd1, s_gd = _gdn_scan(qkv, proj, col_misc, exa, exb, alog, dtb, s0_gd, l, tables, n_ctx, tb, bw)

        w2p = jnp.stack([_pad_rows(jnp.concatenate([jnp.zeros((dd * lw_w, bw), f32), rwkv_w2[l, dd]], 0), LANES)
                         for dd in range(2)])
        a2p = jnp.stack([_pad_rows(jnp.concatenate([jnp.zeros((dd * la_w, bw), f32), rwkv_a2[l, dd]], 0), LANES)
                         for dd in range(2)])
        g2p = _pad_rows(rwkv_g2[l], LANES)
        ka = rwkv_ka[l].reshape(1, bw)
        kk, lw, icl, bonus, og = _rwkv_prep(proj, col_rw, col_misc, rwkv_w0[l], w2p, rwkv_a0[l], a2p, g2p,
                                            rwkv_kk[l].reshape(1, bw), ka, rwkv_rk[l].reshape(1, bw), b64,
                                            tb, bw)
        orw0, orw1, s_rw = _rwkv_scan(proj, col_rw, kk, lw, icl, ka, s0_rw, l, tables, n_ctx, tb, bw)

        x = _merge(x, mod, proj, (ohg0, ohg1, ogd0, ogd1, orw0, orw1), bonus, og, hgrn_norm[l].reshape(1, bw),
                   jnp.tile(gdn_norm[l], gd_h).reshape(1, bw), rwkv_ln_g[l].reshape(1, bw),
                   rwkv_ln_b[l].reshape(1, bw), wb16, wo16, l, cond_of_tile(tb), tb, bw,
                   4, col_gdz, col_gate)
        x = _ffn(x, mod, norm_g, up16, down16, l, 2, cond_of_tile(tm), tm)

        new_hg.append(jnp.swapaxes(s_hg, -1, -2))
        new_gd.append(s_gd)
        s_rw = s_rw.reshape(n_ctx, 2, rw_h // 2, 2, RW_DK, 2, RW_DK)
        new_rw.append(jnp.einsum('bdpeifj,ef->bdpeij', s_rw, eye2).reshape(n_ctx, 2, rw_h, RW_DK, RW_DK))

    y_ctx = _final_norm(x, final_norm, tp, 0, tc)
    y_lat = _final_norm(x, final_norm, tp, tc, bs * ls)
    return (y_ctx.reshape(bc, lc, d), y_lat.reshape(bs, ls, d),
            jnp.stack(new_hg, axis=1), jnp.stack(new_gd, axis=1), jnp.stack(new_rw, axis=1))
```

```python
import functools

import numpy as np
import jax
import jax.numpy as jnp
from jax import lax
from jax.experimental import pallas as pl
from jax.experimental.pallas import tpu as pltpu

f32 = jnp.float32
bf16 = jnp.bfloat16

LANES = 128
SUBLANES = 8
MXU_W = 256
VMEM_LIMIT = 60 * 1024 * 1024

HG_DK = 128
GD_DK = 128
RW_DK = 64
N_MOD = 9
GRID_W = 64
CONV_K = 3
NORM_EPS = 1e-6
RW_GN_EPS = 64e-5
GATE_FLOOR = 1e-30
N_COND_PAD = 16

HG_C = 64
GD_C = 128
RW_C = 64
SUB = 8
INV_PASSES = 1
MISC_W = 4 * LANES
TM_FFN = 512
TM_PROJ = 1024


def _sigmoid(x):
    return 1.0 / (1.0 + jnp.exp(-x))


def _silu(x):
    return x * _sigmoid(x)


def _softplus(x):
    return jnp.maximum(x, 0.0) + jnp.log(1.0 + jnp.exp(-jnp.abs(x)))


def _dot(a, b):
    return jnp.dot(a.astype(bf16), b.astype(bf16), preferred_element_type=f32)


def _dot_nt(a, b):
    return lax.dot_general(a.astype(bf16), b.astype(bf16), (((1,), (1,)), ((), ())),
                           preferred_element_type=f32)


def _dot_tn(a, b):
    return lax.dot_general(a.astype(bf16), b.astype(bf16), (((0,), (0,)), ((), ())),
                           preferred_element_type=f32)


def _split3(x):
    hi = x.astype(bf16)
    r = x - hi.astype(f32)
    mid = r.astype(bf16)
    lo = (r - mid.astype(f32)).astype(bf16)
    return hi, mid, lo


def _ldot01(m, x):
    hi, mid, lo = _split3(x)
    m = m.astype(bf16)
    d = functools.partial(jnp.dot, preferred_element_type=f32)
    return d(m, hi) + d(m, mid) + d(m, lo)


def _rdot01(x, m):
    hi, mid, lo = _split3(x)
    m = m.astype(bf16)
    d = functools.partial(jnp.dot, preferred_element_type=f32)
    return d(hi, m) + d(mid, m) + d(lo, m)


def _dotp(a, b):
    if INV_PASSES == 1:
        return _dot(a, b)
    ah = a.astype(bf16)
    al = (a - ah.astype(f32)).astype(bf16)
    bh = b.astype(bf16)
    bl = (b - bh.astype(f32)).astype(bf16)
    d = functools.partial(jnp.dot, preferred_element_type=f32)
    return d(ah, bh) + d(ah, bl) + d(al, bh)


def _tri_inverse_y(ns, mdiag, moffs):
    nd = [n * m for n, m in zip(ns, mdiag)]
    n2 = [_dotp(a, a) for a in nd]
    n4 = [_dotp(a, a) for a in n2]
    y = [b - a - _dotp(a, b) for a, b in zip(nd, n2)]
    y = [a + b + _dotp(a, b) for a, b in zip(y, n4)]
    for lvl in range(len(moffs[0])):
        xl = [n * m[lvl] for n, m in zip(ns, moffs)]
        xl = [l + _dotp(a, l) for a, l in zip(y, xl)]
        y = [a - l - _dotp(l, a) for a, l in zip(y, xl)]
    return y


def _normmod(x, g, scale, shift):
    ms = jnp.mean(x * x, axis=-1, keepdims=True)
    return (x * lax.rsqrt(ms + NORM_EPS) * g) * (1.0 + scale) + shift


def _tri_np(c, rev):
    i = np.arange(c)
    return (i[None, :] >= i[:, None]) if rev else (i[None, :] <= i[:, None])


def _inverse_masks_np(n, c):
    i = np.arange(n)
    same = (i[:, None] // c) == (i[None, :] // c)
    eye = np.eye(n, dtype=bool)
    mdiag = (i[:, None] // SUB) == (i[None, :] // SUB)
    out = [eye, mdiag]
    m = SUB
    while m < c:
        out.append(same & ((i[:, None] // (2 * m)) == (i[None, :] // (2 * m)))
                   & ((i[:, None] // m) != (i[None, :] // m)))
        m *= 2
    return out


def _causal_np(n, c, rev):
    i = np.arange(n)
    same = (i[:, None] // c) == (i[None, :] // c)
    il, jl = (i % c)[:, None], (i % c)[None, :]
    strict = same & ((jl > il) if rev else (jl < il))
    incl = same & ((jl >= il) if rev else (jl <= il))
    return strict, incl


def _hgrn_tables_np(c):
    levels = []
    m = SUB
    while m < c:
        levels.append(m)
        m *= 2
    stacks, pmasks = [], []
    t = np.arange(c)
    for rev in (False, True):
        mats = [_tri_np(c, rev)]
        pms = []
        for m in levels:
            pair = t // (2 * m)
            half = (t // m) % 2
            tau = t[None, :]
            if not rev:
                bd = (pair * 2 * m + m - 1)[:, None]
                qside = (half == 1)[:, None]
                mat = np.where(qside, (tau > bd) & (tau <= t[:, None]), (tau > t[:, None]) & (tau <= bd))
                pm = (pair[:, None] == pair[None, :]) & (half == 1)[:, None] & (half == 0)[None, :]
            else:
                bd = (pair * 2 * m + m)[:, None]
                qside = (half == 0)[:, None]
                mat = np.where(qside, (tau >= t[:, None]) & (tau < bd), (tau >= bd) & (tau < t[:, None]))
                pm = (pair[:, None] == pair[None, :]) & (half == 0)[:, None] & (half == 1)[None, :]
            mats.append(mat)
            pms.append(pm)
        stacks.append(np.concatenate(mats, axis=0))
        pmasks.append(np.stack(pms, axis=0))
    return (np.stack(stacks).astype(np.float32), np.stack(pmasks).astype(np.float32), len(levels))


def _block_tables_np(n_ctx, n_lat_seq, bps):
    nb = n_ctx + n_lat_seq * bps
    blk = np.zeros((2, nb), np.int32)
    first = np.zeros((nb,), np.int32)
    seq = np.zeros((nb,), np.int32)
    j = 0
    for s in range(n_ctx):
        blk[0, j] = blk[1, j] = s
        first[j], seq[j] = 1, s
        j += 1
    for s in range(n_lat_seq):
        for p in range(bps):
            blk[0, j] = n_ctx + s * bps + p
            blk[1, j] = n_ctx + s * bps + bps - 1 - p
            first[j], seq[j] = int(p == 0), n_ctx + s
            j += 1
    return blk, first, seq


def _pick(n, cap):
    best = LANES
    t = LANES
    while t <= min(n, cap):
        if n % t == 0:
            best = t
        t += LANES
    return best


def _pick_mxu(n, cap):
    best = 0
    t = MXU_W
    while t <= min(n, cap):
        if n % t == 0:
            best = t
        t += MXU_W
    return best or _pick(n, cap)


def _cparams(sem):
    return pltpu.CompilerParams(dimension_semantics=sem, vmem_limit_bytes=VMEM_LIMIT)


def _mod_kernel(cond_ref, w_ref, b_ref, o_ref):
    x = cond_ref[...]
    o_ref[...] = _dot(_silu(x), w_ref[...]) + b_ref[...]


def _modulation(cond, mod_w, mod_b):
    depth, d, nd = mod_w.shape
    tn = _pick(nd, 1024)
    return pl.pallas_call(
        _mod_kernel,
        grid=(depth, nd // tn),
        in_specs=[pl.BlockSpec((N_COND_PAD, d), lambda l, j: (0, 0)),
                  pl.BlockSpec((None, d, tn), lambda l, j: (l, 0, j)),
                  pl.BlockSpec((None, 1, tn), lambda l, j: (l, 0, j))],
        out_specs=pl.BlockSpec((None, N_COND_PAD, tn), lambda l, j: (l, 0, j)),
        out_shape=jax.ShapeDtypeStruct((depth, N_COND_PAD, nd), f32),
        compiler_params=_cparams(("arbitrary", "arbitrary")),
        name="modulation",
    )(cond, mod_w, mod_b.reshape(depth, 1, nd))


def _relayout_kernel(w_ref, o_ref, *, pieces):
    nw = o_ref.shape[1]
    o_ref[:, nw - MISC_W:nw] = jnp.zeros((o_ref.shape[0], MISC_W), bf16)
    for src, n, dst in pieces:
        o_ref[:, dst:dst + n] = w_ref[:, src:src + n].astype(bf16)


def _relayout_mix_in(w, pieces, nw):
    depth, d, win = w.shape
    tr = LANES
    return pl.pallas_call(
        functools.partial(_relayout_kernel, pieces=pieces),
        grid=(depth, d // tr),
        in_specs=[pl.BlockSpec((None, tr, win), lambda l, r: (l, r, 0))],
        out_specs=pl.BlockSpec((None, tr, nw), lambda l, r: (l, r, 0)),
        out_shape=jax.ShapeDtypeStruct((depth, d, nw), bf16),
        compiler_params=_cparams(("arbitrary", "arbitrary")),
        name="relayout_mix_in",
    )(w)


def _ffn_kernel(x_ref, mod_ref, g_ref, wa_ref, wb_ref, wd_ref, o_ref, h_ref, *, slot, nf):
    f = pl.program_id(1)

    @pl.when(f == 0)
    def _():
        h = _normmod(x_ref[...], g_ref[slot:slot + 1, :], mod_ref[3 * slot + 1:3 * slot + 2, :],
                     mod_ref[3 * slot:3 * slot + 1, :])
        h_ref[...] = h.astype(bf16)
        o_ref[...] = jnp.zeros_like(o_ref)

    h = h_ref[...]
    a = jnp.dot(h, wa_ref[...], preferred_element_type=f32)
    b = jnp.dot(h, wb_ref[...], preferred_element_type=f32)
    act = (_silu(a) * b).astype(bf16)
    o_ref[...] += jnp.dot(act, wd_ref[...], preferred_element_type=f32)

    @pl.when(f == nf - 1)
    def _():
        o_ref[...] = x_ref[...] + 0.5 * mod_ref[3 * slot + 2:3 * slot + 3, :] * o_ref[...]


def _ffn(x, mod, norm_g, w_up, w_down, layer, slot, cond_of_tile, tm):
    t, d = x.shape
    dff = w_down.shape[2]
    tf = _pick(dff, 512)
    nf = dff // tf
    which = slot // 2
    return pl.pallas_call(
        functools.partial(_ffn_kernel, slot=slot, nf=nf),
        grid=(t // tm, nf),
        in_specs=[pl.BlockSpec((tm, d), lambda i, f: (i, 0)),
                  pl.BlockSpec((None, None, N_MOD, d), lambda i, f: (layer, cond_of_tile(i), 0, 0)),
                  pl.BlockSpec((None, 3, d), lambda i, f: (layer, 0, 0)),
                  pl.BlockSpec((None, None, d, tf), lambda i, f: (layer, which, 0, f)),
                  pl.BlockSpec((None, None, d, tf), lambda i, f: (layer, which, 0, f + nf)),
                  pl.BlockSpec((None, None, tf, d), lambda i, f: (layer, which, f, 0))],
        out_specs=pl.BlockSpec((tm, d), lambda i, f: (i, 0)),
        out_shape=jax.ShapeDtypeStruct((t, d), f32),
        scratch_shapes=[pltpu.VMEM((tm, d), bf16)],
        compiler_params=_cparams(("arbitrary", "arbitrary")),
        name="ffn%d" % slot,
    )(x, mod, norm_g, w_up, w_up, w_down)


def _mixin_kernel(x_ref, mod_ref, g_ref, w_ref, o_ref, h_ref):
    @pl.when(pl.program_id(1) == 0)
    def _():
        h = _normmod(x_ref[...], g_ref[1:2, :], mod_ref[4:5, :], mod_ref[3:4, :])
        h_ref[...] = h.astype(bf16)

    o_ref[...] = jnp.dot(h_ref[...], w_ref[...], preferred_element_type=f32)


def _mixin(x, mod, norm_g, w, layer, cond_of_tile, tm):
    t, d = x.shape
    nw = w.shape[2]
    tn = _pick_mxu(nw, 1280)
    return pl.pallas_call(
        _mixin_kernel,
        grid=(t // tm, nw // tn),
        in_specs=[pl.BlockSpec((tm, d), lambda i, j: (i, 0)),
                  pl.BlockSpec((None, None, N_MOD, d), lambda i, j: (layer, cond_of_tile(i), 0, 0)),
                  pl.BlockSpec((None, 3, d), lambda i, j: (layer, 0, 0)),
                  pl.BlockSpec((None, d, tn), lambda i, j: (layer, 0, j))],
        out_specs=pl.BlockSpec((tm, tn), lambda i, j: (i, j)),
        out_shape=jax.ShapeDtypeStruct((t, nw), f32),
        scratch_shapes=[pltpu.VMEM((tm, d), bf16)],
        compiler_params=_cparams(("arbitrary", "arbitrary")),
        name="mix_in",
    )(x, mod, norm_g, w)


def _dir_row(d, col):
    return lambda j, b, f, s: (b[d, j], col)


def _dir_slab(d):
    return lambda j, b, f, s: (d, b[d, j], 0)


def _const_spec(shape):
    return pl.BlockSpec(shape, lambda j, b, f, s: (0,) * len(shape))


def _state_in_spec(tail, n_ctx, layer):
    return pl.BlockSpec((None, None) + tail,
                        lambda j, b, f, s: (jnp.maximum(s[j] - n_ctx, 0), layer) + (0,) * len(tail))


def _state_out_spec(tail, n_ctx):
    return pl.BlockSpec((None,) + tail, lambda j, b, f, s: (jnp.minimum(s[j], n_ctx - 1),) + (0,) * len(tail))


def _chunk_rows(ci, nc, c):
    return (pl.multiple_of(ci * c, c), pl.multiple_of((nc - 1 - ci) * c, c))


def _hgrn_kernel(blk_ref, first_ref, seq_ref, q0_ref, v0_ref, f0_ref, q1_ref, v1_ref, f1_ref, lb_ref, s0_ref,
                 mst_ref, pm_ref, o0_ref, o1_ref, sfin_ref, st_ref, qs_ref, ks_ref, lf_ref,
                 *, n_ctx, tb, bw, nlev):
    j = pl.program_id(0)
    c = HG_C
    nc = tb // c
    nh = bw // HG_DK
    is_lat = seq_ref[j] >= n_ctx
    q_refs, v_refs, f_refs, o_refs = (q0_ref, q1_ref), (v0_ref, v1_ref), (f0_ref, f1_ref), (o0_ref, o1_ref)

    @pl.when(first_ref[j] == 1)
    def _():
        st_ref[...] = jnp.where(is_lat, s0_ref[...], 0.0)

    for d in range(2):
        lb = lb_ref[d]
        fl = f_refs[d][...]
        sig = _sigmoid(fl)
        lf_ref[d] = jnp.log(jnp.maximum(lb + (1.0 - lb) * sig, GATE_FLOOR))
        ks_ref[d] = (1.0 - lb) * (1.0 - sig)
        qs_ref[d] = _silu(q_refs[d][...]) * HG_DK ** -0.5

    mst = [mst_ref[d].astype(bf16) for d in range(2)]
    tl = lax.broadcasted_iota(jnp.int32, (c, bw), 0) % SUB
    heads = [slice(h * HG_DK, (h + 1) * HG_DK) for h in range(nh)]
    items = [(d, h) for d in range(2) for h in range(nh)]

    def chunk(ci, carry):
        r0 = _chunk_rows(ci, nc, c)
        qe, kd, etot, qm, km, vs, accs = [], [], [], [], [], [], []
        for d in range(2):
            rows = pl.ds(r0[d], c)
            lf = lf_ref[d, rows, :]
            q = qs_ref[d, rows, :]
            k = ks_ref[d, rows, :]
            v = v_refs[d][rows, :]
            xs = _ldot01(mst[d], lf)
            b = xs[0:c]
            btot = jnp.sum(lf, axis=0, keepdims=True)
            qe.append(q * jnp.exp(b))
            kd.append(k * jnp.exp(btot - b))
            etot.append(jnp.exp(btot))
            es = [jnp.exp(xs[(lv + 1) * c:(lv + 2) * c]) for lv in range(nlev)]
            qm.append([q * e for e in es])
            km.append([k * e for e in es])
            vs.append(v)
            b3 = b.reshape(c // SUB, SUB, bw)
            k3 = k.reshape(c // SUB, SUB, bw)
            v3 = v.reshape(c // SUB, SUB, bw)
            acc = [jnp.zeros((c, HG_DK), f32) for _ in range(nh)]
            for sl in range(SUB):
                bs = jnp.broadcast_to(b3[:, sl:sl + 1, :], b3.shape).reshape(c, bw)
                kk = jnp.broadcast_to(k3[:, sl:sl + 1, :], k3.shape).reshape(c, bw)
                vv = jnp.broadcast_to(v3[:, sl:sl + 1, :], v3.shape).reshape(c, bw)
                valid = (tl >= sl) if d == 0 else (tl <= sl)
                p = jnp.where(valid, q * kk * jnp.exp(jnp.minimum(b - bs, 0.0)), 0.0)
                for h, hs in enumerate(heads):
                    acc[h] = acc[h] + jnp.sum(p[:, hs], axis=-1, keepdims=True) * vv[:, hs]
            accs.append(acc)
        sts = [st_ref[d, h] for d, h in items]
        inter = [_dot_nt(qe[d][:, heads[h]], st) for (d, h), st in zip(items, sts)]
        sc = [jnp.zeros((c, c), f32) for _ in items]
        for lv in range(nlev):
            sc = [s + _dot_nt(qm[d][lv][:, heads[h]], km[d][lv][:, heads[h]]) * pm_ref[d, lv]
                  for (d, h), s in zip(items, sc)]
        outs = [a + accs[d][h] + _dot(s, vs[d][:, heads[h]]) for (d, h), a, s in zip(items, inter, sc)]
        snew = [st * etot[d][:, heads[h]] + _dot_tn(vs[d][:, heads[h]], kd[d][:, heads[h]])
                for (d, h), st in zip(items, sts)]
        for (d, h), o, s in zip(items, outs, snew):
            o_refs[d][pl.ds(r0[d], c), heads[h]] = o
            st_ref[d, h] = s
        return carry

    lax.fori_loop(0, nc, chunk, 0)

    @pl.when(jnp.logical_not(is_lat))
    def _():
        for d, h in items:
            sfin_ref[d, h] = st_ref[d, h].T


def _hgrn_scan(proj, lb, s0t, layer, tables, n_ctx, tb, bw):
    t = proj.shape[0]
    nh = bw // HG_DK
    blk, first, seq = tables
    nb = blk.shape[1]
    mst, pm, nlev = _hgrn_tables_np(HG_C)
    kern = functools.partial(_hgrn_kernel, n_ctx=n_ctx, tb=tb, bw=bw, nlev=nlev)
    tail = (2, nh, HG_DK, HG_DK)
    dir_specs = [pl.BlockSpec((tb, bw), _dir_row(d, col)) for d in range(2) for col in (0, 1, 2 + d)]
    grid_spec = pltpu.PrefetchScalarGridSpec(
        num_scalar_prefetch=3,
        grid=(nb,),
        in_specs=dir_specs + [_const_spec((2, 1, bw)), _state_in_spec(tail, n_ctx, layer),
                              _const_spec(mst.shape), _const_spec(pm.shape)],
        out_specs=[pl.BlockSpec((tb, bw), _dir_row(0, 0)), pl.BlockSpec((tb, bw), _dir_row(1, 0)),
                   _state_out_spec(tail, n_ctx)],
        scratch_shapes=[pltpu.VMEM(tail, f32), pltpu.VMEM((2, tb, bw), f32),
                        pltpu.VMEM((2, tb, bw), f32), pltpu.VMEM((2, tb, bw), f32)],
    )
    return pl.pallas_call(
        kern, grid_spec=grid_spec,
        out_shape=[jax.ShapeDtypeStruct((t, bw), f32), jax.ShapeDtypeStruct((t, bw), f32),
                   jax.ShapeDtypeStruct((n_ctx,) + tail, f32)],
        compiler_params=_cparams(("arbitrary",)),
        name="hgrn_scan",
    )(jnp.asarray(blk), jnp.asarray(first), jnp.asarray(seq), *([proj] * 6), lb, s0t,
      jnp.asarray(mst), jnp.asarray(pm))


def _gdn_prep_kernel(prev_ref, cur_ref, nxt_ref, w_ref, o_ref, xe_ref, *, n_ctx, bps, tb, bw, halo):
    r = pl.program_id(0)
    part = pl.program_id(1)
    is_lat = r >= n_ctx
    pos = jnp.where(is_lat, (r - n_ctx) % bps, 0)
    pv = jnp.where(jnp.logical_and(is_lat, pos > 0), 1.0, 0.0)
    nv = jnp.where(jnp.logical_and(is_lat, pos < bps - 1), 1.0, 0.0)
    lat = jnp.where(is_lat, 1.0, 0.0)
    xe_ref[0:halo, :] = prev_ref[...] * pv
    xe_ref[halo:halo + tb, :] = cur_ref[...]
    xe_ref[halo + tb:halo + tb + halo, :] = nxt_ref[...] * nv
    col = lax.broadcasted_iota(jnp.int32, (tb, LANES), 0) % GRID_W
    m_lo = jnp.where(col >= 1, 1.0, 1.0 - lat)
    m_hi = jnp.where(col <= GRID_W - 2, 1.0, 1.0 - lat)
    for g in range(bw // LANES):
        gs = slice(g * LANES, (g + 1) * LANES)
        acc = None
        for dc in (-1, 0, 1):
            col_sum = None
            for dr in (-1, 0, 1):
                off = halo + GRID_W * dr + dc
                tap = (dr + 1) * CONV_K + (dc + 1)
                w = w_ref[tap:tap + 1, gs]
                if dr != 0:
                    w = w * lat
                term = xe_ref[off:off + tb, gs] * w
                col_sum = term if col_sum is None else col_sum + term
            if dc == -1:
                col_sum = col_sum * m_lo
            elif dc == 1:
                col_sum = col_sum * m_hi
            acc = col_sum if acc is None else acc + col_sum
        y = _silu(acc)
        ss = jnp.sum(y * y, axis=-1, keepdims=True)
        nrm = y * lax.rsqrt(ss + NORM_EPS)
        scale = jnp.where(part == 0, GD_DK ** -0.5, 1.0)
        o_ref[:, gs] = jnp.where(part == 2, y, nrm * scale)


def _gdn_prep(proj, conv_w, n_ctx, bps, tb, bw, qkv_col):
    t = proj.shape[0]
    nb = t // tb
    halo = LANES
    hb = tb // halo
    nhb = t // halo
    kern = functools.partial(_gdn_prep_kernel, n_ctx=n_ctx, bps=bps, tb=tb, bw=bw, halo=halo)
    return pl.pallas_call(
        kern,
        grid=(nb, 3),
        in_specs=[pl.BlockSpec((halo, bw), lambda r, p: (jnp.maximum(r * hb - 1, 0), qkv_col + p)),
                  pl.BlockSpec((tb, bw), lambda r, p: (r, qkv_col + p)),
                  pl.BlockSpec((halo, bw), lambda r, p: (jnp.minimum((r + 1) * hb, nhb - 1), qkv_col + p)),
                  pl.BlockSpec((CONV_K * CONV_K, bw), lambda r, p: (0, p))],
        out_specs=pl.BlockSpec((tb, bw), lambda r, p: (r, p)),
        out_shape=jax.ShapeDtypeStruct((t, 3 * bw), f32),
        scratch_shapes=[pltpu.VMEM((tb + 2 * halo, bw), f32)],
        compiler_params=_cparams(("arbitrary", "arbitrary")),
        name="gdn_prep",
    )(proj, proj, proj, conv_w)


def _gdn_kernel(blk_ref, first_ref, seq_ref, q0_ref, k0_ref, v0_ref, m0_ref, q1_ref, k1_ref, v1_ref, m1_ref,
                exa_ref, exb_ref, alog_ref, dtb_ref, s0_ref, tri_ref, msk_ref, o0_ref, o1_ref, sfin_ref,
                st_ref, lg_ref, be_ref, *, n_ctx, tb, bw, nmerge):
    j = pl.program_id(0)
    c = GD_C
    nc = tb // c
    nh = bw // GD_DK
    is_lat = seq_ref[j] >= n_ctx
    q_refs, k_refs, v_refs = (q0_ref, q1_ref), (k0_ref, k1_ref), (v0_ref, v1_ref)
    m_refs, o_refs = (m0_ref, m1_ref), (o0_ref, o1_ref)

    @pl.when(first_ref[j] == 1)
    def _():
        st_ref[...] = jnp.where(is_lat, s0_ref[...], 0.0)

    for d in range(2):
        ab = m_refs[d][:, MISC_W - LANES:MISC_W]
        a_in = _rdot01(ab, exa_ref[d])
        b_in = _rdot01(ab, exb_ref[d])
        lg_ref[d] = -jnp.exp(alog_ref[d]) * _softplus(a_in + dtb_ref[d])
        be_ref[d] = _sigmoid(b_in)

    tri = [tri_ref[d].astype(bf16) for d in range(2)]
    heads = [slice(h * GD_DK, (h + 1) * GD_DK) for h in range(nh)]
    items = [(d, h) for d in range(2) for h in range(nh)]
    strict = [msk_ref[d, 0] for d, _ in items]
    incl = [msk_ref[d, 1] for d, _ in items]
    mdiag = [msk_ref[d, 3] for d, _ in items]
    moffs = [[msk_ref[d, 4 + m] for m in range(nmerge)] for d, _ in items]

    def chunk(ci, carry):
        r0 = _chunk_rows(ci, nc, c)
        be, q, k, v, gcum, eg, qdec, kdec, gend = [], [], [], [], [], [], [], [], []
        for d in range(2):
            rows = pl.ds(r0[d], c)
            lg = lg_ref[d, rows, :]
            be.append(be_ref[d, rows, :])
            q.append(q_refs[d][rows, :])
            k.append(k_refs[d][rows, :])
            v.append(v_refs[d][rows, :])
            g = _ldot01(tri[d], lg)
            gtot = jnp.sum(lg, axis=0, keepdims=True)
            gcum.append(g)
            eg.append(jnp.exp(g))
            qdec.append(q[d] * eg[d])
            kdec.append(k[d] * jnp.exp(gtot - g))
            gend.append(jnp.exp(gtot))
        qk = [_dot_nt(jnp.concatenate([q[d][:, heads[h]], k[d][:, heads[h]]], axis=0), k[d][:, heads[h]])
              for d, h in items]
        dec = []
        for d, h in items:
            gc = gcum[d][:, heads[h]]
            dec.append(jnp.exp(jnp.minimum(gc - gc.T, 0.0)))
        ns = [be[d][:, heads[h]] * a[c:2 * c] * e * m for (d, h), a, e, m in zip(items, qk, dec, strict)]
        ys = _tri_inverse_y(ns, mdiag, moffs)
        rhs = [jnp.concatenate([be[d][:, heads[h]] * eg[d][:, heads[h]] * k[d][:, heads[h]],
                                be[d][:, heads[h]] * v[d][:, heads[h]]], axis=1) for d, h in items]
        wu = [r + _dotp(y, r) for y, r in zip(ys, rhs)]
        ps = [a[0:c] * e * m for a, e, m in zip(qk, dec, incl)]
        ss = [st_ref[d, h] for d, h in items]
        ws = [_dot(jnp.concatenate([x[:, 0:GD_DK], qdec[d][:, heads[h]]], axis=0), s)
              for (d, h), x, s in zip(items, wu, ss)]
        vnew = [x[:, GD_DK:2 * GD_DK] - y[0:c] for x, y in zip(wu, ws)]
        outs = [y[c:2 * c] + _dot(p, vn) for y, p, vn in zip(ws, ps, vnew)]
        snew = [gend[d][:, heads[h]] * s + _dot_tn(kdec[d][:, heads[h]], vn)
                for (d, h), s, vn in zip(items, ss, vnew)]
        for (d, h), o, s in zip(items, outs, snew):
            o_refs[d][pl.ds(r0[d], c), heads[h]] = o
            st_ref[d, h] = s
        return carry

    lax.fori_loop(0, nc, chunk, 0, unroll=True)

    @pl.when(jnp.logical_not(is_lat))
    def _():
        sfin_ref[...] = st_ref[...]


def _gdn_scan(qkv, proj, misc_col, exa, exb, alog, dtb, s0, layer, tables, n_ctx, tb, bw):
    t = proj.shape[0]
    nh = bw // GD_DK
    blk, first, seq = tables
    nb = blk.shape[1]
    c = GD_C
    tri = np.stack([_tri_np(c, False), _tri_np(c, True)]).astype(np.float32)
    inv = _inverse_masks_np(c, c)
    msk = np.stack([np.stack(list(_causal_np(c, c, rev)) + inv) for rev in (False, True)]).astype(np.float32)
    nmerge = len(inv) - 2
    kern = functools.partial(_gdn_kernel, n_ctx=n_ctx, tb=tb, bw=bw, nmerge=nmerge)
    tail = (2, nh, GD_DK, GD_DK)
    dir_specs = []
    for d in range(2):
        dir_specs += [pl.BlockSpec((tb, bw), _dir_row(d, col)) for col in range(3)]
        dir_specs += [pl.BlockSpec((tb, MISC_W), _dir_row(d, misc_col))]
    grid_spec = pltpu.PrefetchScalarGridSpec(
        num_scalar_prefetch=3,
        grid=(nb,),
        in_specs=dir_specs + [_const_spec((2, LANES, bw)), _const_spec((2, LANES, bw)), _const_spec((2, 1, bw)),
                              _const_spec((2, 1, bw)), _state_in_spec(tail, n_ctx, layer),
                              _const_spec(tri.shape), _const_spec(msk.shape)],
        out_specs=[pl.BlockSpec((tb, bw), _dir_row(0, 0)), pl.BlockSpec((tb, bw), _dir_row(1, 0)),
                   _state_out_spec(tail, n_ctx)],
        scratch_shapes=[pltpu.VMEM(tail, f32), pltpu.VMEM((2, tb, bw), f32), pltpu.VMEM((2, tb, bw), f32)],
    )
    return pl.pallas_call(
        kern, grid_spec=grid_spec,
        out_shape=[jax.ShapeDtypeStruct((t, bw), f32), jax.ShapeDtypeStruct((t, bw), f32),
                   jax.ShapeDtypeStruct((n_ctx,) + tail, f32)],
        compiler_params=_cparams(("arbitrary",)),
        name="gdn_scan",
    )(jnp.asarray(blk), jnp.asarray(first), jnp.asarray(seq), qkv, qkv, qkv, proj, qkv, qkv, qkv, proj,
      exa, exb, alog, dtb, s0, jnp.asarray(tri), jnp.asarray(msk))


def _rwkv_prep_kernel(r_ref, k_ref, v_ref, misc_ref, w0_ref, w2_ref, a0_ref, a2_ref, g2_ref, kkp_ref,
                      ka_ref, rk_ref, b64_ref, kk_ref, lw_ref, icl_ref, bonus_ref, og_ref):
    rr = r_ref[...]
    rk = k_ref[...]
    wl = jnp.tanh(misc_ref[:, 0:LANES])
    al = misc_ref[:, LANES:2 * LANES]
    gl = _sigmoid(misc_ref[:, 2 * LANES:3 * LANES])
    ka = ka_ref[...]
    bon = jnp.zeros_like(rr)
    for d in range(2):
        logw = -_softplus(-(w0_ref[d:d + 1, :] + _dot(wl, w2_ref[d]))) - 0.5
        lw_ref[d] = -jnp.exp(logw)
        icl = _sigmoid(a0_ref[d:d + 1, :] + _dot(al, a2_ref[d]))
        icl_ref[d] = icl
        bon = bon + rr * (rk * (1.0 + (icl - 1.0) * ka)) * rk_ref[...]
    og_ref[...] = _dot(gl, g2_ref[...])
    b64 = b64_ref[...]
    kx = rk * kkp_ref[...]
    kk_ref[...] = kx * lax.rsqrt(_rdot01(kx * kx, b64) + NORM_EPS)
    bonus_ref[...] = _rdot01(bon, b64) * v_ref[...]


def _rwkv_prep(proj, rkv_col, misc_col, w0, w2p, a0, a2p, g2p, kkp, ka, rkp, b64, tb, bw):
    t = proj.shape[0]
    full = lambda shape: pl.BlockSpec(shape, lambda r: (0,) * len(shape))
    return pl.pallas_call(
        _rwkv_prep_kernel,
        grid=(t // tb,),
        in_specs=[pl.BlockSpec((tb, bw), lambda r: (r, rkv_col)),
                  pl.BlockSpec((tb, bw), lambda r: (r, rkv_col + 1)),
                  pl.BlockSpec((tb, bw), lambda r: (r, rkv_col + 2)),
                  pl.BlockSpec((tb, MISC_W), lambda r: (r, misc_col)),
                  full((2, bw)), full((2, LANES, bw)), full((2, bw)), full((2, LANES, bw)),
                  full((LANES, bw)), full((1, bw)), full((1, bw)), full((1, bw)), full((bw, bw))],
        out_specs=[pl.BlockSpec((tb, bw), lambda r: (r, 0)),
                   pl.BlockSpec((2, tb, bw), lambda r: (0, r, 0)),
                   pl.BlockSpec((2, tb, bw), lambda r: (0, r, 0)),
                   pl.BlockSpec((tb, bw), lambda r: (r, 0)),
                   pl.BlockSpec((tb, bw), lambda r: (r, 0))],
        out_shape=[jax.ShapeDtypeStruct((t, bw), f32), jax.ShapeDtypeStruct((2, t, bw), f32),
                   jax.ShapeDtypeStruct((2, t, bw), f32), jax.ShapeDtypeStruct((t, bw), f32),
                   jax.ShapeDtypeStruct((t, bw), f32)],
        compiler_params=_cparams(("arbitrary",)),
        name="rwkv_prep",
    )(proj, proj, proj, proj, w0, w2p, a0, a2p, g2p, kkp, ka, rkp, b64)


def _rwkv_kernel(blk_ref, first_ref, seq_ref, r0_ref, k0_ref, v0_ref, kk0_ref, lw0_ref, icl0_ref,
                 r1_ref, k1_ref, v1_ref, kk1_ref, lw1_ref, icl1_ref, ka_ref, s0_ref, tri_ref, msk_ref,
                 o0_ref, o1_ref, sfin_ref, st_ref, *, n_ctx, tb, bw, nmerge):
    j = pl.program_id(0)
    c = RW_C
    nc = tb // c
    n2c = 2 * c
    npair = bw // LANES
    is_lat = seq_ref[j] >= n_ctx
    r_refs, k_refs, v_refs = (r0_ref, r1_ref), (k0_ref, k1_ref), (v0_ref, v1_ref)
    kk_refs, lw_refs, icl_refs, o_refs = (kk0_ref, kk1_ref), (lw0_ref, lw1_ref), (icl0_ref, icl1_ref), (o0_ref, o1_ref)

    @pl.when(first_ref[j] == 1)
    def _():
        st_ref[...] = jnp.where(is_lat, s0_ref[...], 0.0)

    tri = [tri_ref[d].astype(bf16) for d in range(2)]
    pairs = [slice(p * LANES, (p + 1) * LANES) for p in range(npair)]
    items = [(d, p) for d in range(2) for p in range(npair)]
    strict = [msk_ref[d, 0] for d, _ in items]
    incl2 = [jnp.concatenate([msk_ref[d, 1], msk_ref[d, 1]], axis=1) for d in range(2)]
    mdiag = [msk_ref[d, 3] for d, _ in items]
    moffs = [[msk_ref[d, 4 + m] for m in range(nmerge)] for d, _ in items]
    lane = lax.broadcasted_iota(jnp.int32, (c, LANES), 1)
    m_even = jnp.where(lane < RW_DK, 1.0, 0.0)
    m_odd = 1.0 - m_even
    ka = ka_ref[...]

    def stack2(x):
        return jnp.concatenate([x * m_even, x * m_odd], axis=0)

    def chunk(ci, carry):
        r0 = _chunk_rows(ci, nc, c)
        at, kt, bt, rt, rv, kdec, bdec, etot = [], [], [], [], [], [], [], []
        for d in range(2):
            rows = pl.ds(r0[d], c)
            lw = lw_refs[d][rows, :]
            icl = icl_refs[d][rows, :]
            kk = kk_refs[d][rows, :]
            cum = _ldot01(tri[d], lw)
            ctot = jnp.sum(lw, axis=0, keepdims=True)
            kdir = k_refs[d][rows, :] * (1.0 + (icl - 1.0) * ka)
            bvec = kk * icl
            einv = jnp.exp(-cum)
            edec = jnp.exp(ctot - cum)
            at.append((-kk) * jnp.exp(cum - lw))
            kt.append(kdir * einv)
            bt.append(bvec * einv)
            rt.append(r_refs[d][rows, :] * jnp.exp(cum))
            rv.append(v_refs[d][rows, :])
            kdec.append(kdir * edec)
            bdec.append(bvec * edec)
            etot.append(jnp.exp(ctot))
        a2 = [stack2(at[d][:, pairs[p]]) for d, p in items]
        r2 = [stack2(rt[d][:, pairs[p]]) for d, p in items]
        v2 = [stack2(rv[d][:, pairs[p]]) for d, p in items]
        kb2 = [jnp.concatenate([stack2(kt[d][:, pairs[p]]), stack2(bt[d][:, pairs[p]])], axis=0) for d, p in items]
        kbd2 = [jnp.concatenate([stack2(kdec[d][:, pairs[p]]), stack2(bdec[d][:, pairs[p]])], axis=0)
                for d, p in items]
        big = [_dot_nt(jnp.concatenate([a, r], axis=0), kb) for a, r, kb in zip(a2, r2, kb2)]
        aak = [x[0:n2c, 0:n2c] * m for x, m in zip(big, strict)]
        ns = [-(x[0:n2c, n2c:2 * n2c] * m) for x, m in zip(big, strict)]
        arkb = [x[n2c:2 * n2c] * incl2[d] for x, (d, _) in zip(big, items)]
        ys = _tri_inverse_y(ns, mdiag, moffs)
        rhs = [jnp.concatenate([a, _dot(m, v)], axis=1) for a, m, v in zip(a2, aak, v2)]
        tat = [x + _dotp(y, x) for y, x in zip(ys, rhs)]
        sts = [st_ref[d, p] for d, p in items]
        ur = [_dot_nt(jnp.concatenate([x[:, 0:LANES], r], axis=0), st) for x, r, st in zip(tat, r2, sts)]
        u2 = [y[0:n2c] + x[:, LANES:2 * LANES] for y, x in zip(ur, tat)]
        vu = [jnp.concatenate([v, u], axis=0) for v, u in zip(v2, u2)]
        y2 = [y[n2c:2 * n2c] + _dot(m, x) for y, m, x in zip(ur, arkb, vu)]
        snew = [st * etot[d][:, pairs[p]] + _dot_tn(x, kbd) for (d, p), st, x, kbd in zip(items, sts, vu, kbd2)]
        for (d, p), y, s in zip(items, y2, snew):
            o_refs[d][pl.ds(r0[d], c), pairs[p]] = y[0:c] + y[c:n2c]
            st_ref[d, p] = s
        return carry

    lax.fori_loop(0, nc, chunk, 0, unroll=True)

    @pl.when(jnp.logical_not(is_lat))
    def _():
        for d, p in items:
            st = st_ref[d, p]
            sfin_ref[d, 2 * p] = st[0:RW_DK, 0:RW_DK]
            sfin_ref[d, 2 * p + 1] = pltpu.roll(st, RW_DK, axis=1)[RW_DK:LANES, 0:RW_DK]


def _rwkv_scan(proj, rkv_col, kk, lw, icl, ka, s0p, layer, tables, n_ctx, tb, bw):
    t = proj.shape[0]
    npair = bw // LANES
    blk, first, seq = tables
    nb = blk.shape[1]
    c = RW_C
    n = 2 * c
    tri = np.stack([_tri_np(c, False), _tri_np(c, True)]).astype(np.float32)
    inv = _inverse_masks_np(n, c)
    msk = np.stack([np.stack(list(_causal_np(n, c, rev)) + inv) for rev in (False, True)]).astype(np.float32)
    nmerge = len(inv) - 2
    kern = functools.partial(_rwkv_kernel, n_ctx=n_ctx, tb=tb, bw=bw, nmerge=nmerge)
    tail = (2, npair, LANES, LANES)
    head_tail = (2, 2 * npair, RW_DK, RW_DK)
    dir_specs, dir_args = [], []
    for d in range(2):
        dir_specs += [pl.BlockSpec((tb, bw), _dir_row(d, rkv_col + i)) for i in range(3)]
        dir_specs += [pl.BlockSpec((tb, bw), _dir_row(d, 0)),
                      pl.BlockSpec((None, tb, bw), _dir_slab(d)), pl.BlockSpec((None, tb, bw), _dir_slab(d))]
        dir_args += [proj, proj, proj, kk, lw, icl]
    grid_spec = pltpu.PrefetchScalarGridSpec(
        num_scalar_prefetch=3,
        grid=(nb,),
        in_specs=dir_specs + [_const_spec((1, bw)), _state_in_spec(tail, n_ctx, layer),
                              _const_spec(tri.shape), _const_spec(msk.shape)],
        out_specs=[pl.BlockSpec((tb, bw), _dir_row(0, 0)), pl.BlockSpec((tb, bw), _dir_row(1, 0)),
                   _state_out_spec(head_tail, n_ctx)],
        scratch_shapes=[pltpu.VMEM(tail, f32)],
    )
    return pl.pallas_call(
        kern, grid_spec=grid_spec,
        out_shape=[jax.ShapeDtypeStruct((t, bw), f32), jax.ShapeDtypeStruct((t, bw), f32),
                   jax.ShapeDtypeStruct((n_ctx,) + head_tail, f32)],
        compiler_params=_cparams(("arbitrary",)),
        name="rwkv_scan",
    )(jnp.asarray(blk), jnp.asarray(first), jnp.asarray(seq), *dir_args, ka, s0p,
      jnp.asarray(tri), jnp.asarray(msk))


def _merge_kernel(x_ref, mod_ref, ohg0_ref, ohg1_ref, ogd0_ref, ogd1_ref, orw0_ref, orw1_ref, hgg_ref, gdz_ref,
                  bonus_ref, og_ref, g0_ref, g1_ref, g2_ref, hgn_ref, gdn_ref, lng_ref, lnb_ref, wb_ref, wo_ref,
                  o_ref, *, bw):
    def head_mean(x, width):
        outs = []
        for g in range(bw // LANES):
            xg = x[:, g * LANES:(g + 1) * LANES]
            if width == LANES:
                outs.append(jnp.broadcast_to(jnp.mean(xg, axis=-1, keepdims=True), xg.shape))
            else:
                lane = lax.broadcasted_iota(jnp.int32, xg.shape, 1)
                lo = lane < width
                s_lo = jnp.sum(jnp.where(lo, xg, 0.0), axis=-1, keepdims=True)
                s_hi = jnp.sum(jnp.where(lo, 0.0, xg), axis=-1, keepdims=True)
                outs.append(jnp.where(lo, s_lo, s_hi) * (1.0 / width))
        return jnp.concatenate(outs, axis=-1)

    ohg = ohg0_ref[...] + ohg1_ref[...]
    ohg = ohg * lax.rsqrt(head_mean(ohg * ohg, HG_DK) + NORM_EPS) * hgn_ref[...]
    ohg = ohg * _silu(hgg_ref[...])
    ogd = ogd0_ref[...] + ogd1_ref[...]
    ogd = ogd * lax.rsqrt(head_mean(ogd * ogd, GD_DK) + NORM_EPS) * gdn_ref[...]
    ogd = ogd * _silu(gdz_ref[...])
    orw = orw0_ref[...] + orw1_ref[...]
    mu = head_mean(orw, RW_DK)
    cen = orw - mu
    var = head_mean(cen * cen, RW_DK)
    orw = cen * lax.rsqrt(var + RW_GN_EPS) * lng_ref[...] + lnb_ref[...]
    orw = (orw + bonus_ref[...]) * og_ref[...]
    merged = (_sigmoid(g0_ref[...]) * _dot(ohg, wb_ref[0])
              + _sigmoid(g1_ref[...]) * _dot(ogd, wb_ref[1])
              + _sigmoid(g2_ref[...]) * _dot(orw, wb_ref[2]))
    o_ref[...] = x_ref[...] + mod_ref[5:6, :] * _dot(merged, wo_ref[...])


def _merge(x, mod, proj, branch_outs, bonus, og, hgn, gdn, lng, lnb, wb, wo, layer, cond_of_tile, tm, bw,
           hgg_col, gdz_col, gate_col):
    t, d = x.shape
    row = lambda col: (lambda i: (i, col))
    act = pl.BlockSpec((tm, bw), row(0))
    vec = pl.BlockSpec((1, bw), lambda i: (0, 0))
    once = pl.Buffered(1)
    return pl.pallas_call(
        functools.partial(_merge_kernel, bw=bw),
        grid=(t // tm,),
        in_specs=[pl.BlockSpec((tm, d), row(0)),
                  pl.BlockSpec((None, None, N_MOD, d), lambda i: (layer, cond_of_tile(i), 0, 0)),
                  act, act, act, act, act, act,
                  pl.BlockSpec((tm, bw), row(hgg_col)),
                  pl.BlockSpec((tm, bw), row(gdz_col)),
                  act, act,
                  pl.BlockSpec((tm, d), row(gate_col)),
                  pl.BlockSpec((tm, d), row(gate_col + 1)),
                  pl.BlockSpec((tm, d), row(gate_col + 2)),
                  vec, vec, vec, vec,
                  pl.BlockSpec((None, 3, bw, d), lambda i: (layer, 0, 0, 0), pipeline_mode=once),
                  pl.BlockSpec((None, d, d), lambda i: (layer, 0, 0), pipeline_mode=once)],
        out_specs=pl.BlockSpec((tm, d), row(0)),
        out_shape=jax.ShapeDtypeStruct((t, d), f32),
        compiler_params=_cparams(("arbitrary",)),
        name="merge",
    )(x, mod, *branch_outs, proj, proj, bonus, og, proj, proj, proj, hgn, gdn, lng, lnb, wb, wo)


def _final_norm_kernel(x_ref, g_ref, o_ref):
    x = x_ref[...]
    ms = jnp.mean(x * x, axis=-1, keepdims=True)
    o_ref[...] = x * lax.rsqrt(ms + NORM_EPS) * g_ref[...]


def _final_norm(x, g, tm, row0, rows):
    d = x.shape[1]
    first = row0 // tm
    return pl.pallas_call(
        _final_norm_kernel,
        grid=(rows // tm,),
        in_specs=[pl.BlockSpec((tm, d), lambda i: (first + i, 0)), pl.BlockSpec((1, d), lambda i: (0, 0))],
        out_specs=pl.BlockSpec((tm, d), lambda i: (i, 0)),
        out_shape=jax.ShapeDtypeStruct((rows, d), f32),
        compiler_params=_cparams(("arbitrary",)),
        name="final_norm",
    )(x, g.reshape(1, d))


def _pad_rows(w, rows):
    return jnp.concatenate([w, jnp.zeros((rows - w.shape[0],) + w.shape[1:], w.dtype)], axis=0)


def kernel(x_prompt, x_sample, state_hgrn, state_gdn, state_rwkv, c, c_ctx, mod_w, mod_b, norm_g, ffn_up,
           ffn_down, mix_in, hgrn_lb, hgrn_norm, gdn_conv, gdn_a_log, gdn_dt_bias, gdn_norm, rwkv_w0, rwkv_w2,
           rwkv_a0, rwkv_a2, rwkv_g2, rwkv_kk, rwkv_ka, rwkv_rk, rwkv_ln_g, rwkv_ln_b, mix_branch, mix_out,
           final_norm):
    bc, lc, d = x_prompt.shape
    bs, ls, _ = x_sample.shape
    depth = mod_w.shape[0]
    bw = d // 4
    tb = lc
    assert tb % GD_C == 0 and ls % tb == 0 and bw % LANES == 0 and tb % GRID_W == 0
    assert bs + 1 <= N_COND_PAD
    n_ctx = bc
    bps = ls // tb
    tc = bc * lc
    t = tc + bs * ls
    gd_h, rw_h = bw // GD_DK, bw // RW_DK
    lw_w, la_w, lg_w = rwkv_w2.shape[2], rwkv_a2.shape[2], rwkv_g2.shape[1]
    assert max(2 * lw_w, 2 * la_w, lg_w, 4 * gd_h) <= LANES

    def token_tile(cap):
        rows = tb
        while rows * 2 <= cap and tc % (rows * 2) == 0 and ls % (rows * 2) == 0:
            rows *= 2
        return rows

    tm, tp = token_tile(TM_FFN), token_tile(TM_PROJ)
    cond_of_tile = lambda tile_rows: (lambda i: jnp.where(i * tile_rows < tc, 0, 1 + (i * tile_rows - tc) // ls))

    x = jnp.concatenate([x_prompt.reshape(tc, d), x_sample.reshape(bs * ls, d)], axis=0)
    cond = jnp.concatenate([c_ctx[None, :], c, jnp.zeros((N_COND_PAD - 1 - bs, d), f32)], axis=0)
    mod_all = _modulation(cond, mod_w, mod_b).reshape(depth, N_COND_PAD, N_MOD, d)

    lb_p = jax.nn.softmax(hgrn_lb.astype(f32), axis=0)
    lb_all = jnp.cumsum(lb_p, axis=0) - lb_p[0:1]

    o_gda = 9 * bw
    o_rw = 9 * bw + 4 * gd_h
    o_lo = o_rw + 3 * bw
    o_mg = o_lo + 2 * lw_w + 2 * la_w + lg_w
    base = 24 * bw
    pieces = ((0, 9 * bw, 0), (o_rw, 3 * bw, 9 * bw), (o_mg, 3 * d, 12 * bw),
              (o_lo, 2 * lw_w, base), (o_lo + 2 * lw_w, 2 * la_w, base + LANES),
              (o_lo + 2 * lw_w + 2 * la_w, lg_w, base + 2 * LANES), (o_gda, 4 * gd_h, base + 3 * LANES))
    w_in = _relayout_mix_in(mix_in, pieces, base + MISC_W)
    col_gdq, col_gdz, col_rw, col_gate = 5, 8, 9, 3
    col_misc = (24 * bw) // MISC_W
    assert (24 * bw) % MISC_W == 0

    up16 = ffn_up.astype(bf16)
    down16 = ffn_down.astype(bf16)
    wb16 = mix_branch.astype(bf16)
    wo16 = mix_out.astype(bf16)

    exa = np.zeros((2, LANES, bw), np.float32)
    exb = np.zeros((2, LANES, bw), np.float32)
    for dd in range(2):
        for h in range(gd_h):
            exa[dd, dd * gd_h + h, h * GD_DK:(h + 1) * GD_DK] = 1.0
            exb[dd, 2 * gd_h + dd * gd_h + h, h * GD_DK:(h + 1) * GD_DK] = 1.0
    exa, exb = jnp.asarray(exa), jnp.asarray(exb)
    b64 = jnp.asarray(np.kron(np.eye(rw_h), np.ones((RW_DK, RW_DK))).astype(np.float32))

    tables = _block_tables_np(n_ctx, bs, bps)

    s0_hg = jnp.swapaxes(state_hgrn, -1, -2)
    s0_gd = state_gdn
    eye2 = jnp.eye(2, dtype=f32)
    s0_rw = state_rwkv.reshape(bs, depth, 2, rw_h // 2, 2, RW_DK, RW_DK)
    s0_rw = jnp.einsum('bldpeij,ef->bldpeifj', s0_rw, eye2).reshape(bs, depth, 2, rw_h // 2, LANES, LANES)

    new_hg, new_gd, new_rw = [], [], []
    for l in range(depth):
        mod = mod_all
        x = _ffn(x, mod, norm_g, up16, down16, l, 0, cond_of_tile(tm), tm)
        proj = _mixin(x, mod, norm_g, w_in, l, cond_of_tile(tp), tp)

        ohg0, ohg1, s_hg = _hgrn_scan(proj, lb_all[l].reshape(2, 1, bw), s0_hg, l, tables, n_ctx, tb, bw)

        conv_w = gdn_conv[l].reshape(CONV_K * CONV_K, 3 * bw)
        qkv = _gdn_prep(proj, conv_w, n_ctx, bps, tb, bw, col_gdq)
        alog = jnp.repeat(gdn_a_log[l], GD_DK, axis=-1).reshape(2, 1, bw)
        dtb = jnp.repeat(gdn_dt_bias[l], GD_DK, axis=-1).reshape(2, 1, bw)

        w2p = jnp.stack([_pad_rows(jnp.concatenate([jnp.zeros((dd * lw_w, bw), f32), rwkv_w2[l, dd]], 0), LANES)
                         for dd in range(2)])
        a2p = jnp.stack([_pad_rows(jnp.concatenate([jnp.zeros((dd * la_w, bw), f32), rwkv_a2[l, dd]], 0), LANES)
                         for dd in range(2)])
        g2p = _pad_rows(rwkv_g2[l], LANES)
        ka = rwkv_ka[l].reshape(1, bw)
        kk, lw, icl, bonus, og = _rwkv_prep(proj, col_rw, col_misc, rwkv_w0[l], w2p, rwkv_a0[l], a2p, g2p,
                                            rwkv_kk[l].reshape(1, bw), ka, rwkv_rk[l].reshape(1, bw), b64,
                                            tb, bw)
        ogd0, ogd1, s_gd = _gdn_scan(qkv, proj, col_misc, exa, exb, alog, dtb, s0_gd, l, tables, n_ctx, tb, bw)
        orw0, orw1, s_rw = _rwkv_scan(proj, col_rw, kk, lw, icl, ka, s0_rw, l, tables, n_ctx, tb, bw)

        x = _merge(x, mod, proj, (ohg0, ohg1, ogd0, ogd1, orw0, orw1), bonus, og, hgrn_norm[l].reshape(1, bw),
                   jnp.tile(gdn_norm[l], gd_h).reshape(1, bw), rwkv_ln_g[l].reshape(1, bw),
                   rwkv_ln_b[l].reshape(1, bw), wb16, wo16, l, cond_of_tile(tb), tb, bw,
                   4, col_gdz, col_gate)
        x = _ffn(x, mod, norm_g, up16, down16, l, 2, cond_of_tile(tm), tm)

        new_hg.append(s_hg)
        new_gd.append(s_gd)
        new_rw.append(s_rw)

    y_ctx = _final_norm(x, final_norm, tp, 0, tc)
    y_lat = _final_norm(x, final_norm, tp, tc, bs * ls)
    return (y_ctx.reshape(bc, lc, d), y_lat.reshape(bs, ls, d),
            jnp.stack(new_hg, axis=1), jnp.stack(new_gd, axis=1), jnp.stack(new_rw, axis=1))
```

```python
import functools

import numpy as np
import jax
import jax.numpy as jnp
from jax import lax
from jax.experimental import pallas as pl
from jax.experimental.pallas import tpu as pltpu

f32 = jnp.float32
bf16 = jnp.bfloat16

LANES = 128
NORM_ROWS = 16
MXU_W = 256
VMEM_LIMIT = 60 * 1024 * 1024

HG_DK = 128
GD_DK = 128
RW_DK = 64
N_MOD = 9
GRID_W = 64
CONV_K = 3
NORM_EPS = 1e-6
RW_GN_EPS = 64e-5
GATE_FLOOR = 1e-30
N_COND_PAD = 16

HG_C = 64
GD_C = 128
RW_C = 64
SUB = 8
INV_PASSES = 1
MISC_W = 4 * LANES
TM_FFN = 512
TM_PROJ = 1024


def _sigmoid(x):
    return 1.0 / (1.0 + jnp.exp(-x))


def _silu(x):
    return x * _sigmoid(x)


def _softplus(x):
    return jnp.maximum(x, 0.0) + jnp.log(1.0 + jnp.exp(-jnp.abs(x)))


def _dot(a, b):
    return jnp.dot(a.astype(bf16), b.astype(bf16), preferred_element_type=f32)


def _dot_nt(a, b):
    return lax.dot_general(a.astype(bf16), b.astype(bf16), (((1,), (1,)), ((), ())),
                           preferred_element_type=f32)


def _dot_tn(a, b):
    return lax.dot_general(a.astype(bf16), b.astype(bf16), (((0,), (0,)), ((), ())),
                           preferred_element_type=f32)


def _split3(x):
    hi = x.astype(bf16)
    r = x - hi.astype(f32)
    mid = r.astype(bf16)
    lo = (r - mid.astype(f32)).astype(bf16)
    return hi, mid, lo


def _ldot01(m, x):
    hi, mid, lo = _split3(x)
    m = m.astype(bf16)
    d = functools.partial(jnp.dot, preferred_element_type=f32)
    return d(m, hi) + d(m, mid) + d(m, lo)


def _rdot01(x, m):
    hi, mid, lo = _split3(x)
    m = m.astype(bf16)
    d = functools.partial(jnp.dot, preferred_element_type=f32)
    return d(hi, m) + d(mid, m) + d(lo, m)


def _dotp(a, b):
    if INV_PASSES == 1:
        return _dot(a, b)
    ah = a.astype(bf16)
    al = (a - ah.astype(f32)).astype(bf16)
    bh = b.astype(bf16)
    bl = (b - bh.astype(f32)).astype(bf16)
    d = functools.partial(jnp.dot, preferred_element_type=f32)
    return d(ah, bh) + d(ah, bl) + d(al, bh)


def _tri_inverse_y(ns, mdiag, moffs):
    nd = [n * m for n, m in zip(ns, mdiag)]
    n2 = [_dotp(a, a) for a in nd]
    n4 = [_dotp(a, a) for a in n2]
    y = [b - a - _dotp(a, b) for a, b in zip(nd, n2)]
    y = [a + b + _dotp(a, b) for a, b in zip(y, n4)]
    for lvl in range(len(moffs[0])):
        xl = [n * m[lvl] for n, m in zip(ns, moffs)]
        xl = [l + _dotp(a, l) for a, l in zip(y, xl)]
        y = [a - l - _dotp(l, a) for a, l in zip(y, xl)]
    return y


def _normmod_rows(read_rows, h_ref, g, scale, shift):
    gain = g * (1.0 + scale)

    def body(r, carry):
        r0 = pl.multiple_of(r * NORM_ROWS, NORM_ROWS)
        x = read_rows(r0)
        ms = jnp.mean(x * x, axis=-1, keepdims=True)
        h_ref[pl.ds(r0, NORM_ROWS), :] = ((x * lax.rsqrt(ms + NORM_EPS)) * gain + shift).astype(bf16)
        return carry

    lax.fori_loop(0, h_ref.shape[0] // NORM_ROWS, body, 0, unroll=8)


def _tri_np(c, rev):
    i = np.arange(c)
    return (i[None, :] >= i[:, None]) if rev else (i[None, :] <= i[:, None])


def _inverse_masks_np(n, c):
    i = np.arange(n)
    same = (i[:, None] // c) == (i[None, :] // c)
    eye = np.eye(n, dtype=bool)
    mdiag = (i[:, None] // SUB) == (i[None, :] // SUB)
    out = [eye, mdiag]
    m = SUB
    while m < c:
        out.append(same & ((i[:, None] // (2 * m)) == (i[None, :] // (2 * m)))
                   & ((i[:, None] // m) != (i[None, :] // m)))
        m *= 2
    return out


def _causal_np(n, c, rev):
    i = np.arange(n)
    same = (i[:, None] // c) == (i[None, :] // c)
    il, jl = (i % c)[:, None], (i % c)[None, :]
    strict = same & ((jl > il) if rev else (jl < il))
    incl = same & ((jl >= il) if rev else (jl <= il))
    return strict, incl


def _hgrn_tables_np(c):
    levels = []
    m = SUB
    while m < c:
        levels.append(m)
        m *= 2
    stacks, pmasks = [], []
    t = np.arange(c)
    for rev in (False, True):
        mats = [_tri_np(c, rev)]
        pms = []
        for m in levels:
            pair = t // (2 * m)
            half = (t // m) % 2
            tau = t[None, :]
            if not rev:
                bd = (pair * 2 * m + m - 1)[:, None]
                qside = (half == 1)[:, None]
                mat = np.where(qside, (tau > bd) & (tau <= t[:, None]), (tau > t[:, None]) & (tau <= bd))
                pm = (pair[:, None] == pair[None, :]) & (half == 1)[:, None] & (half == 0)[None, :]
            else:
                bd = (pair * 2 * m + m)[:, None]
                qside = (half == 0)[:, None]
                mat = np.where(qside, (tau >= t[:, None]) & (tau < bd), (tau >= bd) & (tau < t[:, None]))
                pm = (pair[:, None] == pair[None, :]) & (half == 0)[:, None] & (half == 1)[None, :]
            mats.append(mat)
            pms.append(pm)
        stacks.append(np.concatenate(mats, axis=0))
        pmasks.append(np.stack(pms, axis=0))
    return (np.stack(stacks).astype(np.float32), np.stack(pmasks).astype(np.float32), len(levels))


def _block_tables_np(n_ctx, n_lat_seq, bps):
    nb = n_ctx + n_lat_seq * bps
    blk = np.zeros((2, nb), np.int32)
    first = np.zeros((nb,), np.int32)
    seq = np.zeros((nb,), np.int32)
    j = 0
    for s in range(n_ctx):
        blk[0, j] = blk[1, j] = s
        first[j], seq[j] = 1, s
        j += 1
    for s in range(n_lat_seq):
        for p in range(bps):
            blk[0, j] = n_ctx + s * bps + p
            blk[1, j] = n_ctx + s * bps + bps - 1 - p
            first[j], seq[j] = int(p == 0), n_ctx + s
            j += 1
    return blk, first, seq


def _pick(n, cap):
    best = LANES
    t = LANES
    while t <= min(n, cap):
        if n % t == 0:
            best = t
        t += LANES
    return best


def _pick_mxu(n, cap):
    best = 0
    t = MXU_W
    while t <= min(n, cap):
        if n % t == 0:
            best = t
        t += MXU_W
    return best or _pick(n, cap)


def _cparams(sem):
    return pltpu.CompilerParams(dimension_semantics=sem, vmem_limit_bytes=VMEM_LIMIT)


def _mod_kernel(cond_ref, w_ref, b_ref, o_ref):
    x = cond_ref[...]
    o_ref[...] = _dot(_silu(x), w_ref[...]) + b_ref[...]


def _modulation(cond, mod_w, mod_b):
    depth, d, nd = mod_w.shape
    tn = _pick(nd, 1024)
    return pl.pallas_call(
        _mod_kernel,
        grid=(depth, nd // tn),
        in_specs=[pl.BlockSpec((N_COND_PAD, d), lambda l, j: (0, 0)),
                  pl.BlockSpec((None, d, tn), lambda l, j: (l, 0, j)),
                  pl.BlockSpec((None, 1, tn), lambda l, j: (l, 0, j))],
        out_specs=pl.BlockSpec((None, N_COND_PAD, tn), lambda l, j: (l, 0, j)),
        out_shape=jax.ShapeDtypeStruct((depth, N_COND_PAD, nd), f32),
        compiler_params=_cparams(("arbitrary", "arbitrary")),
        name="modulation",
    )(cond, mod_w, mod_b.reshape(depth, 1, nd))


def _relayout_kernel(w_ref, o_ref, *, pieces):
    nw = o_ref.shape[1]
    o_ref[:, nw - MISC_W:nw] = jnp.zeros((o_ref.shape[0], MISC_W), bf16)
    for src, n, dst in pieces:
        o_ref[:, dst:dst + n] = w_ref[:, src:src + n].astype(bf16)


def _relayout_mix_in(w, pieces, nw):
    depth, d, win = w.shape
    tr = LANES
    return pl.pallas_call(
        functools.partial(_relayout_kernel, pieces=pieces),
        grid=(depth, d // tr),
        in_specs=[pl.BlockSpec((None, tr, win), lambda l, r: (l, r, 0))],
        out_specs=pl.BlockSpec((None, tr, nw), lambda l, r: (l, r, 0)),
        out_shape=jax.ShapeDtypeStruct((depth, d, nw), bf16),
        compiler_params=_cparams(("arbitrary", "arbitrary")),
        name="relayout_mix_in",
    )(w)


def _ffn_kernel(*refs, slot, nf, split):
    if split is None:
        x_ref, mod_ref, g_ref, wa_ref, wb_ref, wd_ref, o_ref, h_ref = refs
        read_x = lambda rows: x_ref[rows, :]
    else:
        xa_ref, xb_ref, mod_ref, g_ref, wa_ref, wb_ref, wd_ref, o_ref, h_ref = refs
        first = pl.program_id(0) < split
        read_x = lambda rows: jnp.where(first, xa_ref[rows, :], xb_ref[rows, :])
    f = pl.program_id(1)

    @pl.when(f == 0)
    def _():
        _normmod_rows(lambda r0: read_x(pl.ds(r0, NORM_ROWS)), h_ref, g_ref[slot:slot + 1, :],
                      mod_ref[3 * slot + 1:3 * slot + 2, :], mod_ref[3 * slot:3 * slot + 1, :])
        o_ref[...] = jnp.zeros_like(o_ref)

    h = h_ref[...]
    a = jnp.dot(h, wa_ref[...], preferred_element_type=f32)
    b = jnp.dot(h, wb_ref[...], preferred_element_type=f32)
    act = (_silu(a) * b).astype(bf16)
    o_ref[...] += jnp.dot(act, wd_ref[...], preferred_element_type=f32)

    @pl.when(f == nf - 1)
    def _():
        o_ref[...] = read_x(slice(None)) + 0.5 * mod_ref[3 * slot + 2:3 * slot + 3, :] * o_ref[...]


def _ffn(xs, mod, norm_g, w_up, w_down, layer, slot, cond_of_tile, tm):
    if isinstance(xs, tuple):
        xa, xb = xs
        d = xa.shape[1]
        split = xa.shape[0] // tm
        assert xa.shape[0] % tm == 0 and xb.shape[0] % tm == 0
        t = xa.shape[0] + xb.shape[0]
        x_specs = [pl.BlockSpec((tm, d), lambda i, f: (jnp.minimum(i, split - 1), 0)),
                   pl.BlockSpec((tm, d), lambda i, f: (jnp.maximum(i - split, 0), 0))]
        x_args = [xa, xb]
    else:
        t, d = xs.shape
        split = None
        x_specs = [pl.BlockSpec((tm, d), lambda i, f: (i, 0))]
        x_args = [xs]
    dff = w_down.shape[2]
    tf = _pick(dff, 512)
    nf = dff // tf
    which = slot // 2
    return pl.pallas_call(
        functools.partial(_ffn_kernel, slot=slot, nf=nf, split=split),
        grid=(t // tm, nf),
        in_specs=x_specs + [
                  pl.BlockSpec((None, None, N_MOD, d), lambda i, f: (layer, cond_of_tile(i), 0, 0)),
                  pl.BlockSpec((None, 3, d), lambda i, f: (layer, 0, 0)),
                  pl.BlockSpec((None, None, d, tf), lambda i, f: (layer, which, 0, f)),
                  pl.BlockSpec((None, None, d, tf), lambda i, f: (layer, which, 0, f + nf)),
                  pl.BlockSpec((None, None, tf, d), lambda i, f: (layer, which, f, 0))],
        out_specs=pl.BlockSpec((tm, d), lambda i, f: (i, 0)),
        out_shape=jax.ShapeDtypeStruct((t, d), f32),
        scratch_shapes=[pltpu.VMEM((tm, d), bf16)],
        compiler_params=_cparams(("arbitrary", "arbitrary")),
        name="ffn%d" % slot,
    )(*x_args, mod, norm_g, w_up, w_up, w_down)


def _mixin_kernel(x_ref, mod_ref, g_ref, w_ref, o_ref, h_ref):
    @pl.when(pl.program_id(1) == 0)
    def _():
        _normmod_rows(lambda r0: x_ref[pl.ds(r0, NORM_ROWS), :], h_ref, g_ref[1:2, :], mod_ref[4:5, :],
                      mod_ref[3:4, :])

    o_ref[...] = jnp.dot(h_ref[...], w_ref[...], preferred_element_type=f32)


def _mixin(x, mod, norm_g, w, layer, cond_of_tile, tm):
    t, d = x.shape
    nw = w.shape[2]
    tn = _pick_mxu(nw, 1280)
    return pl.pallas_call(
        _mixin_kernel,
        grid=(t // tm, nw // tn),
        in_specs=[pl.BlockSpec((tm, d), lambda i, j: (i, 0)),
                  pl.BlockSpec((None, None, N_MOD, d), lambda i, j: (layer, cond_of_tile(i), 0, 0)),
                  pl.BlockSpec((None, 3, d), lambda i, j: (layer, 0, 0)),
                  pl.BlockSpec((None, d, tn), lambda i, j: (layer, 0, j))],
        out_specs=pl.BlockSpec((tm, tn), lambda i, j: (i, j)),
        out_shape=jax.ShapeDtypeStruct((t, nw), f32),
        scratch_shapes=[pltpu.VMEM((tm, d), bf16)],
        compiler_params=_cparams(("arbitrary", "arbitrary")),
        name="mix_in",
    )(x, mod, norm_g, w)


def _dir_row(d, col):
    return lambda j, b, f, s: (b[d, j], col)


def _dir_slab(d):
    return lambda j, b, f, s: (d, b[d, j], 0)


def _const_spec(shape):
    return pl.BlockSpec(shape, lambda j, b, f, s: (0,) * len(shape))


def _state_in_spec(tail, n_ctx, layer):
    return pl.BlockSpec((None, None) + tail,
                        lambda j, b, f, s: (jnp.maximum(s[j] - n_ctx, 0), layer) + (0,) * len(tail))


def _state_out_spec(tail, n_ctx):
    return pl.BlockSpec((None,) + tail, lambda j, b, f, s: (jnp.minimum(s[j], n_ctx - 1),) + (0,) * len(tail))


def _chunk_rows(ci, nc, c):
    return (pl.multiple_of(ci * c, c), pl.multiple_of((nc - 1 - ci) * c, c))


def _hgrn_kernel(blk_ref, first_ref, seq_ref, q0_ref, v0_ref, f0_ref, q1_ref, v1_ref, f1_ref, lb_ref, s0_ref,
                 mst_ref, pm_ref, o0_ref, o1_ref, sfin_ref, st_ref, qs_ref, ks_ref, lf_ref,
                 *, n_ctx, tb, bw, nlev):
    j = pl.program_id(0)
    c = HG_C
    nc = tb // c
    nh = bw // HG_DK
    is_lat = seq_ref[j] >= n_ctx
    q_refs, v_refs, f_refs, o_refs = (q0_ref, q1_ref), (v0_ref, v1_ref), (f0_ref, f1_ref), (o0_ref, o1_ref)

    @pl.when(first_ref[j] == 1)
    def _():
        st_ref[...] = jnp.where(is_lat, s0_ref[...], 0.0)

    for d in range(2):
        lb = lb_ref[d]
        fl = f_refs[d][...]
        sig = _sigmoid(fl)
        lf_ref[d] = jnp.log(jnp.maximum(lb + (1.0 - lb) * sig, GATE_FLOOR))
        ks_ref[d] = (1.0 - lb) * (1.0 - sig)
        qs_ref[d] = _silu(q_refs[d][...]) * HG_DK ** -0.5

    mst = [mst_ref[d].astype(bf16) for d in range(2)]
    tl = lax.broadcasted_iota(jnp.int32, (c, bw), 0) % SUB
    heads = [slice(h * HG_DK, (h + 1) * HG_DK) for h in range(nh)]
    items = [(d, h) for d in range(2) for h in range(nh)]

    def chunk(ci, carry):
        r0 = _chunk_rows(ci, nc, c)
        qe, kd, etot, qm, km, vs, accs = [], [], [], [], [], [], []
        for d in range(2):
            rows = pl.ds(r0[d], c)
            lf = lf_ref[d, rows, :]
            q = qs_ref[d, rows, :]
            k = ks_ref[d, rows, :]
            v = v_refs[d][rows, :]
            xs = _ldot01(mst[d], lf)
            b = xs[0:c]
            btot = jnp.sum(lf, axis=0, keepdims=True)
            qe.append(q * jnp.exp(b))
            kd.append(k * jnp.exp(btot - b))
            etot.append(jnp.exp(btot))
            es = [jnp.exp(xs[(lv + 1) * c:(lv + 2) * c]) for lv in range(nlev)]
            qm.append([q * e for e in es])
            km.append([k * e for e in es])
            vs.append(v)
            b3 = b.reshape(c // SUB, SUB, bw)
            k3 = k.reshape(c // SUB, SUB, bw)
            v3 = v.reshape(c // SUB, SUB, bw)
            acc = [jnp.zeros((c, HG_DK), f32) for _ in range(nh)]
            for sl in range(SUB):
                bs = jnp.broadcast_to(b3[:, sl:sl + 1, :], b3.shape).reshape(c, bw)
                kk = jnp.broadcast_to(k3[:, sl:sl + 1, :], k3.shape).reshape(c, bw)
                vv = jnp.broadcast_to(v3[:, sl:sl + 1, :], v3.shape).reshape(c, bw)
                valid = (tl >= sl) if d == 0 else (tl <= sl)
                p = jnp.where(valid, q * kk * jnp.exp(jnp.minimum(b - bs, 0.0)), 0.0)
                for h, hs in enumerate(heads):
                    acc[h] = acc[h] + jnp.sum(p[:, hs], axis=-1, keepdims=True) * vv[:, hs]
            accs.append(acc)
        sts = [st_ref[d, h] for d, h in items]
        inter = [_dot_nt(qe[d][:, heads[h]], st) for (d, h), st in zip(items, sts)]
        sc = [jnp.zeros((c, c), f32) for _ in items]
        for lv in range(nlev):
            sc = [s + _dot_nt(qm[d][lv][:, heads[h]], km[d][lv][:, heads[h]]) * pm_ref[d, lv]
                  for (d, h), s in zip(items, sc)]
        outs = [a + accs[d][h] + _dot(s, vs[d][:, heads[h]]) for (d, h), a, s in zip(items, inter, sc)]
        snew = [st * etot[d][:, heads[h]] + _dot_tn(vs[d][:, heads[h]], kd[d][:, heads[h]])
                for (d, h), st in zip(items, sts)]
        for (d, h), o, s in zip(items, outs, snew):
            o_refs[d][pl.ds(r0[d], c), heads[h]] = o
            st_ref[d, h] = s
        return carry

    lax.fori_loop(0, nc, chunk, 0)

    @pl.when(jnp.logical_not(is_lat))
    def _():
        for d, h in items:
            sfin_ref[d, h] = st_ref[d, h].T


def _hgrn_scan(proj, lb, s0t, layer, tables, n_ctx, tb, bw):
    t = proj.shape[0]
    nh = bw // HG_DK
    blk, first, seq = tables
    nb = blk.shape[1]
    mst, pm, nlev = _hgrn_tables_np(HG_C)
    kern = functools.partial(_hgrn_kernel, n_ctx=n_ctx, tb=tb, bw=bw, nlev=nlev)
    tail = (2, nh, HG_DK, HG_DK)
    dir_specs = [pl.BlockSpec((tb, bw), _dir_row(d, col)) for d in range(2) for col in (0, 1, 2 + d)]
    grid_spec = pltpu.PrefetchScalarGridSpec(
        num_scalar_prefetch=3,
        grid=(nb,),
        in_specs=dir_specs + [_const_spec((2, 1, bw)), _state_in_spec(tail, n_ctx, layer),
                              _const_spec(mst.shape), _const_spec(pm.shape)],
        out_specs=[pl.BlockSpec((tb, bw), _dir_row(0, 0)), pl.BlockSpec((tb, bw), _dir_row(1, 0)),
                   _state_out_spec(tail, n_ctx)],
        scratch_shapes=[pltpu.VMEM(tail, f32), pltpu.VMEM((2, tb, bw), f32),
                        pltpu.VMEM((2, tb, bw), f32), pltpu.VMEM((2, tb, bw), f32)],
    )
    return pl.pallas_call(
        kern, grid_spec=grid_spec,
        out_shape=[jax.ShapeDtypeStruct((t, bw), f32), jax.ShapeDtypeStruct((t, bw), f32),
                   jax.ShapeDtypeStruct((n_ctx,) + tail, f32)],
        compiler_params=_cparams(("arbitrary",)),
        name="hgrn_scan",
    )(jnp.asarray(blk), jnp.asarray(first), jnp.asarray(seq), *([proj] * 6), lb, s0t,
      jnp.asarray(mst), jnp.asarray(pm))


def _gdn_prep_kernel(prev_ref, cur_ref, nxt_ref, w_ref, o_ref, xe_ref, *, n_ctx, bps, tb, bw, halo):
    r = pl.program_id(0)
    part = pl.program_id(1)
    is_lat = r >= n_ctx
    pos = jnp.where(is_lat, (r - n_ctx) % bps, 0)
    pv = jnp.where(jnp.logical_and(is_lat, pos > 0), 1.0, 0.0)
    nv = jnp.where(jnp.logical_and(is_lat, pos < bps - 1), 1.0, 0.0)
    lat = jnp.where(is_lat, 1.0, 0.0)
    xe_ref[0:halo, :] = prev_ref[...] * pv
    xe_ref[halo:halo + tb, :] = cur_ref[...]
    xe_ref[halo + tb:halo + tb + halo, :] = nxt_ref[...] * nv
    col = lax.broadcasted_iota(jnp.int32, (tb, LANES), 0) % GRID_W
    m_lo = jnp.where(col >= 1, 1.0, 1.0 - lat)
    m_hi = jnp.where(col <= GRID_W - 2, 1.0, 1.0 - lat)
    for g in range(bw // LANES):
        gs = slice(g * LANES, (g + 1) * LANES)
        acc = None
        for dc in (-1, 0, 1):
            col_sum = None
            for dr in (-1, 0, 1):
                off = halo + GRID_W * dr + dc
                tap = (dr + 1) * CONV_K + (dc + 1)
                w = w_ref[tap:tap + 1, gs]
                if dr != 0:
                    w = w * lat
                term = xe_ref[off:off + tb, gs] * w
                col_sum = term if col_sum is None else col_sum + term
            if dc == -1:
                col_sum = col_sum * m_lo
            elif dc == 1:
                col_sum = col_sum * m_hi
            acc = col_sum if acc is None else acc + col_sum
        y = _silu(acc)
        ss = jnp.sum(y * y, axis=-1, keepdims=True)
        nrm = y * lax.rsqrt(ss + NORM_EPS)
        scale = jnp.where(part == 0, GD_DK ** -0.5, 1.0)
        o_ref[:, gs] = jnp.where(part == 2, y, nrm * scale)


def _gdn_prep(proj, conv_w, n_ctx, bps, tb, bw, qkv_col):
    t = proj.shape[0]
    nb = t // tb
    halo = LANES
    hb = tb // halo
    nhb = t // halo
    kern = functools.partial(_gdn_prep_kernel, n_ctx=n_ctx, bps=bps, tb=tb, bw=bw, halo=halo)
    return pl.pallas_call(
        kern,
        grid=(nb, 3),
        in_specs=[pl.BlockSpec((halo, bw), lambda r, p: (jnp.maximum(r * hb - 1, 0), qkv_col + p)),
                  pl.BlockSpec((tb, bw), lambda r, p: (r, qkv_col + p)),
                  pl.BlockSpec((halo, bw), lambda r, p: (jnp.minimum((r + 1) * hb, nhb - 1), qkv_col + p)),
                  pl.BlockSpec((CONV_K * CONV_K, bw), lambda r, p: (0, p))],
        out_specs=pl.BlockSpec((tb, bw), lambda r, p: (r, p)),
        out_shape=jax.ShapeDtypeStruct((t, 3 * bw), f32),
        scratch_shapes=[pltpu.VMEM((tb + 2 * halo, bw), f32)],
        compiler_params=_cparams(("arbitrary", "arbitrary")),
        name="gdn_prep",
    )(proj, proj, proj, conv_w)


def _gdn_kernel(blk_ref, first_ref, seq_ref, q0_ref, k0_ref, v0_ref, m0_ref, q1_ref, k1_ref, v1_ref, m1_ref,
                exa_ref, exb_ref, alog_ref, dtb_ref, s0_ref, tri_ref, msk_ref, o0_ref, o1_ref, sfin_ref,
                st_ref, lg_ref, be_ref, *, n_ctx, tb, bw, nmerge):
    j = pl.program_id(0)
    c = GD_C
    nc = tb // c
    nh = bw // GD_DK
    is_lat = seq_ref[j] >= n_ctx
    q_refs, k_refs, v_refs = (q0_ref, q1_ref), (k0_ref, k1_ref), (v0_ref, v1_ref)
    m_refs, o_refs = (m0_ref, m1_ref), (o0_ref, o1_ref)

    @pl.when(first_ref[j] == 1)
    def _():
        st_ref[...] = jnp.where(is_lat, s0_ref[...], 0.0)

    for d in range(2):
        ab = m_refs[d][:, MISC_W - LANES:MISC_W]
        a_in = _rdot01(ab, exa_ref[d])
        b_in = _rdot01(ab, exb_ref[d])
        lg_ref[d] = -jnp.exp(alog_ref[d]) * _softplus(a_in + dtb_ref[d])
        be_ref[d] = _sigmoid(b_in)

    tri = [tri_ref[d].astype(bf16) for d in range(2)]
    heads = [slice(h * GD_DK, (h + 1) * GD_DK) for h in range(nh)]
    items = [(d, h) for d in range(2) for h in range(nh)]
    strict = [msk_ref[d, 0] for d, _ in items]
    incl = [msk_ref[d, 1] for d, _ in items]
    mdiag = [msk_ref[d, 3] for d, _ in items]
    moffs = [[msk_ref[d, 4 + m] for m in range(nmerge)] for d, _ in items]

    def chunk(ci, carry):
        r0 = _chunk_rows(ci, nc, c)
        be, q, k, v, gcum, eg, qdec, kdec, gend = [], [], [], [], [], [], [], [], []
        for d in range(2):
            rows = pl.ds(r0[d], c)
            lg = lg_ref[d, rows, :]
            be.append(be_ref[d, rows, :])
            q.append(q_refs[d][rows, :])
            k.append(k_refs[d][rows, :])
            v.append(v_refs[d][rows, :])
            g = _ldot01(tri[d], lg)
            gtot = jnp.sum(lg, axis=0, keepdims=True)
            gcum.append(g)
            eg.append(jnp.exp(g))
            qdec.append(q[d] * eg[d])
            kdec.append(k[d] * jnp.exp(gtot - g))
            gend.append(jnp.exp(gtot))
        qk = [_dot_nt(jnp.concatenate([q[d][:, heads[h]], k[d][:, heads[h]]], axis=0), k[d][:, heads[h]])
              for d, h in items]
        dec = []
        for d, h in items:
            gc = gcum[d][:, heads[h]]
            dec.append(jnp.exp(jnp.minimum(gc - gc.T, 0.0)))
        ns = [be[d][:, heads[h]] * a[c:2 * c] * e * m for (d, h), a, e, m in zip(items, qk, dec, strict)]
        ys = _tri_inverse_y(ns, mdiag, moffs)
        rhs = [jnp.concatenate([be[d][:, heads[h]] * eg[d][:, heads[h]] * k[d][:, heads[h]],
                                be[d][:, heads[h]] * v[d][:, heads[h]]], axis=1) for d, h in items]
        wu = [r + _dotp(y, r) for y, r in zip(ys, rhs)]
        ps = [a[0:c] * e * m for a, e, m in zip(qk, dec, incl)]
        ss = [st_ref[d, h] for d, h in items]
        ws = [_dot(jnp.concatenate([x[:, 0:GD_DK], qdec[d][:, heads[h]]], axis=0), s)
              for (d, h), x, s in zip(items, wu, ss)]
        vnew = [x[:, GD_DK:2 * GD_DK] - y[0:c] for x, y in zip(wu, ws)]
        outs = [y[c:2 * c] + _dot(p, vn) for y, p, vn in zip(ws, ps, vnew)]
        snew = [gend[d][:, heads[h]] * s + _dot_tn(kdec[d][:, heads[h]], vn)
                for (d, h), s, vn in zip(items, ss, vnew)]
        for (d, h), o, s in zip(items, outs, snew):
            o_refs[d][pl.ds(r0[d], c), heads[h]] = o
            st_ref[d, h] = s
        return carry

    lax.fori_loop(0, nc, chunk, 0, unroll=True)

    @pl.when(jnp.logical_not(is_lat))
    def _():
        sfin_ref[...] = st_ref[...]


def _gdn_scan(qkv, proj, misc_col, exa, exb, alog, dtb, s0, layer, tables, n_ctx, tb, bw):
    t = proj.shape[0]
    nh = bw // GD_DK
    blk, first, seq = tables
    nb = blk.shape[1]
    c = GD_C
    tri = np.stack([_tri_np(c, False), _tri_np(c, True)]).astype(np.float32)
    inv = _inverse_masks_np(c, c)
    msk = np.stack([np.stack(list(_causal_np(c, c, rev)) + inv) for rev in (False, True)]).astype(np.float32)
    nmerge = len(inv) - 2
    kern = functools.partial(_gdn_kernel, n_ctx=n_ctx, tb=tb, bw=bw, nmerge=nmerge)
    tail = (2, nh, GD_DK, GD_DK)
    dir_specs = []
    for d in range(2):
        dir_specs += [pl.BlockSpec((tb, bw), _dir_row(d, col)) for col in range(3)]
        dir_specs += [pl.BlockSpec((tb, MISC_W), _dir_row(d, misc_col))]
    grid_spec = pltpu.PrefetchScalarGridSpec(
        num_scalar_prefetch=3,
        grid=(nb,),
        in_specs=dir_specs + [_const_spec((2, LANES, bw)), _const_spec((2, LANES, bw)), _const_spec((2, 1, bw)),
                              _const_spec((2, 1, bw)), _state_in_spec(tail, n_ctx, layer),
                              _const_spec(tri.shape), _const_spec(msk.shape)],
        out_specs=[pl.BlockSpec((tb, bw), _dir_row(0, 0)), pl.BlockSpec((tb, bw), _dir_row(1, 0)),
                   _state_out_spec(tail, n_ctx)],
        scratch_shapes=[pltpu.VMEM(tail, f32), pltpu.VMEM((2, tb, bw), f32), pltpu.VMEM((2, tb, bw), f32)],
    )
    return pl.pallas_call(
        kern, grid_spec=grid_spec,
        out_shape=[jax.ShapeDtypeStruct((t, bw), f32), jax.ShapeDtypeStruct((t, bw), f32),
                   jax.ShapeDtypeStruct((n_ctx,) + tail, f32)],
        compiler_params=_cparams(("arbitrary",)),
        name="gdn_scan",
    )(jnp.asarray(blk), jnp.asarray(first), jnp.asarray(seq), qkv, qkv, qkv, proj, qkv, qkv, qkv, proj,
      exa, exb, alog, dtb, s0, jnp.asarray(tri), jnp.asarray(msk))


def _rwkv_prep_kernel(r_ref, k_ref, v_ref, misc_ref, w0_ref, w2_ref, a0_ref, a2_ref, g2_ref, kkp_ref,
                      ka_ref, rk_ref, b64_ref, kk_ref, lw_ref, icl_ref, bonus_ref, og_ref):
    rr = r_ref[...]
    rk = k_ref[...]
    wl = jnp.tanh(misc_ref[:, 0:LANES])
    al = misc_ref[:, LANES:2 * LANES]
    gl = _sigmoid(misc_ref[:, 2 * LANES:3 * LANES])
    ka = ka_ref[...]
    bon = jnp.zeros_like(rr)
    for d in range(2):
        logw = -_softplus(-(w0_ref[d:d + 1, :] + _dot(wl, w2_ref[d]))) - 0.5
        lw_ref[d] = -jnp.exp(logw)
        icl = _sigmoid(a0_ref[d:d + 1, :] + _dot(al, a2_ref[d]))
        icl_ref[d] = icl
        bon = bon + rr * (rk * (1.0 + (icl - 1.0) * ka)) * rk_ref[...]
    og_ref[...] = _dot(gl, g2_ref[...])
    b64 = b64_ref[...]
    kx = rk * kkp_ref[...]
    kk_ref[...] = kx * lax.rsqrt(_rdot01(kx * kx, b64) + NORM_EPS)
    bonus_ref[...] = _rdot01(bon, b64) * v_ref[...]


def _rwkv_prep(proj, rkv_col, misc_col, w0, w2p, a0, a2p, g2p, kkp, ka, rkp, b64, tb, bw):
    t = proj.shape[0]
    full = lambda shape: pl.BlockSpec(shape, lambda r: (0,) * len(shape))
    return pl.pallas_call(
        _rwkv_prep_kernel,
        grid=(t // tb,),
        in_specs=[pl.BlockSpec((tb, bw), lambda r: (r, rkv_col)),
                  pl.BlockSpec((tb, bw), lambda r: (r, rkv_col + 1)),
                  pl.BlockSpec((tb, bw), lambda r: (r, rkv_col + 2)),
                  pl.BlockSpec((tb, MISC_W), lambda r: (r, misc_col)),
                  full((2, bw)), full((2, LANES, bw)), full((2, bw)), full((2, LANES, bw)),
                  full((LANES, bw)), full((1, bw)), full((1, bw)), full((1, bw)), full((bw, bw))],
        out_specs=[pl.BlockSpec((tb, bw), lambda r: (r, 0)),
                   pl.BlockSpec((2, tb, bw), lambda r: (0, r, 0)),
                   pl.BlockSpec((2, tb, bw), lambda r: (0, r, 0)),
                   pl.BlockSpec((tb, bw), lambda r: (r, 0)),
                   pl.BlockSpec((tb, bw), lambda r: (r, 0))],
        out_shape=[jax.ShapeDtypeStruct((t, bw), f32), jax.ShapeDtypeStruct((2, t, bw), f32),
                   jax.ShapeDtypeStruct((2, t, bw), f32), jax.ShapeDtypeStruct((t, bw), f32),
                   jax.ShapeDtypeStruct((t, bw), f32)],
        compiler_params=_cparams(("arbitrary",)),
        name="rwkv_prep",
    )(proj, proj, proj, proj, w0, w2p, a0, a2p, g2p, kkp, ka, rkp, b64)


def _rwkv_kernel(blk_ref, first_ref, seq_ref, r0_ref, k0_ref, v0_ref, kk0_ref, lw0_ref, icl0_ref,
                 r1_ref, k1_ref, v1_ref, kk1_ref, lw1_ref, icl1_ref, ka_ref, s0_ref, tri_ref, msk_ref,
                 o0_ref, o1_ref, sfin_ref, st_ref, *, n_ctx, tb, bw, nmerge):
    j = pl.program_id(0)
    c = RW_C
    nc = tb // c
    n2c = 2 * c
    npair = bw // LANES
    is_lat = seq_ref[j] >= n_ctx
    r_refs, k_refs, v_refs = (r0_ref, r1_ref), (k0_ref, k1_ref), (v0_ref, v1_ref)
    kk_refs, lw_refs, icl_refs, o_refs = (kk0_ref, kk1_ref), (lw0_ref, lw1_ref), (icl0_ref, icl1_ref), (o0_ref, o1_ref)

    @pl.when(first_ref[j] == 1)
    def _():
        st_ref[...] = jnp.where(is_lat, s0_ref[...], 0.0)

    tri = [tri_ref[d].astype(bf16) for d in range(2)]
    pairs = [slice(p * LANES, (p + 1) * LANES) for p in range(npair)]
    items = [(d, p) for d in range(2) for p in range(npair)]
    strict = [msk_ref[d, 0] for d, _ in items]
    incl2 = [jnp.concatenate([msk_ref[d, 1], msk_ref[d, 1]], axis=1) for d in range(2)]
    mdiag = [msk_ref[d, 3] for d, _ in items]
    moffs = [[msk_ref[d, 4 + m] for m in range(nmerge)] for d, _ in items]
    lane = lax.broadcasted_iota(jnp.int32, (c, LANES), 1)
    m_even = jnp.where(lane < RW_DK, 1.0, 0.0)
    m_odd = 1.0 - m_even
    ka = ka_ref[...]

    def stack2(x):
        return jnp.concatenate([x * m_even, x * m_odd], axis=0)

    def chunk(ci, carry):
        r0 = _chunk_rows(ci, nc, c)
        at, kt, bt, rt, rv, kdec, bdec, etot = [], [], [], [], [], [], [], []
        for d in range(2):
            rows = pl.ds(r0[d], c)
            lw = lw_refs[d][rows, :]
            icl = icl_refs[d][rows, :]
            kk = kk_refs[d][rows, :]
            cum = _ldot01(tri[d], lw)
            ctot = jnp.sum(lw, axis=0, keepdims=True)
            kdir = k_refs[d][rows, :] * (1.0 + (icl - 1.0) * ka)
            bvec = kk * icl
            einv = jnp.exp(-cum)
            edec = jnp.exp(ctot - cum)
            at.append((-kk) * jnp.exp(cum - lw))
            kt.append(kdir * einv)
            bt.append(bvec * einv)
            rt.append(r_refs[d][rows, :] * jnp.exp(cum))
            rv.append(v_refs[d][rows, :])
            kdec.append(kdir * edec)
            bdec.append(bvec * edec)
            etot.append(jnp.exp(ctot))
        a2 = [stack2(at[d][:, pairs[p]]) for d, p in items]
        r2 = [stack2(rt[d][:, pairs[p]]) for d, p in items]
        v2 = [stack2(rv[d][:, pairs[p]]) for d, p in items]
        kb2 = [jnp.concatenate([stack2(kt[d][:, pairs[p]]), stack2(bt[d][:, pairs[p]])], axis=0) for d, p in items]
        kbd2 = [jnp.concatenate([stack2(kdec[d][:, pairs[p]]), stack2(bdec[d][:, pairs[p]])], axis=0)
                for d, p in items]
        big = [_dot_nt(jnp.concatenate([a, r], axis=0), kb) for a, r, kb in zip(a2, r2, kb2)]
        aak = [x[0:n2c, 0:n2c] * m for x, m in zip(big, strict)]
        ns = [-(x[0:n2c, n2c:2 * n2c] * m) for x, m in zip(big, strict)]
        arkb = [x[n2c:2 * n2c] * incl2[d] for x, (d, _) in zip(big, items)]
        ys = _tri_inverse_y(ns, mdiag, moffs)
        rhs = [jnp.concatenate([a, _dot(m, v)], axis=1) for a, m, v in zip(a2, aak, v2)]
        tat = [x + _dotp(y, x) for y, x in zip(ys, rhs)]
        sts = [st_ref[d, p] for d, p in items]
        ur = [_dot_nt(jnp.concatenate([x[:, 0:LANES], r], axis=0), st) for x, r, st in zip(tat, r2, sts)]
        u2 = [y[0:n2c] + x[:, LANES:2 * LANES] for y, x in zip(ur, tat)]
        vu = [jnp.concatenate([v, u], axis=0) for v, u in zip(v2, u2)]
        y2 = [y[n2c:2 * n2c] + _dot(m, x) for y, m, x in zip(ur, arkb, vu)]
        snew = [st * etot[d][:, pairs[p]] + _dot_tn(x, kbd) for (d, p), st, x, kbd in zip(items, sts, vu, kbd2)]
        for (d, p), y, s in zip(items, y2, snew):
            o_refs[d][pl.ds(r0[d], c), pairs[p]] = y[0:c] + y[c:n2c]
            st_ref[d, p] = s
        return carry

    lax.fori_loop(0, nc, chunk, 0, unroll=True)

    @pl.when(jnp.logical_not(is_lat))
    def _():
        for d, p in items:
            st = st_ref[d, p]
            sfin_ref[d, 2 * p] = st[0:RW_DK, 0:RW_DK]
            sfin_ref[d, 2 * p + 1] = pltpu.roll(st, RW_DK, axis=1)[RW_DK:LANES, 0:RW_DK]


def _rwkv_scan(proj, rkv_col, kk, lw, icl, ka, s0p, layer, tables, n_ctx, tb, bw):
    t = proj.shape[0]
    npair = bw // LANES
    blk, first, seq = tables
    nb = blk.shape[1]
    c = RW_C
    n = 2 * c
    tri = np.stack([_tri_np(c, False), _tri_np(c, True)]).astype(np.float32)
    inv = _inverse_masks_np(n, c)
    msk = np.stack([np.stack(list(_causal_np(n, c, rev)) + inv) for rev in (False, True)]).astype(np.float32)
    nmerge = len(inv) - 2
    kern = functools.partial(_rwkv_kernel, n_ctx=n_ctx, tb=tb, bw=bw, nmerge=nmerge)
    tail = (2, npair, LANES, LANES)
    head_tail = (2, 2 * npair, RW_DK, RW_DK)
    dir_specs, dir_args = [], []
    for d in range(2):
        dir_specs += [pl.BlockSpec((tb, bw), _dir_row(d, rkv_col + i)) for i in range(3)]
        dir_specs += [pl.BlockSpec((tb, bw), _dir_row(d, 0)),
                      pl.BlockSpec((None, tb, bw), _dir_slab(d)), pl.BlockSpec((None, tb, bw), _dir_slab(d))]
        dir_args += [proj, proj, proj, kk, lw, icl]
    grid_spec = pltpu.PrefetchScalarGridSpec(
        num_scalar_prefetch=3,
        grid=(nb,),
        in_specs=dir_specs + [_const_spec((1, bw)), _state_in_spec(tail, n_ctx, layer),
                              _const_spec(tri.shape), _const_spec(msk.shape)],
        out_specs=[pl.BlockSpec((tb, bw), _dir_row(0, 0)), pl.BlockSpec((tb, bw), _dir_row(1, 0)),
                   _state_out_spec(head_tail, n_ctx)],
        scratch_shapes=[pltpu.VMEM(tail, f32)],
    )
    return pl.pallas_call(
        kern, grid_spec=grid_spec,
        out_shape=[jax.ShapeDtypeStruct((t, bw), f32), jax.ShapeDtypeStruct((t, bw), f32),
                   jax.ShapeDtypeStruct((n_ctx,) + head_tail, f32)],
        compiler_params=_cparams(("arbitrary",)),
        name="rwkv_scan",
    )(jnp.asarray(blk), jnp.asarray(first), jnp.asarray(seq), *dir_args, ka, s0p,
      jnp.asarray(tri), jnp.asarray(msk))


def _merge_kernel(x_ref, mod_ref, ohg0_ref, ohg1_ref, ogd0_ref, ogd1_ref, orw0_ref, orw1_ref, hgg_ref, gdz_ref,
                  bonus_ref, og_ref, g0_ref, g1_ref, g2_ref, hgn_ref, gdn_ref, lng_ref, lnb_ref, wb_ref, wo_ref,
                  o_ref, *, bw):
    def head_mean(x, width):
        outs = []
        for g in range(bw // LANES):
            xg = x[:, g * LANES:(g + 1) * LANES]
            if width == LANES:
                outs.append(jnp.broadcast_to(jnp.mean(xg, axis=-1, keepdims=True), xg.shape))
            else:
                lane = lax.broadcasted_iota(jnp.int32, xg.shape, 1)
                lo = lane < width
                s_lo = jnp.sum(jnp.where(lo, xg, 0.0), axis=-1, keepdims=True)
                s_hi = jnp.sum(jnp.where(lo, 0.0, xg), axis=-1, keepdims=True)
                outs.append(jnp.where(lo, s_lo, s_hi) * (1.0 / width))
        return jnp.concatenate(outs, axis=-1)

    ohg = ohg0_ref[...] + ohg1_ref[...]
    ohg = ohg * lax.rsqrt(head_mean(ohg * ohg, HG_DK) + NORM_EPS) * hgn_ref[...]
    ohg = ohg * _silu(hgg_ref[...])
    ogd = ogd0_ref[...] + ogd1_ref[...]
    ogd = ogd * lax.rsqrt(head_mean(ogd * ogd, GD_DK) + NORM_EPS) * gdn_ref[...]
    ogd = ogd * _silu(gdz_ref[...])
    orw = orw0_ref[...] + orw1_ref[...]
    mu = head_mean(orw, RW_DK)
    cen = orw - mu
    var = head_mean(cen * cen, RW_DK)
    orw = cen * lax.rsqrt(var + RW_GN_EPS) * lng_ref[...] + lnb_ref[...]
    orw = (orw + bonus_ref[...]) * og_ref[...]
    merged = (_sigmoid(g0_ref[...]) * _dot(ohg, wb_ref[0])
              + _sigmoid(g1_ref[...]) * _dot(ogd, wb_ref[1])
              + _sigmoid(g2_ref[...]) * _dot(orw, wb_ref[2]))
    o_ref[...] = x_ref[...] + mod_ref[5:6, :] * _dot(merged, wo_ref[...])


def _merge(x, mod, proj, branch_outs, bonus, og, hgn, gdn, lng, lnb, wb, wo, layer, cond_of_tile, tm, bw,
           hgg_col, gdz_col, gate_col):
    t, d = x.shape
    row = lambda col: (lambda i: (i, col))
    act = pl.BlockSpec((tm, bw), row(0))
    vec = pl.BlockSpec((1, bw), lambda i: (0, 0))
    once = pl.Buffered(1)
    return pl.pallas_call(
        functools.partial(_merge_kernel, bw=bw),
        grid=(t // tm,),
        in_specs=[pl.BlockSpec((tm, d), row(0)),
                  pl.BlockSpec((None, None, N_MOD, d), lambda i: (layer, cond_of_tile(i), 0, 0)),
                  act, act, act, act, act, act,
                  pl.BlockSpec((tm, bw), row(hgg_col)),
                  pl.BlockSpec((tm, bw), row(gdz_col)),
                  act, act,
                  pl.BlockSpec((tm, d), row(gate_col)),
                  pl.BlockSpec((tm, d), row(gate_col + 1)),
                  pl.BlockSpec((tm, d), row(gate_col + 2)),
                  vec, vec, vec, vec,
                  pl.BlockSpec((None, 3, bw, d), lambda i: (layer, 0, 0, 0), pipeline_mode=once),
                  pl.BlockSpec((None, d, d), lambda i: (layer, 0, 0), pipeline_mode=once)],
        out_specs=pl.BlockSpec((tm, d), row(0)),
        out_shape=jax.ShapeDtypeStruct((t, d), f32),
        compiler_params=_cparams(("arbitrary",)),
        name="merge",
    )(x, mod, *branch_outs, proj, proj, bonus, og, proj, proj, proj, hgn, gdn, lng, lnb, wb, wo)


def _final_norm_kernel(x_ref, g_ref, o_ref):
    x = x_ref[...]
    ms = jnp.mean(x * x, axis=-1, keepdims=True)
    o_ref[...] = x * lax.rsqrt(ms + NORM_EPS) * g_ref[...]


def _final_norm(x, g, tm, row0, rows):
    d = x.shape[1]
    first = row0 // tm
    return pl.pallas_call(
        _final_norm_kernel,
        grid=(rows // tm,),
        in_specs=[pl.BlockSpec((tm, d), lambda i: (first + i, 0)), pl.BlockSpec((1, d), lambda i: (0, 0))],
        out_specs=pl.BlockSpec((tm, d), lambda i: (i, 0)),
        out_shape=jax.ShapeDtypeStruct((rows, d), f32),
        compiler_params=_cparams(("arbitrary",)),
        name="final_norm",
    )(x, g.reshape(1, d))


def _pad_rows(w, rows):
    return jnp.concatenate([w, jnp.zeros((rows - w.shape[0],) + w.shape[1:], w.dtype)], axis=0)


def kernel(x_prompt, x_sample, state_hgrn, state_gdn, state_rwkv, c, c_ctx, mod_w, mod_b, norm_g, ffn_up,
           ffn_down, mix_in, hgrn_lb, hgrn_norm, gdn_conv, gdn_a_log, gdn_dt_bias, gdn_norm, rwkv_w0, rwkv_w2,
           rwkv_a0, rwkv_a2, rwkv_g2, rwkv_kk, rwkv_ka, rwkv_rk, rwkv_ln_g, rwkv_ln_b, mix_branch, mix_out,
           final_norm):
    bc, lc, d = x_prompt.shape
    bs, ls, _ = x_sample.shape
    depth = mod_w.shape[0]
    bw = d // 4
    tb = lc
    assert tb % GD_C == 0 and ls % tb == 0 and bw % LANES == 0 and tb % GRID_W == 0
    assert bs + 1 <= N_COND_PAD
    n_ctx = bc
    bps = ls // tb
    tc = bc * lc
    t = tc + bs * ls
    gd_h, rw_h = bw // GD_DK, bw // RW_DK
    lw_w, la_w, lg_w = rwkv_w2.shape[2], rwkv_a2.shape[2], rwkv_g2.shape[1]
    assert max(2 * lw_w, 2 * la_w, lg_w, 4 * gd_h) <= LANES

    def token_tile(cap):
        rows = tb
        while rows * 2 <= cap and tc % (rows * 2) == 0 and ls % (rows * 2) == 0:
            rows *= 2
        return rows

    tm, tp = token_tile(TM_FFN), token_tile(TM_PROJ)
    cond_of_tile = lambda tile_rows: (lambda i: jnp.where(i * tile_rows < tc, 0, 1 + (i * tile_rows - tc) // ls))

    x = (x_prompt.reshape(tc, d), x_sample.reshape(bs * ls, d))
    cond = jnp.concatenate([c_ctx[None, :], c, jnp.zeros((N_COND_PAD - 1 - bs, d), f32)], axis=0)
    mod_all = _modulation(cond, mod_w, mod_b).reshape(depth, N_COND_PAD, N_MOD, d)

    lb_p = jax.nn.softmax(hgrn_lb.astype(f32), axis=0)
    lb_all = jnp.cumsum(lb_p, axis=0) - lb_p[0:1]

    o_gda = 9 * bw
    o_rw = 9 * bw + 4 * gd_h
    o_lo = o_rw + 3 * bw
    o_mg = o_lo + 2 * lw_w + 2 * la_w + lg_w
    base = 24 * bw
    pieces = ((0, 9 * bw, 0), (o_rw, 3 * bw, 9 * bw), (o_mg, 3 * d, 12 * bw),
              (o_lo, 2 * lw_w, base), (o_lo + 2 * lw_w, 2 * la_w, base + LANES),
              (o_lo + 2 * lw_w + 2 * la_w, lg_w, base + 2 * LANES), (o_gda, 4 * gd_h, base + 3 * LANES))
    w_in = _relayout_mix_in(mix_in, pieces, base + MISC_W)
    col_gdq, col_gdz, col_rw, col_gate = 5, 8, 9, 3
    col_misc = (24 * bw) // MISC_W
    assert (24 * bw) % MISC_W == 0

    up16 = ffn_up.astype(bf16)
    down16 = ffn_down.astype(bf16)
    wb16 = mix_branch.astype(bf16)
    wo16 = mix_out.astype(bf16)

    exa = np.zeros((2, LANES, bw), np.float32)
    exb = np.zeros((2, LANES, bw), np.float32)
    for dd in range(2):
        for h in range(gd_h):
            exa[dd, dd * gd_h + h, h * GD_DK:(h + 1) * GD_DK] = 1.0
            exb[dd, 2 * gd_h + dd * gd_h + h, h * GD_DK:(h + 1) * GD_DK] = 1.0
    exa, exb = jnp.asarray(exa), jnp.asarray(exb)
    b64 = jnp.asarray(np.kron(np.eye(rw_h), np.ones((RW_DK, RW_DK))).astype(np.float32))

    tables = _block_tables_np(n_ctx, bs, bps)

    s0_hg = jnp.swapaxes(state_hgrn, -1, -2)
    s0_gd = state_gdn
    eye2 = jnp.eye(2, dtype=f32)
    s0_rw = state_rwkv.reshape(bs, depth, 2, rw_h // 2, 2, RW_DK, RW_DK)
    s0_rw = jnp.einsum('bldpeij,ef->bldpeifj', s0_rw, eye2).reshape(bs, depth, 2, rw_h // 2, LANES, LANES)

    new_hg, new_gd, new_rw = [], [], []
    for l in range(depth):
        mod = mod_all
        x = _ffn(x, mod, norm_g, up16, down16, l, 0, cond_of_tile(tm), tm)
        proj = _mixin(x, mod, norm_g, w_in, l, cond_of_tile(tp), tp)

        ohg0, ohg1, s_hg = _hgrn_scan(proj, lb_all[l].reshape(2, 1, bw), s0_hg, l, tables, n_ctx, tb, bw)

        conv_w = gdn_conv[l].reshape(CONV_K * CONV_K, 3 * bw)
        qkv = _gdn_prep(proj, conv_w, n_ctx, bps, tb, bw, col_gdq)
        alog = jnp.repeat(gdn_a_log[l], GD_DK, axis=-1).reshape(2, 1, bw)
        dtb = jnp.repeat(gdn_dt_bias[l], GD_DK, axis=-1).reshape(2, 1, bw)

        w2p = jnp.stack([_pad_rows(jnp.concatenate([jnp.zeros((dd * lw_w, bw), f32), rwkv_w2[l, dd]], 0), LANES)
                         for dd in range(2)])
        a2p = jnp.stack([_pad_rows(jnp.concatenate([jnp.zeros((dd * la_w, bw), f32), rwkv_a2[l, dd]], 0), LANES)
                         for dd in range(2)])
        g2p = _pad_rows(rwkv_g2[l], LANES)
        ka = rwkv_ka[l].reshape(1, bw)
        kk, lw, icl, bonus, og = _rwkv_prep(proj, col_rw, col_misc, rwkv_w0[l], w2p, rwkv_a0[l], a2p, g2p,
                                            rwkv_kk[l].reshape(1, bw), ka, rwkv_rk[l].reshape(1, bw), b64,
                                            tb, bw)
        ogd0, ogd1, s_gd = _gdn_scan(qkv, proj, col_misc, exa, exb, alog, dtb, s0_gd, l, tables, n_ctx, tb, bw)
        orw0, orw1, s_rw = _rwkv_scan(proj, col_rw, kk, lw, icl, ka, s0_rw, l, tables, n_ctx, tb, bw)

        x = _merge(x, mod, proj, (ohg0, ohg1, ogd0, ogd1, orw0, orw1), bonus, og, hgrn_norm[l].reshape(1, bw),
                   jnp.tile(gdn_norm[l], gd_h).reshape(1, bw), rwkv_ln_g[l].reshape(1, bw),
                   rwkv_ln_b[l].reshape(1, bw), wb16, wo16, l, cond_of_tile(tb), tb, bw,
                   4, col_gdz, col_gate)
        x = _ffn(x, mod, norm_g, up16, down16, l, 2, cond_of_tile(tm), tm)

        new_hg.append(s_hg)
        new_gd.append(s_gd)
        new_rw.append(s_rw)

    y_ctx = _final_norm(x, final_norm, tp, 0, tc)
    y_lat = _final_norm(x, final_norm, tp, tc, bs * ls)
    return (y_ctx.reshape(bc, lc, d), y_lat.reshape(bs, ls, d),
            jnp.stack(new_hg, axis=1), jnp.stack(new_gd, axis=1), jnp.stack(new_rw, axis=1))
```

```python
import functools

import numpy as np
import jax
import jax.numpy as jnp
from jax import lax
from jax.experimental import pallas as pl
from jax.experimental.pallas import tpu as pltpu

f32 = jnp.float32
bf16 = jnp.bfloat16

LANES = 128
NORM_ROWS = 16
MXU_W = 256
VMEM_LIMIT = 60 * 1024 * 1024

HG_DK = 128
GD_DK = 128
RW_DK = 64
N_MOD = 9
GRID_W = 64
CONV_K = 3
NORM_EPS = 1e-6
RW_GN_EPS = 64e-5
GATE_FLOOR = 1e-30
N_COND_PAD = 16

HG_C = 64
GD_C = 128
RW_C = 64
SUB = 8
INV_PASSES = 1
MISC_W = 4 * LANES
TM_FFN = 1024
TM_PROJ = 1024


def _sigmoid(x):
    return 1.0 / (1.0 + jnp.exp(-x))


def _silu(x):
    return x * _sigmoid(x)


def _softplus(x):
    return jnp.maximum(x, 0.0) + jnp.log(1.0 + jnp.exp(-jnp.abs(x)))


def _dot(a, b):
    return jnp.dot(a.astype(bf16), b.astype(bf16), preferred_element_type=f32)


def _dot_nt(a, b):
    return lax.dot_general(a.astype(bf16), b.astype(bf16), (((1,), (1,)), ((), ())),
                           preferred_element_type=f32)


def _dot_tn(a, b):
    return lax.dot_general(a.astype(bf16), b.astype(bf16), (((0,), (0,)), ((), ())),
                           preferred_element_type=f32)


def _split3(x):
    hi = x.astype(bf16)
    r = x - hi.astype(f32)
    mid = r.astype(bf16)
    lo = (r - mid.astype(f32)).astype(bf16)
    return hi, mid, lo


def _ldot01(m, x):
    hi, mid, lo = _split3(x)
    m = m.astype(bf16)
    d = functools.partial(jnp.dot, preferred_element_type=f32)
    return d(m, hi) + d(m, mid) + d(m, lo)


def _rdot01(x, m):
    hi, mid, lo = _split3(x)
    m = m.astype(bf16)
    d = functools.partial(jnp.dot, preferred_element_type=f32)
    return d(hi, m) + d(mid, m) + d(lo, m)


def _dotp(a, b):
    if INV_PASSES == 1:
        return _dot(a, b)
    ah = a.astype(bf16)
    al = (a - ah.astype(f32)).astype(bf16)
    bh = b.astype(bf16)
    bl = (b - bh.astype(f32)).astype(bf16)
    d = functools.partial(jnp.dot, preferred_element_type=f32)
    return d(ah, bh) + d(ah, bl) + d(al, bh)


def _tri_inverse_y(ns, mdiag, moffs):
    nd = [n * m for n, m in zip(ns, mdiag)]
    n2 = [_dotp(a, a) for a in nd]
    n4 = [_dotp(a, a) for a in n2]
    y = [b - a - _dotp(a, b) for a, b in zip(nd, n2)]
    y = [a + b + _dotp(a, b) for a, b in zip(y, n4)]
    for lvl in range(len(moffs[0])):
        xl = [n * m[lvl] for n, m in zip(ns, moffs)]
        xl = [l + _dotp(a, l) for a, l in zip(y, xl)]
        y = [a - l - _dotp(l, a) for a, l in zip(y, xl)]
    return y


def _normmod_rows(read_rows, h_ref, g, scale, shift):
    gain = g * (1.0 + scale)

    def body(r, carry):
        r0 = pl.multiple_of(r * NORM_ROWS, NORM_ROWS)
        x = read_rows(r0)
        ms = jnp.mean(x * x, axis=-1, keepdims=True)
        h_ref[pl.ds(r0, NORM_ROWS), :] = ((x * lax.rsqrt(ms + NORM_EPS)) * gain + shift).astype(bf16)
        return carry

    lax.fori_loop(0, h_ref.shape[0] // NORM_ROWS, body, 0, unroll=8)


def _tri_np(c, rev):
    i = np.arange(c)
    return (i[None, :] >= i[:, None]) if rev else (i[None, :] <= i[:, None])


def _inverse_masks_np(n, c):
    i = np.arange(n)
    same = (i[:, None] // c) == (i[None, :] // c)
    eye = np.eye(n, dtype=bool)
    mdiag = (i[:, None] // SUB) == (i[None, :] // SUB)
    out = [eye, mdiag]
    m = SUB
    while m < c:
        out.append(same & ((i[:, None] // (2 * m)) == (i[None, :] // (2 * m)))
                   & ((i[:, None] // m) != (i[None, :] // m)))
        m *= 2
    return out


def _causal_np(n, c, rev):
    i = np.arange(n)
    same = (i[:, None] // c) == (i[None, :] // c)
    il, jl = (i % c)[:, None], (i % c)[None, :]
    strict = same & ((jl > il) if rev else (jl < il))
    incl = same & ((jl >= il) if rev else (jl <= il))
    return strict, incl


def _hgrn_tables_np(c):
    levels = []
    m = SUB
    while m < c:
        levels.append(m)
        m *= 2
    stacks, pmasks = [], []
    t = np.arange(c)
    for rev in (False, True):
        mats = [_tri_np(c, rev)]
        pms = []
        for m in levels:
            pair = t // (2 * m)
            half = (t // m) % 2
            tau = t[None, :]
            if not rev:
                bd = (pair * 2 * m + m - 1)[:, None]
                qside = (half == 1)[:, None]
                mat = np.where(qside, (tau > bd) & (tau <= t[:, None]), (tau > t[:, None]) & (tau <= bd))
                pm = (pair[:, None] == pair[None, :]) & (half == 1)[:, None] & (half == 0)[None, :]
            else:
                bd = (pair * 2 * m + m)[:, None]
                qside = (half == 0)[:, None]
                mat = np.where(qside, (tau >= t[:, None]) & (tau < bd), (tau >= bd) & (tau < t[:, None]))
                pm = (pair[:, None] == pair[None, :]) & (half == 0)[:, None] & (half == 1)[None, :]
            mats.append(mat)
            pms.append(pm)
        stacks.append(np.concatenate(mats, axis=0))
        pmasks.append(np.stack(pms, axis=0))
    return (np.stack(stacks).astype(np.float32), np.stack(pmasks).astype(np.float32), len(levels))


def _block_tables_np(n_ctx, n_lat_seq, bps):
    nb = n_ctx + n_lat_seq * bps
    blk = np.zeros((2, nb), np.int32)
    first = np.zeros((nb,), np.int32)
    seq = np.zeros((nb,), np.int32)
    j = 0
    for s in range(n_ctx):
        blk[0, j] = blk[1, j] = s
        first[j], seq[j] = 1, s
        j += 1
    for s in range(n_lat_seq):
        for p in range(bps):
            blk[0, j] = n_ctx + s * bps + p
            blk[1, j] = n_ctx + s * bps + bps - 1 - p
            first[j], seq[j] = int(p == 0), n_ctx + s
            j += 1
    return blk, first, seq


def _pick(n, cap):
    best = LANES
    t = LANES
    while t <= min(n, cap):
        if n % t == 0:
            best = t
        t += LANES
    return best


def _pick_mxu(n, cap):
    best = 0
    t = MXU_W
    while t <= min(n, cap):
        if n % t == 0:
            best = t
        t += MXU_W
    return best or _pick(n, cap)


def _cparams(sem):
    return pltpu.CompilerParams(dimension_semantics=sem, vmem_limit_bytes=VMEM_LIMIT)


def _mod_kernel(cond_ref, w_ref, b_ref, o_ref):
    x = cond_ref[...]
    o_ref[...] = _dot(_silu(x), w_ref[...]) + b_ref[...]


def _modulation(cond, mod_w, mod_b):
    depth, d, nd = mod_w.shape
    tn = _pick(nd, 1024)
    return pl.pallas_call(
        _mod_kernel,
        grid=(depth, nd // tn),
        in_specs=[pl.BlockSpec((N_COND_PAD, d), lambda l, j: (0, 0)),
                  pl.BlockSpec((None, d, tn), lambda l, j: (l, 0, j)),
                  pl.BlockSpec((None, 1, tn), lambda l, j: (l, 0, j))],
        out_specs=pl.BlockSpec((None, N_COND_PAD, tn), lambda l, j: (l, 0, j)),
        out_shape=jax.ShapeDtypeStruct((depth, N_COND_PAD, nd), f32),
        compiler_params=_cparams(("arbitrary", "arbitrary")),
        name="modulation",
    )(cond, mod_w, mod_b.reshape(depth, 1, nd))


def _relayout_kernel(w_ref, o_ref, *, pieces):
    nw = o_ref.shape[1]
    o_ref[:, nw - MISC_W:nw] = jnp.zeros((o_ref.shape[0], MISC_W), bf16)
    for src, n, dst in pieces:
        o_ref[:, dst:dst + n] = w_ref[:, src:src + n].astype(bf16)


def _relayout_mix_in(w, pieces, nw):
    depth, d, win = w.shape
    tr = LANES
    return pl.pallas_call(
        functools.partial(_relayout_kernel, pieces=pieces),
        grid=(depth, d // tr),
        in_specs=[pl.BlockSpec((None, tr, win), lambda l, r: (l, r, 0))],
        out_specs=pl.BlockSpec((None, tr, nw), lambda l, r: (l, r, 0)),
        out_shape=jax.ShapeDtypeStruct((depth, d, nw), bf16),
        compiler_params=_cparams(("arbitrary", "arbitrary")),
        name="relayout_mix_in",
    )(w)


def _ffn_kernel(*refs, slot, nf, split):
    if split is None:
        x_ref, mod_ref, g_ref, wa_ref, wb_ref, wd_ref, o_ref, h_ref = refs
        read_x = lambda rows: x_ref[rows, :]
    else:
        xa_ref, xb_ref, mod_ref, g_ref, wa_ref, wb_ref, wd_ref, o_ref, h_ref = refs
        first = pl.program_id(0) < split
        read_x = lambda rows: jnp.where(first, xa_ref[rows, :], xb_ref[rows, :])
    f = pl.program_id(1)

    @pl.when(f == 0)
    def _():
        _normmod_rows(lambda r0: read_x(pl.ds(r0, NORM_ROWS)), h_ref, g_ref[slot:slot + 1, :],
                      mod_ref[3 * slot + 1:3 * slot + 2, :], mod_ref[3 * slot:3 * slot + 1, :])
        o_ref[...] = jnp.zeros_like(o_ref)

    h = h_ref[...]
    a = jnp.dot(h, wa_ref[...], preferred_element_type=f32)
    b = jnp.dot(h, wb_ref[...], preferred_element_type=f32)
    act = (_silu(a) * b).astype(bf16)
    o_ref[...] += jnp.dot(act, wd_ref[...], preferred_element_type=f32)

    @pl.when(f == nf - 1)
    def _():
        o_ref[...] = read_x(slice(None)) + 0.5 * mod_ref[3 * slot + 2:3 * slot + 3, :] * o_ref[...]


def _ffn(xs, mod, norm_g, w_up, w_down, layer, slot, cond_of_tile, tm):
    if isinstance(xs, tuple):
        xa, xb = xs
        d = xa.shape[1]
        split = xa.shape[0] // tm
        assert xa.shape[0] % tm == 0 and xb.shape[0] % tm == 0
        t = xa.shape[0] + xb.shape[0]
        x_specs = [pl.BlockSpec((tm, d), lambda i, f: (jnp.minimum(i, split - 1), 0), pipeline_mode=pl.Buffered(1)),
                   pl.BlockSpec((tm, d), lambda i, f: (jnp.maximum(i - split, 0), 0), pipeline_mode=pl.Buffered(1))]
        x_args = [xa, xb]
    else:
        t, d = xs.shape
        split = None
        x_specs = [pl.BlockSpec((tm, d), lambda i, f: (i, 0), pipeline_mode=pl.Buffered(1))]
        x_args = [xs]
    dff = w_down.shape[2]
    tf = _pick(dff, 512)
    nf = dff // tf
    which = slot // 2
    return pl.pallas_call(
        functools.partial(_ffn_kernel, slot=slot, nf=nf, split=split),
        grid=(t // tm, nf),
        in_specs=x_specs + [
                  pl.BlockSpec((None, None, N_MOD, d), lambda i, f: (layer, cond_of_tile(i), 0, 0)),
                  pl.BlockSpec((None, 3, d), lambda i, f: (layer, 0, 0)),
                  pl.BlockSpec((None, None, d, tf), lambda i, f: (layer, which, 0, f)),
                  pl.BlockSpec((None, None, d, tf), lambda i, f: (layer, which, 0, f + nf)),
                  pl.BlockSpec((None, None, tf, d), lambda i, f: (layer, which, f, 0))],
        out_specs=pl.BlockSpec((tm, d), lambda i, f: (i, 0)),
        out_shape=jax.ShapeDtypeStruct((t, d), f32),
        scratch_shapes=[pltpu.VMEM((tm, d), bf16)],
        compiler_params=_cparams(("arbitrary", "arbitrary")),
        name="ffn%d" % slot,
    )(*x_args, mod, norm_g, w_up, w_up, w_down)


def _mixin_kernel(x_ref, mod_ref, g_ref, w_ref, o_ref, h_ref):
    @pl.when(pl.program_id(1) == 0)
    def _():
        _normmod_rows(lambda r0: x_ref[pl.ds(r0, NORM_ROWS), :], h_ref, g_ref[1:2, :], mod_ref[4:5, :],
                      mod_ref[3:4, :])

    o_ref[...] = jnp.dot(h_ref[...], w_ref[...], preferred_element_type=f32)


def _mixin(x, mod, norm_g, w, layer, cond_of_tile, tm):
    t, d = x.shape
    nw = w.shape[2]
    tn = _pick_mxu(nw, 1280)
    return pl.pallas_call(
        _mixin_kernel,
        grid=(t // tm, nw // tn),
        in_specs=[pl.BlockSpec((tm, d), lambda i, j: (i, 0)),
                  pl.BlockSpec((None, None, N_MOD, d), lambda i, j: (layer, cond_of_tile(i), 0, 0)),
                  pl.BlockSpec((None, 3, d), lambda i, j: (layer, 0, 0)),
                  pl.BlockSpec((None, d, tn), lambda i, j: (layer, 0, j))],
        out_specs=pl.BlockSpec((tm, tn), lambda i, j: (i, j)),
        out_shape=jax.ShapeDtypeStruct((t, nw), f32),
        scratch_shapes=[pltpu.VMEM((tm, d), bf16)],
        compiler_params=_cparams(("arbitrary", "arbitrary")),
        name="mix_in",
    )(x, mod, norm_g, w)


def _dir_row(d, col):
    return lambda j, b, f, s: (b[d, j], col)


def _dir_slab(d):
    return lambda j, b, f, s: (d, b[d, j], 0)


def _const_spec(shape):
    return pl.BlockSpec(shape, lambda j, b, f, s: (0,) * len(shape))


def _state_in_spec(tail, n_ctx, layer):
    return pl.BlockSpec((None, None) + tail,
                        lambda j, b, f, s: (jnp.maximum(s[j] - n_ctx, 0), layer) + (0,) * len(tail))


def _state_out_spec(tail, n_ctx):
    return pl.BlockSpec((None,) + tail, lambda j, b, f, s: (jnp.minimum(s[j], n_ctx - 1),) + (0,) * len(tail))


def _chunk_rows(ci, nc, c):
    return (pl.multiple_of(ci * c, c), pl.multiple_of((nc - 1 - ci) * c, c))


def _hgrn_kernel(blk_ref, first_ref, seq_ref, q0_ref, v0_ref, f0_ref, q1_ref, v1_ref, f1_ref, lb_ref, s0_ref,
                 mst_ref, pm_ref, o0_ref, o1_ref, sfin_ref, st_ref, qs_ref, ks_ref, lf_ref,
                 *, n_ctx, tb, bw, nlev):
    j = pl.program_id(0)
    c = HG_C
    nc = tb // c
    nh = bw // HG_DK
    is_lat = seq_ref[j] >= n_ctx
    q_refs, v_refs, f_refs, o_refs = (q0_ref, q1_ref), (v0_ref, v1_ref), (f0_ref, f1_ref), (o0_ref, o1_ref)

    @pl.when(first_ref[j] == 1)
    def _():
        st_ref[...] = jnp.where(is_lat, s0_ref[...], 0.0)

    for d in range(2):
        lb = lb_ref[d]
        fl = f_refs[d][...]
        sig = _sigmoid(fl)
        lf_ref[d] = jnp.log(jnp.maximum(lb + (1.0 - lb) * sig, GATE_FLOOR))
        ks_ref[d] = (1.0 - lb) * (1.0 - sig)
        qs_ref[d] = _silu(q_refs[d][...]) * HG_DK ** -0.5

    mst = [mst_ref[d].astype(bf16) for d in range(2)]
    tl = lax.broadcasted_iota(jnp.int32, (c, bw), 0) % SUB
    heads = [slice(h * HG_DK, (h + 1) * HG_DK) for h in range(nh)]
    items = [(d, h) for d in range(2) for h in range(nh)]

    def chunk(ci, carry):
        r0 = _chunk_rows(ci, nc, c)
        qe, kd, etot, qm, km, vs, accs = [], [], [], [], [], [], []
        for d in range(2):
            rows = pl.ds(r0[d], c)
            lf = lf_ref[d, rows, :]
            q = qs_ref[d, rows, :]
            k = ks_ref[d, rows, :]
            v = v_refs[d][rows, :]
            xs = _ldot01(mst[d], lf)
            b = xs[0:c]
            btot = jnp.sum(lf, axis=0, keepdims=True)
            qe.append(q * jnp.exp(b))
            kd.append(k * jnp.exp(btot - b))
            etot.append(jnp.exp(btot))
            es = [jnp.exp(xs[(lv + 1) * c:(lv + 2) * c]) for lv in range(nlev)]
            qm.append([q * e for e in es])
            km.append([k * e for e in es])
            vs.append(v)
            b3 = b.reshape(c // SUB, SUB, bw)
            k3 = k.reshape(c // SUB, SUB, bw)
            v3 = v.reshape(c // SUB, SUB, bw)
            acc = [jnp.zeros((c, HG_DK), f32) for _ in range(nh)]
            for sl in range(SUB):
                bs = jnp.broadcast_to(b3[:, sl:sl + 1, :], b3.shape).reshape(c, bw)
                kk = jnp.broadcast_to(k3[:, sl:sl + 1, :], k3.shape).reshape(c, bw)
                vv = jnp.broadcast_to(v3[:, sl:sl + 1, :], v3.shape).reshape(c, bw)
                valid = (tl >= sl) if d == 0 else (tl <= sl)
                p = jnp.where(valid, q * kk * jnp.exp(jnp.minimum(b - bs, 0.0)), 0.0)
                for h, hs in enumerate(heads):
                    acc[h] = acc[h] + jnp.sum(p[:, hs], axis=-1, keepdims=True) * vv[:, hs]
            accs.append(acc)
        sts = [st_ref[d, h] for d, h in items]
        inter = [_dot_nt(qe[d][:, heads[h]], st) for (d, h), st in zip(items, sts)]
        sc = [jnp.zeros((c, c), f32) for _ in items]
        for lv in range(nlev):
            sc = [s + _dot_nt(qm[d][lv][:, heads[h]], km[d][lv][:, heads[h]]) * pm_ref[d, lv]
                  for (d, h), s in zip(items, sc)]
        outs = [a + accs[d][h] + _dot(s, vs[d][:, heads[h]]) for (d, h), a, s in zip(items, inter, sc)]
        snew = [st * etot[d][:, heads[h]] + _dot_tn(vs[d][:, heads[h]], kd[d][:, heads[h]])
                for (d, h), st in zip(items, sts)]
        for (d, h), o, s in zip(items, outs, snew):
            o_refs[d][pl.ds(r0[d], c), heads[h]] = o
            st_ref[d, h] = s
        return carry

    lax.fori_loop(0, nc, chunk, 0)

    @pl.when(jnp.logical_not(is_lat))
    def _():
        for d, h in items:
            sfin_ref[d, h] = st_ref[d, h].T


def _hgrn_scan(proj, lb, s0t, layer, tables, n_ctx, tb, bw):
    t = proj.shape[0]
    nh = bw // HG_DK
    blk, first, seq = tables
    nb = blk.shape[1]
    mst, pm, nlev = _hgrn_tables_np(HG_C)
    kern = functools.partial(_hgrn_kernel, n_ctx=n_ctx, tb=tb, bw=bw, nlev=nlev)
    tail = (2, nh, HG_DK, HG_DK)
    dir_specs = [pl.BlockSpec((tb, bw), _dir_row(d, col)) for d in range(2) for col in (0, 1, 2 + d)]
    grid_spec = pltpu.PrefetchScalarGridSpec(
        num_scalar_prefetch=3,
        grid=(nb,),
        in_specs=dir_specs + [_const_spec((2, 1, bw)), _state_in_spec(tail, n_ctx, layer),
                              _const_spec(mst.shape), _const_spec(pm.shape)],
        out_specs=[pl.BlockSpec((tb, bw), _dir_row(0, 0)), pl.BlockSpec((tb, bw), _dir_row(1, 0)),
                   _state_out_spec(tail, n_ctx)],
        scratch_shapes=[pltpu.VMEM(tail, f32), pltpu.VMEM((2, tb, bw), f32),
                        pltpu.VMEM((2, tb, bw), f32), pltpu.VMEM((2, tb, bw), f32)],
    )
    return pl.pallas_call(
        kern, grid_spec=grid_spec,
        out_shape=[jax.ShapeDtypeStruct((t, bw), f32), jax.ShapeDtypeStruct((t, bw), f32),
                   jax.ShapeDtypeStruct((n_ctx,) + tail, f32)],
        compiler_params=_cparams(("arbitrary",)),
        name="hgrn_scan",
    )(jnp.asarray(blk), jnp.asarray(first), jnp.asarray(seq), *([proj] * 6), lb, s0t,
      jnp.asarray(mst), jnp.asarray(pm))


def _gdn_prep_kernel(prev_ref, cur_ref, nxt_ref, w_ref, o_ref, xe_ref, *, n_ctx, bps, tb, bw, halo):
    r = pl.program_id(0)
    part = pl.program_id(1)
    is_lat = r >= n_ctx
    pos = jnp.where(is_lat, (r - n_ctx) % bps, 0)
    pv = jnp.where(jnp.logical_and(is_lat, pos > 0), 1.0, 0.0)
    nv = jnp.where(jnp.logical_and(is_lat, pos < bps - 1), 1.0, 0.0)
    lat = jnp.where(is_lat, 1.0, 0.0)
    xe_ref[0:halo, :] = prev_ref[...] * pv
    xe_ref[halo:halo + tb, :] = cur_ref[...]
    xe_ref[halo + tb:halo + tb + halo, :] = nxt_ref[...] * nv
    col = lax.broadcasted_iota(jnp.int32, (tb, LANES), 0) % GRID_W
    m_lo = jnp.where(col >= 1, 1.0, 1.0 - lat)
    m_hi = jnp.where(col <= GRID_W - 2, 1.0, 1.0 - lat)
    for g in range(bw // LANES):
        gs = slice(g * LANES, (g + 1) * LANES)
        acc = None
        for dc in (-1, 0, 1):
            col_sum = None
            for dr in (-1, 0, 1):
                off = halo + GRID_W * dr + dc
                tap = (dr + 1) * CONV_K + (dc + 1)
                w = w_ref[tap:tap + 1, gs]
                if dr != 0:
                    w = w * lat
                term = xe_ref[off:off + tb, gs] * w
                col_sum = term if col_sum is None else col_sum + term
            if dc == -1:
                col_sum = col_sum * m_lo
            elif dc == 1:
                col_sum = col_sum * m_hi
            acc = col_sum if acc is None else acc + col_sum
        y = _silu(acc)
        ss = jnp.sum(y * y, axis=-1, keepdims=True)
        nrm = y * lax.rsqrt(ss + NORM_EPS)
        scale = jnp.where(part == 0, GD_DK ** -0.5, 1.0)
        o_ref[:, gs] = jnp.where(part == 2, y, nrm * scale)


def _gdn_prep(proj, conv_w, n_ctx, bps, tb, bw, qkv_col):
    t = proj.shape[0]
    nb = t // tb
    halo = LANES
    hb = tb // halo
    nhb = t // halo
    kern = functools.partial(_gdn_prep_kernel, n_ctx=n_ctx, bps=bps, tb=tb, bw=bw, halo=halo)
    return pl.pallas_call(
        kern,
        grid=(nb, 3),
        in_specs=[pl.BlockSpec((halo, bw), lambda r, p: (jnp.maximum(r * hb - 1, 0), qkv_col + p)),
                  pl.BlockSpec((tb, bw), lambda r, p: (r, qkv_col + p)),
                  pl.BlockSpec((halo, bw), lambda r, p: (jnp.minimum((r + 1) * hb, nhb - 1), qkv_col + p)),
                  pl.BlockSpec((CONV_K * CONV_K, bw), lambda r, p: (0, p))],
        out_specs=pl.BlockSpec((tb, bw), lambda r, p: (r, p)),
        out_shape=jax.ShapeDtypeStruct((t, 3 * bw), f32),
        scratch_shapes=[pltpu.VMEM((tb + 2 * halo, bw), f32)],
        compiler_params=_cparams(("arbitrary", "arbitrary")),
        name="gdn_prep",
    )(proj, proj, proj, conv_w)


def _gdn_kernel(blk_ref, first_ref, seq_ref, q0_ref, k0_ref, v0_ref, m0_ref, q1_ref, k1_ref, v1_ref, m1_ref,
                exa_ref, exb_ref, alog_ref, dtb_ref, s0_ref, tri_ref, msk_ref, o0_ref, o1_ref, sfin_ref,
                st_ref, lg_ref, be_ref, *, n_ctx, tb, bw, nmerge):
    j = pl.program_id(0)
    c = GD_C
    nc = tb // c
    nh = bw // GD_DK
    is_lat = seq_ref[j] >= n_ctx
    q_refs, k_refs, v_refs = (q0_ref, q1_ref), (k0_ref, k1_ref), (v0_ref, v1_ref)
    m_refs, o_refs = (m0_ref, m1_ref), (o0_ref, o1_ref)

    @pl.when(first_ref[j] == 1)
    def _():
        st_ref[...] = jnp.where(is_lat, s0_ref[...], 0.0)

    for d in range(2):
        ab = m_refs[d][:, MISC_W - LANES:MISC_W]
        a_in = _rdot01(ab, exa_ref[d])
        b_in = _rdot01(ab, exb_ref[d])
        lg_ref[d] = -jnp.exp(alog_ref[d]) * _softplus(a_in + dtb_ref[d])
        be_ref[d] = _sigmoid(b_in)

    tri = [tri_ref[d].astype(bf16) for d in range(2)]
    heads = [slice(h * GD_DK, (h + 1) * GD_DK) for h in range(nh)]
    items = [(d, h) for d in range(2) for h in range(nh)]
    strict = [msk_ref[d, 0] for d, _ in items]
    incl = [msk_ref[d, 1] for d, _ in items]
    mdiag = [msk_ref[d, 3] for d, _ in items]
    moffs = [[msk_ref[d, 4 + m] for m in range(nmerge)] for d, _ in items]

    def chunk(ci, carry):
        r0 = _chunk_rows(ci, nc, c)
        be, q, k, v, gcum, eg, qdec, kdec, gend = [], [], [], [], [], [], [], [], []
        for d in range(2):
            rows = pl.ds(r0[d], c)
            lg = lg_ref[d, rows, :]
            be.append(be_ref[d, rows, :])
            q.append(q_refs[d][rows, :])
            k.append(k_refs[d][rows, :])
            v.append(v_refs[d][rows, :])
            g = _ldot01(tri[d], lg)
            gtot = jnp.sum(lg, axis=0, keepdims=True)
            gcum.append(g)
            eg.append(jnp.exp(g))
            qdec.append(q[d] * eg[d])
            kdec.append(k[d] * jnp.exp(gtot - g))
            gend.append(jnp.exp(gtot))
        qk = [_dot_nt(jnp.concatenate([q[d][:, heads[h]], k[d][:, heads[h]]], axis=0), k[d][:, heads[h]])
              for d, h in items]
        dec = []
        for d, h in items:
            gc = gcum[d][:, heads[h]]
            dec.append(jnp.exp(jnp.minimum(gc - gc.T, 0.0)))
        ns = [be[d][:, heads[h]] * a[c:2 * c] * e * m for (d, h), a, e, m in zip(items, qk, dec, strict)]
        ys = _tri_inverse_y(ns, mdiag, moffs)
        rhs = [jnp.concatenate([be[d][:, heads[h]] * eg[d][:, heads[h]] * k[d][:, heads[h]],
                                be[d][:, heads[h]] * v[d][:, heads[h]]], axis=1) for d, h in items]
        wu = [r + _dotp(y, r) for y, r in zip(ys, rhs)]
        ps = [a[0:c] * e * m for a, e, m in zip(qk, dec, incl)]
        ss = [st_ref[d, h] for d, h in items]
        ws = [_dot(jnp.concatenate([x[:, 0:GD_DK], qdec[d][:, heads[h]]], axis=0), s)
              for (d, h), x, s in zip(items, wu, ss)]
        vnew = [x[:, GD_DK:2 * GD_DK] - y[0:c] for x, y in zip(wu, ws)]
        outs = [y[c:2 * c] + _dot(p, vn) for y, p, vn in zip(ws, ps, vnew)]
        snew = [gend[d][:, heads[h]] * s + _dot_tn(kdec[d][:, heads[h]], vn)
                for (d, h), s, vn in zip(items, ss, vnew)]
        for (d, h), o, s in zip(items, outs, snew):
            o_refs[d][pl.ds(r0[d], c), heads[h]] = o
            st_ref[d, h] = s
        return carry

    lax.fori_loop(0, nc, chunk, 0, unroll=True)

    @pl.when(jnp.logical_not(is_lat))
    def _():
        sfin_ref[...] = st_ref[...]


def _gdn_scan(qkv, proj, misc_col, exa, exb, alog, dtb, s0, layer, tables, n_ctx, tb, bw):
    t = proj.shape[0]
    nh = bw // GD_DK
    blk, first, seq = tables
    nb = blk.shape[1]
    c = GD_C
    tri = np.stack([_tri_np(c, False), _tri_np(c, True)]).astype(np.float32)
    inv = _inverse_masks_np(c, c)
    msk = np.stack([np.stack(list(_causal_np(c, c, rev)) + inv) for rev in (False, True)]).astype(np.float32)
    nmerge = len(inv) - 2
    kern = functools.partial(_gdn_kernel, n_ctx=n_ctx, tb=tb, bw=bw, nmerge=nmerge)
    tail = (2, nh, GD_DK, GD_DK)
    dir_specs = []
    for d in range(2):
        dir_specs += [pl.BlockSpec((tb, bw), _dir_row(d, col)) for col in range(3)]
        dir_specs += [pl.BlockSpec((tb, MISC_W), _dir_row(d, misc_col))]
    grid_spec = pltpu.PrefetchScalarGridSpec(
        num_scalar_prefetch=3,
        grid=(nb,),
        in_specs=dir_specs + [_const_spec((2, LANES, bw)), _const_spec((2, LANES, bw)), _const_spec((2, 1, bw)),
                              _const_spec((2, 1, bw)), _state_in_spec(tail, n_ctx, layer),
                              _const_spec(tri.shape), _const_spec(msk.shape)],
        out_specs=[pl.BlockSpec((tb, bw), _dir_row(0, 0)), pl.BlockSpec((tb, bw), _dir_row(1, 0)),
                   _state_out_spec(tail, n_ctx)],
        scratch_shapes=[pltpu.VMEM(tail, f32), pltpu.VMEM((2, tb, bw), f32), pltpu.VMEM((2, tb, bw), f32)],
    )
    return pl.pallas_call(
        kern, grid_spec=grid_spec,
        out_shape=[jax.ShapeDtypeStruct((t, bw), f32), jax.ShapeDtypeStruct((t, bw), f32),
                   jax.ShapeDtypeStruct((n_ctx,) + tail, f32)],
        compiler_params=_cparams(("arbitrary",)),
        name="gdn_scan",
    )(jnp.asarray(blk), jnp.asarray(first), jnp.asarray(seq), qkv, qkv, qkv, proj, qkv, qkv, qkv, proj,
      exa, exb, alog, dtb, s0, jnp.asarray(tri), jnp.asarray(msk))


def _rwkv_prep_kernel(r_ref, k_ref, v_ref, misc_ref, w0_ref, w2_ref, a0_ref, a2_ref, g2_ref, kkp_ref,
                      ka_ref, rk_ref, b64_ref, kk_ref, lw_ref, icl_ref, bonus_ref, og_ref):
    rr = r_ref[...]
    rk = k_ref[...]
    wl = jnp.tanh(misc_ref[:, 0:LANES])
    al = misc_ref[:, LANES:2 * LANES]
    gl = _sigmoid(misc_ref[:, 2 * LANES:3 * LANES])
    ka = ka_ref[...]
    bon = jnp.zeros_like(rr)
    for d in range(2):
        logw = -_softplus(-(w0_ref[d:d + 1, :] + _dot(wl, w2_ref[d]))) - 0.5
        lw_ref[d] = -jnp.exp(logw)
        icl = _sigmoid(a0_ref[d:d + 1, :] + _dot(al, a2_ref[d]))
        icl_ref[d] = icl
        bon = bon + rr * (rk * (1.0 + (icl - 1.0) * ka)) * rk_ref[...]
    og_ref[...] = _dot(gl, g2_ref[...])
    b64 = b64_ref[...]
    kx = rk * kkp_ref[...]
    kk_ref[...] = kx * lax.rsqrt(_rdot01(kx * kx, b64) + NORM_EPS)
    bonus_ref[...] = _rdot01(bon, b64) * v_ref[...]


def _rwkv_prep(proj, rkv_col, misc_col, w0, w2p, a0, a2p, g2p, kkp, ka, rkp, b64, tb, bw):
    t = proj.shape[0]
    full = lambda shape: pl.BlockSpec(shape, lambda r: (0,) * len(shape))
    return pl.pallas_call(
        _rwkv_prep_kernel,
        grid=(t // tb,),
        in_specs=[pl.BlockSpec((tb, bw), lambda r: (r, rkv_col)),
                  pl.BlockSpec((tb, bw), lambda r: (r, rkv_col + 1)),
                  pl.BlockSpec((tb, bw), lambda r: (r, rkv_col + 2)),
                  pl.BlockSpec((tb, MISC_W), lambda r: (r, misc_col)),
                  full((2, bw)), full((2, LANES, bw)), full((2, bw)), full((2, LANES, bw)),
                  full((LANES, bw)), full((1, bw)), full((1, bw)), full((1, bw)), full((bw, bw))],
        out_specs=[pl.BlockSpec((tb, bw), lambda r: (r, 0)),
                   pl.BlockSpec((2, tb, bw), lambda r: (0, r, 0)),
                   pl.BlockSpec((2, tb, bw), lambda r: (0, r, 0)),
                   pl.BlockSpec((tb, bw), lambda r: (r, 0)),
                   pl.BlockSpec((tb, bw), lambda r: (r, 0))],
        out_shape=[jax.ShapeDtypeStruct((t, bw), f32), jax.ShapeDtypeStruct((2, t, bw), f32),
                   jax.ShapeDtypeStruct((2, t, bw), f32), jax.ShapeDtypeStruct((t, bw), f32),
                   jax.ShapeDtypeStruct((t, bw), f32)],
        compiler_params=_cparams(("arbitrary",)),
        name="rwkv_prep",
    )(proj, proj, proj, proj, w0, w2p, a0, a2p, g2p, kkp, ka, rkp, b64)


def _rwkv_kernel(blk_ref, first_ref, seq_ref, r0_ref, k0_ref, v0_ref, kk0_ref, lw0_ref, icl0_ref,
                 r1_ref, k1_ref, v1_ref, kk1_ref, lw1_ref, icl1_ref, ka_ref, s0_ref, tri_ref, msk_ref,
                 o0_ref, o1_ref, sfin_ref, st_ref, *, n_ctx, tb, bw, nmerge):
    j = pl.program_id(0)
    c = RW_C
    nc = tb // c
    n2c = 2 * c
    npair = bw // LANES
    is_lat = seq_ref[j] >= n_ctx
    r_refs, k_refs, v_refs = (r0_ref, r1_ref), (k0_ref, k1_ref), (v0_ref, v1_ref)
    kk_refs, lw_refs, icl_refs, o_refs = (kk0_ref, kk1_ref), (lw0_ref, lw1_ref), (icl0_ref, icl1_ref), (o0_ref, o1_ref)

    @pl.when(first_ref[j] == 1)
    def _():
        st_ref[...] = jnp.where(is_lat, s0_ref[...], 0.0)

    tri = [tri_ref[d].astype(bf16) for d in range(2)]
    pairs = [slice(p * LANES, (p + 1) * LANES) for p in range(npair)]
    items = [(d, p) for d in range(2) for p in range(npair)]
    strict = [msk_ref[d, 0] for d, _ in items]
    incl2 = [jnp.concatenate([msk_ref[d, 1], msk_ref[d, 1]], axis=1) for d in range(2)]
    mdiag = [msk_ref[d, 3] for d, _ in items]
    moffs = [[msk_ref[d, 4 + m] for m in range(nmerge)] for d, _ in items]
    lane = lax.broadcasted_iota(jnp.int32, (c, LANES), 1)
    m_even = jnp.where(lane < RW_DK, 1.0, 0.0)
    m_odd = 1.0 - m_even
    ka = ka_ref[...]

    def stack2(x):
        return jnp.concatenate([x * m_even, x * m_odd], axis=0)

    def chunk(ci, carry):
        r0 = _chunk_rows(ci, nc, c)
        at, kt, bt, rt, rv, kdec, bdec, etot = [], [], [], [], [], [], [], []
        for d in range(2):
            rows = pl.ds(r0[d], c)
            lw = lw_refs[d][rows, :]
            icl = icl_refs[d][rows, :]
            kk = kk_refs[d][rows, :]
            cum = _ldot01(tri[d], lw)
            ctot = jnp.sum(lw, axis=0, keepdims=True)
            kdir = k_refs[d][rows, :] * (1.0 + (icl - 1.0) * ka)
            bvec = kk * icl
            einv = jnp.exp(-cum)
            edec = jnp.exp(ctot - cum)
            at.append((-kk) * jnp.exp(cum - lw))
            kt.append(kdir * einv)
            bt.append(bvec * einv)
            rt.append(r_refs[d][rows, :] * jnp.exp(cum))
            rv.append(v_refs[d][rows, :])
            kdec.append(kdir * edec)
            bdec.append(bvec * edec)
            etot.append(jnp.exp(ctot))
        a2 = [stack2(at[d][:, pairs[p]]) for d, p in items]
        r2 = [stack2(rt[d][:, pairs[p]]) for d, p in items]
        v2 = [stack2(rv[d][:, pairs[p]]) for d, p in items]
        kb2 = [jnp.concatenate([stack2(kt[d][:, pairs[p]]), stack2(bt[d][:, pairs[p]])], axis=0) for d, p in items]
        kbd2 = [jnp.concatenate([stack2(kdec[d][:, pairs[p]]), stack2(bdec[d][:, pairs[p]])], axis=0)
                for d, p in items]
        big = [_dot_nt(jnp.concatenate([a, r], axis=0), kb) for a, r, kb in zip(a2, r2, kb2)]
        aak = [x[0:n2c, 0:n2c] * m for x, m in zip(big, strict)]
        ns = [-(x[0:n2c, n2c:2 * n2c] * m) for x, m in zip(big, strict)]
        arkb = [x[n2c:2 * n2c] * incl2[d] for x, (d, _) in zip(big, items)]
        ys = _tri_inverse_y(ns, mdiag, moffs)
        rhs = [jnp.concatenate([a, _dot(m, v)], axis=1) for a, m, v in zip(a2, aak, v2)]
        tat = [x + _dotp(y, x) for y, x in zip(ys, rhs)]
        sts = [st_ref[d, p] for d, p in items]
        ur = [_dot_nt(jnp.concatenate([x[:, 0:LANES], r], axis=0), st) for x, r, st in zip(tat, r2, sts)]
        u2 = [y[0:n2c] + x[:, LANES:2 * LANES] for y, x in zip(ur, tat)]
        vu = [jnp.concatenate([v, u], axis=0) for v, u in zip(v2, u2)]
        y2 = [y[n2c:2 * n2c] + _dot(m, x) for y, m, x in zip(ur, arkb, vu)]
        snew = [st * etot[d][:, pairs[p]] + _dot_tn(x, kbd) for (d, p), st, x, kbd in zip(items, sts, vu, kbd2)]
        for (d, p), y, s in zip(items, y2, snew):
            o_refs[d][pl.ds(r0[d], c), pairs[p]] = y[0:c] + y[c:n2c]
            st_ref[d, p] = s
        return carry

    lax.fori_loop(0, nc, chunk, 0, unroll=True)

    @pl.when(jnp.logical_not(is_lat))
    def _():
        for d, p in items:
            st = st_ref[d, p]
            sfin_ref[d, 2 * p] = st[0:RW_DK, 0:RW_DK]
            sfin_ref[d, 2 * p + 1] = pltpu.roll(st, RW_DK, axis=1)[RW_DK:LANES, 0:RW_DK]


def _rwkv_scan(proj, rkv_col, kk, lw, icl, ka, s0p, layer, tables, n_ctx, tb, bw):
    t = proj.shape[0]
    npair = bw // LANES
    blk, first, seq = tables
    nb = blk.shape[1]
    c = RW_C
    n = 2 * c
    tri = np.stack([_tri_np(c, False), _tri_np(c, True)]).astype(np.float32)
    inv = _inverse_masks_np(n, c)
    msk = np.stack([np.stack(list(_causal_np(n, c, rev)) + inv) for rev in (False, True)]).astype(np.float32)
    nmerge = len(inv) - 2
    kern = functools.partial(_rwkv_kernel, n_ctx=n_ctx, tb=tb, bw=bw, nmerge=nmerge)
    tail = (2, npair, LANES, LANES)
    head_tail = (2, 2 * npair, RW_DK, RW_DK)
    dir_specs, dir_args = [], []
    for d in range(2):
        dir_specs += [pl.BlockSpec((tb, bw), _dir_row(d, rkv_col + i)) for i in range(3)]
        dir_specs += [pl.BlockSpec((tb, bw), _dir_row(d, 0)),
                      pl.BlockSpec((None, tb, bw), _dir_slab(d)), pl.BlockSpec((None, tb, bw), _dir_slab(d))]
        dir_args += [proj, proj, proj, kk, lw, icl]
    grid_spec = pltpu.PrefetchScalarGridSpec(
        num_scalar_prefetch=3,
        grid=(nb,),
        in_specs=dir_specs + [_const_spec((1, bw)), _state_in_spec(tail, n_ctx, layer),
                              _const_spec(tri.shape), _const_spec(msk.shape)],
        out_specs=[pl.BlockSpec((tb, bw), _dir_row(0, 0)), pl.BlockSpec((tb, bw), _dir_row(1, 0)),
                   _state_out_spec(head_tail, n_ctx)],
        scratch_shapes=[pltpu.VMEM(tail, f32)],
    )
    return pl.pallas_call(
        kern, grid_spec=grid_spec,
        out_shape=[jax.ShapeDtypeStruct((t, bw), f32), jax.ShapeDtypeStruct((t, bw), f32),
                   jax.ShapeDtypeStruct((n_ctx,) + head_tail, f32)],
        compiler_params=_cparams(("arbitrary",)),
        name="rwkv_scan",
    )(jnp.asarray(blk), jnp.asarray(first), jnp.asarray(seq), *dir_args, ka, s0p,
      jnp.asarray(tri), jnp.asarray(msk))


def _merge_kernel(x_ref, mod_ref, ohg0_ref, ohg1_ref, ogd0_ref, ogd1_ref, orw0_ref, orw1_ref, hgg_ref, gdz_ref,
                  bonus_ref, og_ref, g0_ref, g1_ref, g2_ref, hgn_ref, gdn_ref, lng_ref, lnb_ref, wb_ref, wo_ref,
                  o_ref, *, bw):
    def head_mean(x, width):
        outs = []
        for g in range(bw // LANES):
            xg = x[:, g * LANES:(g + 1) * LANES]
            if width == LANES:
                outs.append(jnp.broadcast_to(jnp.mean(xg, axis=-1, keepdims=True), xg.shape))
            else:
                lane = lax.broadcasted_iota(jnp.int32, xg.shape, 1)
                lo = lane < width
                s_lo = jnp.sum(jnp.where(lo, xg, 0.0), axis=-1, keepdims=True)
                s_hi = jnp.sum(jnp.where(lo, 0.0, xg), axis=-1, keepdims=True)
                outs.append(jnp.where(lo, s_lo, s_hi) * (1.0 / width))
        return jnp.concatenate(outs, axis=-1)

    ohg = ohg0_ref[...] + ohg1_ref[...]
    ohg = ohg * lax.rsqrt(head_mean(ohg * ohg, HG_DK) + NORM_EPS) * hgn_ref[...]
    ohg = ohg * _silu(hgg_ref[...])
    ogd = ogd0_ref[...] + ogd1_ref[...]
    ogd = ogd * lax.rsqrt(head_mean(ogd * ogd, GD_DK) + NORM_EPS) * gdn_ref[...]
    ogd = ogd * _silu(gdz_ref[...])
    orw = orw0_ref[...] + orw1_ref[...]
    mu = head_mean(orw, RW_DK)
    cen = orw - mu
    var = head_mean(cen * cen, RW_DK)
    orw = cen * lax.rsqrt(var + RW_GN_EPS) * lng_ref[...] + lnb_ref[...]
    orw = (orw + bonus_ref[...]) * og_ref[...]
    merged = (_sigmoid(g0_ref[...]) * _dot(ohg, wb_ref[0])
              + _sigmoid(g1_ref[...]) * _dot(ogd, wb_ref[1])
              + _sigmoid(g2_ref[...]) * _dot(orw, wb_ref[2]))
    o_ref[...] = x_ref[...] + mod_ref[5:6, :] * _dot(merged, wo_ref[...])


def _merge(x, mod, proj, branch_outs, bonus, og, hgn, gdn, lng, lnb, wb, wo, layer, cond_of_tile, tm, bw,
           hgg_col, gdz_col, gate_col):
    t, d = x.shape
    row = lambda col: (lambda i: (i, col))
    act = pl.BlockSpec((tm, bw), row(0))
    vec = pl.BlockSpec((1, bw), lambda i: (0, 0))
    once = pl.Buffered(1)
    return pl.pallas_call(
        functools.partial(_merge_kernel, bw=bw),
        grid=(t // tm,),
        in_specs=[pl.BlockSpec((tm, d), row(0)),
                  pl.BlockSpec((None, None, N_MOD, d), lambda i: (layer, cond_of_tile(i), 0, 0)),
                  act, act, act, act, act, act,
                  pl.BlockSpec((tm, bw), row(hgg_col)),
                  pl.BlockSpec((tm, bw), row(gdz_col)),
                  act, act,
                  pl.BlockSpec((tm, d), row(gate_col)),
                  pl.BlockSpec((tm, d), row(gate_col + 1)),
                  pl.BlockSpec((tm, d), row(gate_col + 2)),
                  vec, vec, vec, vec,
                  pl.BlockSpec((None, 3, bw, d), lambda i: (layer, 0, 0, 0), pipeline_mode=once),
                  pl.BlockSpec((None, d, d), lambda i: (layer, 0, 0), pipeline_mode=once)],
        out_specs=pl.BlockSpec((tm, d), row(0)),
        out_shape=jax.ShapeDtypeStruct((t, d), f32),
        compiler_params=_cparams(("arbitrary",)),
        name="merge",
    )(x, mod, *branch_outs, proj, proj, bonus, og, proj, proj, proj, hgn, gdn, lng, lnb, wb, wo)


def _final_norm_kernel(x_ref, g_ref, o_ref):
    x = x_ref[...]
    ms = jnp.mean(x * x, axis=-1, keepdims=True)
    o_ref[...] = x * lax.rsqrt(ms + NORM_EPS) * g_ref[...]


def _final_norm(x, g, tm, row0, rows):
    d = x.shape[1]
    first = row0 // tm
    return pl.pallas_call(
        _final_norm_kernel,
        grid=(rows // tm,),
        in_specs=[pl.BlockSpec((tm, d), lambda i: (first + i, 0)), pl.BlockSpec((1, d), lambda i: (0, 0))],
        out_specs=pl.BlockSpec((tm, d), lambda i: (i, 0)),
        out_shape=jax.ShapeDtypeStruct((rows, d), f32),
        compiler_params=_cparams(("arbitrary",)),
        name="final_norm",
    )(x, g.reshape(1, d))


def _pad_rows(w, rows):
    return jnp.concatenate([w, jnp.zeros((rows - w.shape[0],) + w.shape[1:], w.dtype)], axis=0)


def kernel(x_prompt, x_sample, state_hgrn, state_gdn, state_rwkv, c, c_ctx, mod_w, mod_b, norm_g, ffn_up,
           ffn_down, mix_in, hgrn_lb, hgrn_norm, gdn_conv, gdn_a_log, gdn_dt_bias, gdn_norm, rwkv_w0, rwkv_w2,
           rwkv_a0, rwkv_a2, rwkv_g2, rwkv_kk, rwkv_ka, rwkv_rk, rwkv_ln_g, rwkv_ln_b, mix_branch, mix_out,
           final_norm):
    bc, lc, d = x_prompt.shape
    bs, ls, _ = x_sample.shape
    depth = mod_w.shape[0]
    bw = d // 4
    tb = lc
    assert tb % GD_C == 0 and ls % tb == 0 and bw % LANES == 0 and tb % GRID_W == 0
    assert bs + 1 <= N_COND_PAD
    n_ctx = bc
    bps = ls // tb
    tc = bc * lc
    t = tc + bs * ls
    gd_h, rw_h = bw // GD_DK, bw // RW_DK
    lw_w, la_w, lg_w = rwkv_w2.shape[2], rwkv_a2.shape[2], rwkv_g2.shape[1]
    assert max(2 * lw_w, 2 * la_w, lg_w, 4 * gd_h) <= LANES

    def token_tile(cap):
        rows = tb
        while rows * 2 <= cap and tc % (rows * 2) == 0 and ls % (rows * 2) == 0:
            rows *= 2
        return rows

    tm, tp = token_tile(TM_FFN), token_tile(TM_PROJ)
    cond_of_tile = lambda tile_rows: (lambda i: jnp.where(i * tile_rows < tc, 0, 1 + (i * tile_rows - tc) // ls))

    x = (x_prompt.reshape(tc, d), x_sample.reshape(bs * ls, d))
    cond = jnp.concatenate([c_ctx[None, :], c, jnp.zeros((N_COND_PAD - 1 - bs, d), f32)], axis=0)
    mod_all = _modulation(cond, mod_w, mod_b).reshape(depth, N_COND_PAD, N_MOD, d)

    lb_p = jax.nn.softmax(hgrn_lb.astype(f32), axis=0)
    lb_all = jnp.cumsum(lb_p, axis=0) - lb_p[0:1]

    o_gda = 9 * bw
    o_rw = 9 * bw + 4 * gd_h
    o_lo = o_rw + 3 * bw
    o_mg = o_lo + 2 * lw_w + 2 * la_w + lg_w
    base = 24 * bw
    pieces = ((0, 9 * bw, 0), (o_rw, 3 * bw, 9 * bw), (o_mg, 3 * d, 12 * bw),
              (o_lo, 2 * lw_w, base), (o_lo + 2 * lw_w, 2 * la_w, base + LANES),
              (o_lo + 2 * lw_w + 2 * la_w, lg_w, base + 2 * LANES), (o_gda, 4 * gd_h, base + 3 * LANES))
    w_in = _relayout_mix_in(mix_in, pieces, base + MISC_W)
    col_gdq, col_gdz, col_rw, col_gate = 5, 8, 9, 3
    col_misc = (24 * bw) // MISC_W
    assert (24 * bw) % MISC_W == 0

    up16 = ffn_up.astype(bf16)
    down16 = ffn_down.astype(bf16)
    wb16 = mix_branch.astype(bf16)
    wo16 = mix_out.astype(bf16)

    exa = np.zeros((2, LANES, bw), np.float32)
    exb = np.zeros((2, LANES, bw), np.float32)
    for dd in range(2):
        for h in range(gd_h):
            exa[dd, dd * gd_h + h, h * GD_DK:(h + 1) * GD_DK] = 1.0
            exb[dd, 2 * gd_h + dd * gd_h + h, h * GD_DK:(h + 1) * GD_DK] = 1.0
    exa, exb = jnp.asarray(exa), jnp.asarray(exb)
    b64 = jnp.asarray(np.kron(np.eye(rw_h), np.ones((RW_DK, RW_DK))).astype(np.float32))

    tables = _block_tables_np(n_ctx, bs, bps)

    s0_hg = jnp.swapaxes(state_hgrn, -1, -2)
    s0_gd = state_gdn
    eye2 = jnp.eye(2, dtype=f32)
    s0_rw = state_rwkv.reshape(bs, depth, 2, rw_h // 2, 2, RW_DK, RW_DK)
    s0_rw = jnp.einsum('bldpeij,ef->bldpeifj', s0_rw, eye2).reshape(bs, depth, 2, rw_h // 2, LANES, LANES)

    new_hg, new_gd, new_rw = [], [], []
    for l in range(depth):
        mod = mod_all
        x = _ffn(x, mod, norm_g, up16, down16, l, 0, cond_of_tile(tm), tm)
        proj = _mixin(x, mod, norm_g, w_in, l, cond_of_tile(tp), tp)

        ohg0, ohg1, s_hg = _hgrn_scan(proj, lb_all[l].reshape(2, 1, bw), s0_hg, l, tables, n_ctx, tb, bw)

        conv_w = gdn_conv[l].reshape(CONV_K * CONV_K, 3 * bw)
        qkv = _gdn_prep(proj, conv_w, n_ctx, bps, tb, bw, col_gdq)
        alog = jnp.repeat(gdn_a_log[l], GD_DK, axis=-1).reshape(2, 1, bw)
        dtb = jnp.repeat(gdn_dt_bias[l], GD_DK, axis=-1).reshape(2, 1, bw)

        w2p = jnp.stack([_pad_rows(jnp.concatenate([jnp.zeros((dd * lw_w, bw), f32), rwkv_w2[l, dd]], 0), LANES)
                         for dd in range(2)])
        a2p = jnp.stack([_pad_rows(jnp.concatenate([jnp.zeros((dd * la_w, bw), f32), rwkv_a2[l, dd]], 0), LANES)
                         for dd in range(2)])
        g2p = _pad_rows(rwkv_g2[l], LANES)
        ka = rwkv_ka[l].reshape(1, bw)
        kk, lw, icl, bonus, og = _rwkv_prep(proj, col_rw, col_misc, rwkv_w0[l], w2p, rwkv_a0[l], a2p, g2p,
                                            rwkv_kk[l].reshape(1, bw), ka, rwkv_rk[l].reshape(1, bw), b64,
                                            tb, bw)
        ogd0, ogd1, s_gd = _gdn_scan(qkv, proj, col_misc, exa, exb, alog, dtb, s0_gd, l, tables, n_ctx, tb, bw)
        orw0, orw1, s_rw = _rwkv_scan(proj, col_rw, kk, lw, icl, ka, s0_rw, l, tables, n_ctx, tb, bw)

        x = _merge(x, mod, proj, (ohg0, ohg1, ogd0, ogd1, orw0, orw1), bonus, og, hgrn_norm[l].reshape(1, bw),
                   jnp.tile(gdn_norm[l], gd_h).reshape(1, bw), rwkv_ln_g[l].reshape(1, bw),
                   rwkv_ln_b[l].reshape(1, bw), wb16, wo16, l, cond_of_tile(tb), tb, bw,
                   4, col_gdz, col_gate)
        x = _ffn(x, mod, norm_g, up16, down16, l, 2, cond_of_tile(tm), tm)

        new_hg.append(s_hg)
        new_gd.append(s_gd)
        new_rw.append(s_rw)

    y_ctx = _final_norm(x, final_norm, tp, 0, tc)
    y_lat = _final_norm(x, final_norm, tp, tc, bs * ls)
    return (y_ctx.reshape(bc, lc, d), y_lat.reshape(bs, ls, d),
            jnp.stack(new_hg, axis=1), jnp.stack(new_gd, axis=1), jnp.stack(new_rw, axis=1))
```

```python
import functools

import numpy as np
import jax
import jax.numpy as jnp
from jax import lax
from jax.experimental import pallas as pl
from jax.experimental.pallas import tpu as pltpu

f32 = jnp.float32
bf16 = jnp.bfloat16

LANES = 128
NORM_ROWS = 16
MXU_W = 256
VMEM_LIMIT = 60 * 1024 * 1024

HG_DK = 128
GD_DK = 128
RW_DK = 64
N_MOD = 9
GRID_W = 64
CONV_K = 3
NORM_EPS = 1e-6
RW_GN_EPS = 64e-5
GATE_FLOOR = 1e-30
N_COND_PAD = 16

HG_C = 64
GD_C = 128
RW_C = 64
SUB = 8
INV_PASSES = 1
MISC_W = 4 * LANES
TM_FFN = 512
TM_PROJ = 1024


def _sigmoid(x):
    return 1.0 / (1.0 + jnp.exp(-x))


def _silu(x):
    return x * _sigmoid(x)


def _softplus(x):
    return jnp.maximum(x, 0.0) + jnp.log(1.0 + jnp.exp(-jnp.abs(x)))


def _dot(a, b):
    return jnp.dot(a.astype(bf16), b.astype(bf16), preferred_element_type=f32)


def _dot_nt(a, b):
    return lax.dot_general(a.astype(bf16), b.astype(bf16), (((1,), (1,)), ((), ())),
                           preferred_element_type=f32)


def _dot_tn(a, b):
    return lax.dot_general(a.astype(bf16), b.astype(bf16), (((0,), (0,)), ((), ())),
                           preferred_element_type=f32)


def _split3(x):
    hi = x.astype(bf16)
    r = x - hi.astype(f32)
    mid = r.astype(bf16)
    lo = (r - mid.astype(f32)).astype(bf16)
    return hi, mid, lo


def _ldot01(m, x):
    hi, mid, lo = _split3(x)
    m = m.astype(bf16)
    d = functools.partial(jnp.dot, preferred_element_type=f32)
    return d(m, hi) + d(m, mid) + d(m, lo)


def _rdot01(x, m):
    hi, mid, lo = _split3(x)
    m = m.astype(bf16)
    d = functools.partial(jnp.dot, preferred_element_type=f32)
    return d(hi, m) + d(mid, m) + d(lo, m)


def _dotp(a, b):
    if INV_PASSES == 1:
        return _dot(a, b)
    ah = a.astype(bf16)
    al = (a - ah.astype(f32)).astype(bf16)
    bh = b.astype(bf16)
    bl = (b - bh.astype(f32)).astype(bf16)
    d = functools.partial(jnp.dot, preferred_element_type=f32)
    return d(ah, bh) + d(ah, bl) + d(al, bh)


def _tri_inverse_y(ns, mdiag, moffs):
    nd = [n * m for n, m in zip(ns, mdiag)]
    n2 = [_dotp(a, a) for a in nd]
    n4 = [_dotp(a, a) for a in n2]
    y = [b - a - _dotp(a, b) for a, b in zip(nd, n2)]
    y = [a + b + _dotp(a, b) for a, b in zip(y, n4)]
    for lvl in range(len(moffs[0])):
        xl = [n * m[lvl] for n, m in zip(ns, moffs)]
        xl = [l + _dotp(a, l) for a, l in zip(y, xl)]
        y = [a - l - _dotp(l, a) for a, l in zip(y, xl)]
    return y


def _normmod_rows(read_rows, h_ref, g, scale, shift):
    gain = g * (1.0 + scale)

    def body(r, carry):
        r0 = pl.multiple_of(r * NORM_ROWS, NORM_ROWS)
        x = read_rows(r0)
        ms = jnp.mean(x * x, axis=-1, keepdims=True)
        h_ref[pl.ds(r0, NORM_ROWS), :] = ((x * lax.rsqrt(ms + NORM_EPS)) * gain + shift).astype(bf16)
        return carry

    lax.fori_loop(0, h_ref.shape[0] // NORM_ROWS, body, 0, unroll=8)


def _tri_np(c, rev):
    i = np.arange(c)
    return (i[None, :] >= i[:, None]) if rev else (i[None, :] <= i[:, None])


def _inverse_masks_np(n, c):
    i = np.arange(n)
    same = (i[:, None] // c) == (i[None, :] // c)
    eye = np.eye(n, dtype=bool)
    mdiag = (i[:, None] // SUB) == (i[None, :] // SUB)
    out = [eye, mdiag]
    m = SUB
    while m < c:
        out.append(same & ((i[:, None] // (2 * m)) == (i[None, :] // (2 * m)))
                   & ((i[:, None] // m) != (i[None, :] // m)))
        m *= 2
    return out


def _causal_np(n, c, rev):
    i = np.arange(n)
    same = (i[:, None] // c) == (i[None, :] // c)
    il, jl = (i % c)[:, None], (i % c)[None, :]
    strict = same & ((jl > il) if rev else (jl < il))
    incl = same & ((jl >= il) if rev else (jl <= il))
    return strict, incl


def _hgrn_tables_np(c):
    levels = []
    m = SUB
    while m < c:
        levels.append(m)
        m *= 2
    stacks, pmasks = [], []
    t = np.arange(c)
    for rev in (False, True):
        mats = [_tri_np(c, rev)]
        pms = []
        for m in levels:
            pair = t // (2 * m)
            half = (t // m) % 2
            tau = t[None, :]
            if not rev:
                bd = (pair * 2 * m + m - 1)[:, None]
                qside = (half == 1)[:, None]
                mat = np.where(qside, (tau > bd) & (tau <= t[:, None]), (tau > t[:, None]) & (tau <= bd))
                pm = (pair[:, None] == pair[None, :]) & (half == 1)[:, None] & (half == 0)[None, :]
            else:
                bd = (pair * 2 * m + m)[:, None]
                qside = (half == 0)[:, None]
                mat = np.where(qside, (tau >= t[:, None]) & (tau < bd), (tau >= bd) & (tau < t[:, None]))
                pm = (pair[:, None] == pair[None, :]) & (half == 0)[:, None] & (half == 1)[None, :]
            mats.append(mat)
            pms.append(pm)
        stacks.append(np.concatenate(mats, axis=0))
        pmasks.append(np.stack(pms, axis=0))
    return (np.stack(stacks).astype(np.float32), np.stack(pmasks).astype(np.float32), len(levels))


def _block_tables_np(n_ctx, n_lat_seq, bps):
    nb = n_ctx + n_lat_seq * bps
    blk = np.zeros((2, nb), np.int32)
    first = np.zeros((nb,), np.int32)
    seq = np.zeros((nb,), np.int32)
    j = 0
    for s in range(n_ctx):
        blk[0, j] = blk[1, j] = s
        first[j], seq[j] = 1, s
        j += 1
    for s in range(n_lat_seq):
        for p in range(bps):
            blk[0, j] = n_ctx + s * bps + p
            blk[1, j] = n_ctx + s * bps + bps - 1 - p
            first[j], seq[j] = int(p == 0), n_ctx + s
            j += 1
    return blk, first, seq


def _pick(n, cap):
    best = LANES
    t = LANES
    while t <= min(n, cap):
        if n % t == 0:
            best = t
        t += LANES
    return best


def _pick_mxu(n, cap):
    best = 0
    t = MXU_W
    while t <= min(n, cap):
        if n % t == 0:
            best = t
        t += MXU_W
    return best or _pick(n, cap)


def _cparams(sem):
    return pltpu.CompilerParams(dimension_semantics=sem, vmem_limit_bytes=VMEM_LIMIT)


def _mod_kernel(cond_ref, w_ref, b_ref, o_ref):
    x = cond_ref[...]
    o_ref[...] = _dot(_silu(x), w_ref[...]) + b_ref[...]


def _modulation(cond, mod_w, mod_b):
    depth, d, nd = mod_w.shape
    tn = _pick(nd, 1024)
    return pl.pallas_call(
        _mod_kernel,
        grid=(depth, nd // tn),
        in_specs=[pl.BlockSpec((N_COND_PAD, d), lambda l, j: (0, 0)),
                  pl.BlockSpec((None, d, tn), lambda l, j: (l, 0, j)),
                  pl.BlockSpec((None, 1, tn), lambda l, j: (l, 0, j))],
        out_specs=pl.BlockSpec((None, N_COND_PAD, tn), lambda l, j: (l, 0, j)),
        out_shape=jax.ShapeDtypeStruct((depth, N_COND_PAD, nd), f32),
        compiler_params=_cparams(("arbitrary", "arbitrary")),
        name="modulation",
    )(cond, mod_w, mod_b.reshape(depth, 1, nd))


def _relayout_kernel(w_ref, o_ref, *, pieces):
    nw = o_ref.shape[1]
    o_ref[:, nw - MISC_W:nw] = jnp.zeros((o_ref.shape[0], MISC_W), bf16)
    for src, n, dst in pieces:
        o_ref[:, dst:dst + n] = w_ref[:, src:src + n].astype(bf16)


def _relayout_mix_in(w, pieces, nw):
    depth, d, win = w.shape
    tr = LANES
    return pl.pallas_call(
        functools.partial(_relayout_kernel, pieces=pieces),
        grid=(depth, d // tr),
        in_specs=[pl.BlockSpec((None, tr, win), lambda l, r: (l, r, 0))],
        out_specs=pl.BlockSpec((None, tr, nw), lambda l, r: (l, r, 0)),
        out_shape=jax.ShapeDtypeStruct((depth, d, nw), bf16),
        compiler_params=_cparams(("arbitrary", "arbitrary")),
        name="relayout_mix_in",
    )(w)


def _ffn_kernel(*refs, slot, nf, split):
    if split is None:
        x_ref, mod_ref, g_ref, wa_ref, wb_ref, wd_ref, o_ref, h_ref = refs
        read_x = lambda rows: x_ref[rows, :]
    else:
        xa_ref, xb_ref, mod_ref, g_ref, wa_ref, wb_ref, wd_ref, o_ref, h_ref = refs
        first = pl.program_id(0) < split
        read_x = lambda rows: jnp.where(first, xa_ref[rows, :], xb_ref[rows, :])
    f = pl.program_id(1)

    @pl.when(f == 0)
    def _():
        _normmod_rows(lambda r0: read_x(pl.ds(r0, NORM_ROWS)), h_ref, g_ref[slot:slot + 1, :],
                      mod_ref[3 * slot + 1:3 * slot + 2, :], mod_ref[3 * slot:3 * slot + 1, :])
        o_ref[...] = jnp.zeros_like(o_ref)

    h = h_ref[...]
    a = jnp.dot(h, wa_ref[...], preferred_element_type=f32)
    b = jnp.dot(h, wb_ref[...], preferred_element_type=f32)
    act = (_silu(a) * b).astype(bf16)
    o_ref[...] += jnp.dot(act, wd_ref[...], preferred_element_type=f32)

    @pl.when(f == nf - 1)
    def _():
        o_ref[...] = read_x(slice(None)) + 0.5 * mod_ref[3 * slot + 2:3 * slot + 3, :] * o_ref[...]


def _ffn(xs, mod, norm_g, w_up, w_down, layer, slot, cond_of_tile, tm):
    if isinstance(xs, tuple):
        xa, xb = xs
        d = xa.shape[1]
        split = xa.shape[0] // tm
        assert xa.shape[0] % tm == 0 and xb.shape[0] % tm == 0
        t = xa.shape[0] + xb.shape[0]
        x_specs = [pl.BlockSpec((tm, d), lambda i, f: (jnp.minimum(i, split - 1), 0)),
                   pl.BlockSpec((tm, d), lambda i, f: (jnp.maximum(i - split, 0), 0))]
        x_args = [xa, xb]
    else:
        t, d = xs.shape
        split = None
        x_specs = [pl.BlockSpec((tm, d), lambda i, f: (i, 0))]
        x_args = [xs]
    dff = w_down.shape[2]
    tf = _pick(dff, 512)
    nf = dff // tf
    which = slot // 2
    return pl.pallas_call(
        functools.partial(_ffn_kernel, slot=slot, nf=nf, split=split),
        grid=(t // tm, nf),
        in_specs=x_specs + [
                  pl.BlockSpec((None, None, N_MOD, d), lambda i, f: (layer, cond_of_tile(i), 0, 0)),
                  pl.BlockSpec((None, 3, d), lambda i, f: (layer, 0, 0)),
                  pl.BlockSpec((None, None, d, tf), lambda i, f: (layer, which, 0, f)),
                  pl.BlockSpec((None, None, d, tf), lambda i, f: (layer, which, 0, f + nf)),
                  pl.BlockSpec((None, None, tf, d), lambda i, f: (layer, which, f, 0))],
        out_specs=pl.BlockSpec((tm, d), lambda i, f: (i, 0)),
        out_shape=jax.ShapeDtypeStruct((t, d), f32),
        scratch_shapes=[pltpu.VMEM((tm, d), bf16)],
        compiler_params=_cparams(("arbitrary", "arbitrary")),
        name="ffn%d" % slot,
    )(*x_args, mod, norm_g, w_up, w_up, w_down)


def _mixin_kernel(x_ref, mod_ref, g_ref, w_ref, o_ref, h_ref):
    @pl.when(pl.program_id(1) == 0)
    def _():
        _normmod_rows(lambda r0: x_ref[pl.ds(r0, NORM_ROWS), :], h_ref, g_ref[1:2, :], mod_ref[4:5, :],
                      mod_ref[3:4, :])

    o_ref[...] = jnp.dot(h_ref[...], w_ref[...], preferred_element_type=f32)


def _mixin(x, mod, norm_g, w, layer, cond_of_tile, tm):
    t, d = x.shape
    nw = w.shape[2]
    tn = _pick_mxu(nw, 1280)
    return pl.pallas_call(
        _mixin_kernel,
        grid=(t // tm, nw // tn),
        in_specs=[pl.BlockSpec((tm, d), lambda i, j: (i, 0)),
                  pl.BlockSpec((None, None, N_MOD, d), lambda i, j: (layer, cond_of_tile(i), 0, 0)),
                  pl.BlockSpec((None, 3, d), lambda i, j: (layer, 0, 0)),
                  pl.BlockSpec((None, d, tn), lambda i, j: (layer, 0, j))],
        out_specs=pl.BlockSpec((tm, tn), lambda i, j: (i, j)),
        out_shape=jax.ShapeDtypeStruct((t, nw), f32),
        scratch_shapes=[pltpu.VMEM((tm, d), bf16)],
        compiler_params=_cparams(("arbitrary", "arbitrary")),
        name="mix_in",
    )(x, mod, norm_g, w)


def _dir_row(d, col):
    return lambda j, b, f, s: (b[d, j], col)


def _dir_slab(d):
    return lambda j, b, f, s: (d, b[d, j], 0)


def _const_spec(shape):
    return pl.BlockSpec(shape, lambda j, b, f, s: (0,) * len(shape))


def _state_in_spec(tail, n_ctx, layer):
    return pl.BlockSpec((None, None) + tail,
                        lambda j, b, f, s: (jnp.maximum(s[j] - n_ctx, 0), layer) + (0,) * len(tail))


def _state_out_spec(tail, n_ctx):
    return pl.BlockSpec((None,) + tail, lambda j, b, f, s: (jnp.minimum(s[j], n_ctx - 1),) + (0,) * len(tail))


def _chunk_rows(ci, nc, c):
    return (pl.multiple_of(ci * c, c), pl.multiple_of((nc - 1 - ci) * c, c))


def _hgrn_kernel(blk_ref, first_ref, seq_ref, q0_ref, v0_ref, f0_ref, q1_ref, v1_ref, f1_ref, lb_ref, s0_ref,
                 mst_ref, pm_ref, o0_ref, o1_ref, sfin_ref, st_ref, qs_ref, ks_ref, lf_ref,
                 *, n_ctx, tb, bw, nlev):
    j = pl.program_id(0)
    c = HG_C
    nc = tb // c
    nh = bw // HG_DK
    is_lat = seq_ref[j] >= n_ctx
    q_refs, v_refs, f_refs, o_refs = (q0_ref, q1_ref), (v0_ref, v1_ref), (f0_ref, f1_ref), (o0_ref, o1_ref)

    @pl.when(first_ref[j] == 1)
    def _():
        st_ref[...] = jnp.where(is_lat, s0_ref[...], 0.0)

    for d in range(2):
        lb = lb_ref[d]
        fl = f_refs[d][...]
        sig = _sigmoid(fl)
        lf_ref[d] = jnp.log(jnp.maximum(lb + (1.0 - lb) * sig, GATE_FLOOR))
        ks_ref[d] = (1.0 - lb) * (1.0 - sig)
        qs_ref[d] = _silu(q_refs[d][...]) * HG_DK ** -0.5

    mst = [mst_ref[d].astype(bf16) for d in range(2)]
    tl = lax.broadcasted_iota(jnp.int32, (c, bw), 0) % SUB
    heads = [slice(h * HG_DK, (h + 1) * HG_DK) for h in range(nh)]
    items = [(d, h) for d in range(2) for h in range(nh)]

    def chunk(ci, carry):
        r0 = _chunk_rows(ci, nc, c)
        qe, kd, etot, qm, km, vs, accs = [], [], [], [], [], [], []
        for d in range(2):
            rows = pl.ds(r0[d], c)
            lf = lf_ref[d, rows, :]
            q = qs_ref[d, rows, :]
            k = ks_ref[d, rows, :]
            v = v_refs[d][rows, :]
            xs = _ldot01(mst[d], lf)
            b = xs[0:c]
            btot = jnp.sum(lf, axis=0, keepdims=True)
            qe.append(q * jnp.exp(b))
            kd.append(k * jnp.exp(btot - b))
            etot.append(jnp.exp(btot))
            es = [jnp.exp(xs[(lv + 1) * c:(lv + 2) * c]) for lv in range(nlev)]
            qm.append([q * e for e in es])
            km.append([k * e for e in es])
            vs.append(v)
            b3 = b.reshape(c // SUB, SUB, bw)
            k3 = k.reshape(c // SUB, SUB, bw)
            v3 = v.reshape(c // SUB, SUB, bw)
            acc = [jnp.zeros((c, HG_DK), f32) for _ in range(nh)]
            for sl in range(SUB):
                bs = jnp.broadcast_to(b3[:, sl:sl + 1, :], b3.shape).reshape(c, bw)
                kk = jnp.broadcast_to(k3[:, sl:sl + 1, :], k3.shape).reshape(c, bw)
                vv = jnp.broadcast_to(v3[:, sl:sl + 1, :], v3.shape).reshape(c, bw)
                valid = (tl >= sl) if d == 0 else (tl <= sl)
                p = jnp.where(valid, q * kk * jnp.exp(b - bs), 0.0)
                for h, hs in enumerate(heads):
                    acc[h] = acc[h] + jnp.sum(p[:, hs], axis=-1, keepdims=True) * vv[:, hs]
            accs.append(acc)
        sts = [st_ref[d, h] for d, h in items]
        inter = [_dot_nt(qe[d][:, heads[h]], st) for (d, h), st in zip(items, sts)]
        sc = [jnp.zeros((c, c), f32) for _ in items]
        for lv in range(nlev):
            sc = [s + _dot_nt(qm[d][lv][:, heads[h]], km[d][lv][:, heads[h]]) * pm_ref[d, lv]
                  for (d, h), s in zip(items, sc)]
        outs = [a + accs[d][h] + _dot(s, vs[d][:, heads[h]]) for (d, h), a, s in zip(items, inter, sc)]
        snew = [st * etot[d][:, heads[h]] + _dot_tn(vs[d][:, heads[h]], kd[d][:, heads[h]])
                for (d, h), st in zip(items, sts)]
        for (d, h), o, s in zip(items, outs, snew):
            o_refs[d][pl.ds(r0[d], c), heads[h]] = o
            st_ref[d, h] = s
        return carry

    lax.fori_loop(0, nc, chunk, 0, unroll=True)

    @pl.when(jnp.logical_not(is_lat))
    def _():
        for d, h in items:
            sfin_ref[d, h] = st_ref[d, h].T


def _hgrn_scan(proj, lb, s0t, layer, tables, n_ctx, tb, bw):
    t = proj.shape[0]
    nh = bw // HG_DK
    blk, first, seq = tables
    nb = blk.shape[1]
    mst, pm, nlev = _hgrn_tables_np(HG_C)
    kern = functools.partial(_hgrn_kernel, n_ctx=n_ctx, tb=tb, bw=bw, nlev=nlev)
    tail = (2, nh, HG_DK, HG_DK)
    dir_specs = [pl.BlockSpec((tb, bw), _dir_row(d, col)) for d in range(2) for col in (0, 1, 2 + d)]
    grid_spec = pltpu.PrefetchScalarGridSpec(
        num_scalar_prefetch=3,
        grid=(nb,),
        in_specs=dir_specs + [_const_spec((2, 1, bw)), _state_in_spec(tail, n_ctx, layer),
                              _const_spec(mst.shape), _const_spec(pm.shape)],
        out_specs=[pl.BlockSpec((tb, bw), _dir_row(0, 0)), pl.BlockSpec((tb, bw), _dir_row(1, 0)),
                   _state_out_spec(tail, n_ctx)],
        scratch_shapes=[pltpu.VMEM(tail, f32), pltpu.VMEM((2, tb, bw), f32),
                        pltpu.VMEM((2, tb, bw), f32), pltpu.VMEM((2, tb, bw), f32)],
    )
    return pl.pallas_call(
        kern, grid_spec=grid_spec,
        out_shape=[jax.ShapeDtypeStruct((t, bw), f32), jax.ShapeDtypeStruct((t, bw), f32),
                   jax.ShapeDtypeStruct((n_ctx,) + tail, f32)],
        compiler_params=_cparams(("arbitrary",)),
        name="hgrn_scan",
    )(jnp.asarray(blk), jnp.asarray(first), jnp.asarray(seq), *([proj] * 6), lb, s0t,
      jnp.asarray(mst), jnp.asarray(pm))


def _gdn_prep_kernel(prev_ref, cur_ref, nxt_ref, w_ref, o_ref, xe_ref, *, n_ctx, bps, tb, bw, halo):
    r = pl.program_id(0)
    part = pl.program_id(1)
    is_lat = r >= n_ctx
    pos = jnp.where(is_lat, (r - n_ctx) % bps, 0)
    pv = jnp.where(jnp.logical_and(is_lat, pos > 0), 1.0, 0.0)
    nv = jnp.where(jnp.logical_and(is_lat, pos < bps - 1), 1.0, 0.0)
    lat = jnp.where(is_lat, 1.0, 0.0)
    xe_ref[0:halo, :] = prev_ref[...] * pv
    xe_ref[halo:halo + tb, :] = cur_ref[...]
    xe_ref[halo + tb:halo + tb + halo, :] = nxt_ref[...] * nv
    col = lax.broadcasted_iota(jnp.int32, (tb, LANES), 0) % GRID_W
    m_lo = jnp.where(col >= 1, 1.0, 1.0 - lat)
    m_hi = jnp.where(col <= GRID_W - 2, 1.0, 1.0 - lat)
    for g in range(bw // LANES):
        gs = slice(g * LANES, (g + 1) * LANES)
        acc = None
        for dc in (-1, 0, 1):
            col_sum = None
            for dr in (-1, 0, 1):
                off = halo + GRID_W * dr + dc
                tap = (dr + 1) * CONV_K + (dc + 1)
                w = w_ref[tap:tap + 1, gs]
                if dr != 0:
                    w = w * lat
                term = xe_ref[off:off + tb, gs] * w
                col_sum = term if col_sum is None else col_sum + term
            if dc == -1:
                col_sum = col_sum * m_lo
            elif dc == 1:
                col_sum = col_sum * m_hi
            acc = col_sum if acc is None else acc + col_sum
        y = _silu(acc)
        ss = jnp.sum(y * y, axis=-1, keepdims=True)
        nrm = y * lax.rsqrt(ss + NORM_EPS)
        scale = jnp.where(part == 0, GD_DK ** -0.5, 1.0)
        o_ref[:, gs] = jnp.where(part == 2, y, nrm * scale)


def _gdn_prep(proj, conv_w, n_ctx, bps, tb, bw, qkv_col):
    t = proj.shape[0]
    nb = t // tb
    halo = LANES
    hb = tb // halo
    nhb = t // halo
    kern = functools.partial(_gdn_prep_kernel, n_ctx=n_ctx, bps=bps, tb=tb, bw=bw, halo=halo)
    return pl.pallas_call(
        kern,
        grid=(nb, 3),
        in_specs=[pl.BlockSpec((halo, bw), lambda r, p: (jnp.maximum(r * hb - 1, 0), qkv_col + p)),
                  pl.BlockSpec((tb, bw), lambda r, p: (r, qkv_col + p)),
                  pl.BlockSpec((halo, bw), lambda r, p: (jnp.minimum((r + 1) * hb, nhb - 1), qkv_col + p)),
                  pl.BlockSpec((CONV_K * CONV_K, bw), lambda r, p: (0, p))],
        out_specs=pl.BlockSpec((tb, bw), lambda r, p: (r, p)),
        out_shape=jax.ShapeDtypeStruct((t, 3 * bw), f32),
        scratch_shapes=[pltpu.VMEM((tb + 2 * halo, bw), f32)],
        compiler_params=_cparams(("arbitrary", "arbitrary")),
        name="gdn_prep",
    )(proj, proj, proj, conv_w)


def _gdn_kernel(blk_ref, first_ref, seq_ref, q0_ref, k0_ref, v0_ref, m0_ref, q1_ref, k1_ref, v1_ref, m1_ref,
                exa_ref, exb_ref, alog_ref, dtb_ref, s0_ref, tri_ref, msk_ref, o0_ref, o1_ref, sfin_ref,
                st_ref, lg_ref, be_ref, *, n_ctx, tb, bw, nmerge):
    j = pl.program_id(0)
    c = GD_C
    nc = tb // c
    nh = bw // GD_DK
    is_lat = seq_ref[j] >= n_ctx
    q_refs, k_refs, v_refs = (q0_ref, q1_ref), (k0_ref, k1_ref), (v0_ref, v1_ref)
    m_refs, o_refs = (m0_ref, m1_ref), (o0_ref, o1_ref)

    @pl.when(first_ref[j] == 1)
    def _():
        st_ref[...] = jnp.where(is_lat, s0_ref[...], 0.0)

    for d in range(2):
        ab = m_refs[d][:, MISC_W - LANES:MISC_W]
        a_in = _rdot01(ab, exa_ref[d])
        b_in = _rdot01(ab, exb_ref[d])
        lg_ref[d] = -jnp.exp(alog_ref[d]) * _softplus(a_in + dtb_ref[d])
        be_ref[d] = _sigmoid(b_in)

    tri = [tri_ref[d].astype(bf16) for d in range(2)]
    heads = [slice(h * GD_DK, (h + 1) * GD_DK) for h in range(nh)]
    items = [(d, h) for d in range(2) for h in range(nh)]
    strict = [msk_ref[d, 0] for d, _ in items]
    incl = [msk_ref[d, 1] for d, _ in items]
    mdiag = [msk_ref[d, 3] for d, _ in items]
    moffs = [[msk_ref[d, 4 + m] for m in range(nmerge)] for d, _ in items]

    def chunk(ci, carry):
        r0 = _chunk_rows(ci, nc, c)
        be, q, k, v, gcum, eg, qdec, kdec, gend = [], [], [], [], [], [], [], [], []
        for d in range(2):
            rows = pl.ds(r0[d], c)
            lg = lg_ref[d, rows, :]
            be.append(be_ref[d, rows, :])
            q.append(q_refs[d][rows, :])
            k.append(k_refs[d][rows, :])
            v.append(v_refs[d][rows, :])
            g = _ldot01(tri[d], lg)
            gtot = jnp.sum(lg, axis=0, keepdims=True)
            gcum.append(g)
            eg.append(jnp.exp(g))
            qdec.append(q[d] * eg[d])
            kdec.append(k[d] * jnp.exp(gtot - g))
            gend.append(jnp.exp(gtot))
        qk = [_dot_nt(jnp.concatenate([q[d][:, heads[h]], k[d][:, heads[h]]], axis=0), k[d][:, heads[h]])
              for d, h in items]
        dec = []
        for d, h in items:
            gc = gcum[d][:, heads[h]]
            dec.append(jnp.exp(jnp.minimum(gc - gc.T, 0.0)))
        ns = [be[d][:, heads[h]] * a[c:2 * c] * e * m for (d, h), a, e, m in zip(items, qk, dec, strict)]
        ys = _tri_inverse_y(ns, mdiag, moffs)
        rhs = [jnp.concatenate([be[d][:, heads[h]] * eg[d][:, heads[h]] * k[d][:, heads[h]],
                                be[d][:, heads[h]] * v[d][:, heads[h]]], axis=1) for d, h in items]
        wu = [r + _dotp(y, r) for y, r in zip(ys, rhs)]
        ps = [a[0:c] * e * m for a, e, m in zip(qk, dec, incl)]
        ss = [st_ref[d, h] for d, h in items]
        ws = [_dot(jnp.concatenate([x[:, 0:GD_DK], qdec[d][:, heads[h]]], axis=0), s)
              for (d, h), x, s in zip(items, wu, ss)]
        vnew = [x[:, GD_DK:2 * GD_DK] - y[0:c] for x, y in zip(wu, ws)]
        outs = [y[c:2 * c] + _dot(p, vn) for y, p, vn in zip(ws, ps, vnew)]
        snew = [gend[d][:, heads[h]] * s + _dot_tn(kdec[d][:, heads[h]], vn)
                for (d, h), s, vn in zip(items, ss, vnew)]
        for (d, h), o, s in zip(items, outs, snew):
            o_refs[d][pl.ds(r0[d], c), heads[h]] = o
            st_ref[d, h] = s
        return carry

    lax.fori_loop(0, nc, chunk, 0, unroll=True)

    @pl.when(jnp.logical_not(is_lat))
    def _():
        sfin_ref[...] = st_ref[...]


def _gdn_scan(qkv, proj, misc_col, exa, exb, alog, dtb, s0, layer, tables, n_ctx, tb, bw):
    t = proj.shape[0]
    nh = bw // GD_DK
    blk, first, seq = tables
    nb = blk.shape[1]
    c = GD_C
    tri = np.stack([_tri_np(c, False), _tri_np(c, True)]).astype(np.float32)
    inv = _inverse_masks_np(c, c)
    msk = np.stack([np.stack(list(_causal_np(c, c, rev)) + inv) for rev in (False, True)]).astype(np.float32)
    nmerge = len(inv) - 2
    kern = functools.partial(_gdn_kernel, n_ctx=n_ctx, tb=tb, bw=bw, nmerge=nmerge)
    tail = (2, nh, GD_DK, GD_DK)
    dir_specs = []
    for d in range(2):
        dir_specs += [pl.BlockSpec((tb, bw), _dir_row(d, col)) for col in range(3)]
        dir_specs += [pl.BlockSpec((tb, MISC_W), _dir_row(d, misc_col))]
    grid_spec = pltpu.PrefetchScalarGridSpec(
        num_scalar_prefetch=3,
        grid=(nb,),
        in_specs=dir_specs + [_const_spec((2, LANES, bw)), _const_spec((2, LANES, bw)), _const_spec((2, 1, bw)),
                              _const_spec((2, 1, bw)), _state_in_spec(tail, n_ctx, layer),
                              _const_spec(tri.shape), _const_spec(msk.shape)],
        out_specs=[pl.BlockSpec((tb, bw), _dir_row(0, 0)), pl.BlockSpec((tb, bw), _dir_row(1, 0)),
                   _state_out_spec(tail, n_ctx)],
        scratch_shapes=[pltpu.VMEM(tail, f32), pltpu.VMEM((2, tb, bw), f32), pltpu.VMEM((2, tb, bw), f32)],
    )
    return pl.pallas_call(
        kern, grid_spec=grid_spec,
        out_shape=[jax.ShapeDtypeStruct((t, bw), f32), jax.ShapeDtypeStruct((t, bw), f32),
                   jax.ShapeDtypeStruct((n_ctx,) + tail, f32)],
        compiler_params=_cparams(("arbitrary",)),
        name="gdn_scan",
    )(jnp.asarray(blk), jnp.asarray(first), jnp.asarray(seq), qkv, qkv, qkv, proj, qkv, qkv, qkv, proj,
      exa, exb, alog, dtb, s0, jnp.asarray(tri), jnp.asarray(msk))


def _rwkv_prep_kernel(r_ref, k_ref, v_ref, misc_ref, w0_ref, w2_ref, a0_ref, a2_ref, g2_ref, kkp_ref,
                      ka_ref, rk_ref, b64_ref, kk_ref, lw_ref, icl_ref, bonus_ref, og_ref):
    rr = r_ref[...]
    rk = k_ref[...]
    wl = jnp.tanh(misc_ref[:, 0:LANES])
    al = misc_ref[:, LANES:2 * LANES]
    gl = _sigmoid(misc_ref[:, 2 * LANES:3 * LANES])
    ka = ka_ref[...]
    bon = jnp.zeros_like(rr)
    for d in range(2):
        logw = -_softplus(-(w0_ref[d:d + 1, :] + _dot(wl, w2_ref[d]))) - 0.5
        lw_ref[d] = -jnp.exp(logw)
        icl = _sigmoid(a0_ref[d:d + 1, :] + _dot(al, a2_ref[d]))
        icl_ref[d] = icl
        bon = bon + rr * (rk * (1.0 + (icl - 1.0) * ka)) * rk_ref[...]
    og_ref[...] = _dot(gl, g2_ref[...])
    b64 = b64_ref[...]
    kx = rk * kkp_ref[...]
    kk_ref[...] = kx * lax.rsqrt(_rdot01(kx * kx, b64) + NORM_EPS)
    bonus_ref[...] = _rdot01(bon, b64) * v_ref[...]


def _rwkv_prep(proj, rkv_col, misc_col, w0, w2p, a0, a2p, g2p, kkp, ka, rkp, b64, tb, bw):
    t = proj.shape[0]
    full = lambda shape: pl.BlockSpec(shape, lambda r: (0,) * len(shape))
    return pl.pallas_call(
        _rwkv_prep_kernel,
        grid=(t // tb,),
        in_specs=[pl.BlockSpec((tb, bw), lambda r: (r, rkv_col)),
                  pl.BlockSpec((tb, bw), lambda r: (r, rkv_col + 1)),
                  pl.BlockSpec((tb, bw), lambda r: (r, rkv_col + 2)),
                  pl.BlockSpec((tb, MISC_W), lambda r: (r, misc_col)),
                  full((2, bw)), full((2, LANES, bw)), full((2, bw)), full((2, LANES, bw)),
                  full((LANES, bw)), full((1, bw)), full((1, bw)), full((1, bw)), full((bw, bw))],
        out_specs=[pl.BlockSpec((tb, bw), lambda r: (r, 0)),
                   pl.BlockSpec((2, tb, bw), lambda r: (0, r, 0)),
                   pl.BlockSpec((2, tb, bw), lambda r: (0, r, 0)),
                   pl.BlockSpec((tb, bw), lambda r: (r, 0)),
                   pl.BlockSpec((tb, bw), lambda r: (r, 0))],
        out_shape=[jax.ShapeDtypeStruct((t, bw), f32), jax.ShapeDtypeStruct((2, t, bw), f32),
                   jax.ShapeDtypeStruct((2, t, bw), f32), jax.ShapeDtypeStruct((t, bw), f32),
                   jax.ShapeDtypeStruct((t, bw), f32)],
        compiler_params=_cparams(("arbitrary",)),
        name="rwkv_prep",
    )(proj, proj, proj, proj, w0, w2p, a0, a2p, g2p, kkp, ka, rkp, b64)


def _rwkv_kernel(blk_ref, first_ref, seq_ref, r0_ref, k0_ref, v0_ref, kk0_ref, lw0_ref, icl0_ref,
                 r1_ref, k1_ref, v1_ref, kk1_ref, lw1_ref, icl1_ref, ka_ref, s0_ref, tri_ref, msk_ref,
                 o0_ref, o1_ref, sfin_ref, st_ref, *, n_ctx, tb, bw, nmerge):
    j = pl.program_id(0)
    c = RW_C
    nc = tb // c
    n2c = 2 * c
    npair = bw // LANES
    is_lat = seq_ref[j] >= n_ctx
    r_refs, k_refs, v_refs = (r0_ref, r1_ref), (k0_ref, k1_ref), (v0_ref, v1_ref)
    kk_refs, lw_refs, icl_refs, o_refs = (kk0_ref, kk1_ref), (lw0_ref, lw1_ref), (icl0_ref, icl1_ref), (o0_ref, o1_ref)

    @pl.when(first_ref[j] == 1)
    def _():
        st_ref[...] = jnp.where(is_lat, s0_ref[...], 0.0)

    tri = [tri_ref[d].astype(bf16) for d in range(2)]
    pairs = [slice(p * LANES, (p + 1) * LANES) for p in range(npair)]
    items = [(d, p) for d in range(2) for p in range(npair)]
    strict = [msk_ref[d, 0] for d, _ in items]
    incl2 = [jnp.concatenate([msk_ref[d, 1], msk_ref[d, 1]], axis=1) for d in range(2)]
    mdiag = [msk_ref[d, 3] for d, _ in items]
    moffs = [[msk_ref[d, 4 + m] for m in range(nmerge)] for d, _ in items]
    lane = lax.broadcasted_iota(jnp.int32, (c, LANES), 1)
    m_even = jnp.where(lane < RW_DK, 1.0, 0.0)
    m_odd = 1.0 - m_even
    ka = ka_ref[...]

    def stack2(x):
        return jnp.concatenate([x * m_even, x * m_odd], axis=0)

    def chunk(ci, carry):
        r0 = _chunk_rows(ci, nc, c)
        at, kt, bt, rt, rv, kdec, bdec, etot = [], [], [], [], [], [], [], []
        for d in range(2):
            rows = pl.ds(r0[d], c)
            lw = lw_refs[d][rows, :]
            icl = icl_refs[d][rows, :]
            kk = kk_refs[d][rows, :]
            cum = _ldot01(tri[d], lw)
            ctot = jnp.sum(lw, axis=0, keepdims=True)
            kdir = k_refs[d][rows, :] * (1.0 + (icl - 1.0) * ka)
            bvec = kk * icl
            einv = jnp.exp(-cum)
            edec = jnp.exp(ctot - cum)
            at.append((-kk) * jnp.exp(cum - lw))
            kt.append(kdir * einv)
            bt.append(bvec * einv)
            rt.append(r_refs[d][rows, :] * jnp.exp(cum))
            rv.append(v_refs[d][rows, :])
            kdec.append(kdir * edec)
            bdec.append(bvec * edec)
            etot.append(jnp.exp(ctot))
        a2 = [stack2(at[d][:, pairs[p]]) for d, p in items]
        r2 = [stack2(rt[d][:, pairs[p]]) for d, p in items]
        v2 = [stack2(rv[d][:, pairs[p]]) for d, p in items]
        kb2 = [jnp.concatenate([stack2(kt[d][:, pairs[p]]), stack2(bt[d][:, pairs[p]])], axis=0) for d, p in items]
        kbd2 = [jnp.concatenate([stack2(kdec[d][:, pairs[p]]), stack2(bdec[d][:, pairs[p]])], axis=0)
                for d, p in items]
        big = [_dot_nt(jnp.concatenate([a, r], axis=0), kb) for a, r, kb in zip(a2, r2, kb2)]
        aak = [x[0:n2c, 0:n2c] * m for x, m in zip(big, strict)]
        ns = [-(x[0:n2c, n2c:2 * n2c] * m) for x, m in zip(big, strict)]
        arkb = [x[n2c:2 * n2c] * incl2[d] for x, (d, _) in zip(big, items)]
        ys = _tri_inverse_y(ns, mdiag, moffs)
        rhs = [jnp.concatenate([a, _dot(m, v)], axis=1) for a, m, v in zip(a2, aak, v2)]
        tat = [x + _dotp(y, x) for y, x in zip(ys, rhs)]
        sts = [st_ref[d, p] for d, p in items]
        ur = [_dot_nt(jnp.concatenate([x[:, 0:LANES], r], axis=0), st) for x, r, st in zip(tat, r2, sts)]
        u2 = [y[0:n2c] + x[:, LANES:2 * LANES] for y, x in zip(ur, tat)]
        vu = [jnp.concatenate([v, u], axis=0) for v, u in zip(v2, u2)]
        y2 = [y[n2c:2 * n2c] + _dot(m, x) for y, m, x in zip(ur, arkb, vu)]
        snew = [st * etot[d][:, pairs[p]] + _dot_tn(x, kbd) for (d, p), st, x, kbd in zip(items, sts, vu, kbd2)]
        for (d, p), y, s in zip(items, y2, snew):
            o_refs[d][pl.ds(r0[d], c), pairs[p]] = y[0:c] + y[c:n2c]
            st_ref[d, p] = s
        return carry

    lax.fori_loop(0, nc, chunk, 0, unroll=True)

    @pl.when(jnp.logical_not(is_lat))
    def _():
        for d, p in items:
            st = st_ref[d, p]
            sfin_ref[d, 2 * p] = st[0:RW_DK, 0:RW_DK]
            sfin_ref[d, 2 * p + 1] = pltpu.roll(st, RW_DK, axis=1)[RW_DK:LANES, 0:RW_DK]


def _rwkv_scan(proj, rkv_col, kk, lw, icl, ka, s0p, layer, tables, n_ctx, tb, bw):
    t = proj.shape[0]
    npair = bw // LANES
    blk, first, seq = tables
    nb = blk.shape[1]
    c = RW_C
    n = 2 * c
    tri = np.stack([_tri_np(c, False), _tri_np(c, True)]).astype(np.float32)
    inv = _inverse_masks_np(n, c)
    msk = np.stack([np.stack(list(_causal_np(n, c, rev)) + inv) for rev in (False, True)]).astype(np.float32)
    nmerge = len(inv) - 2
    kern = functools.partial(_rwkv_kernel, n_ctx=n_ctx, tb=tb, bw=bw, nmerge=nmerge)
    tail = (2, npair, LANES, LANES)
    head_tail = (2, 2 * npair, RW_DK, RW_DK)
    dir_specs, dir_args = [], []
    for d in range(2):
        dir_specs += [pl.BlockSpec((tb, bw), _dir_row(d, rkv_col + i)) for i in range(3)]
        dir_specs += [pl.BlockSpec((tb, bw), _dir_row(d, 0)),
                      pl.BlockSpec((None, tb, bw), _dir_slab(d)), pl.BlockSpec((None, tb, bw), _dir_slab(d))]
        dir_args += [proj, proj, proj, kk, lw, icl]
    grid_spec = pltpu.PrefetchScalarGridSpec(
        num_scalar_prefetch=3,
        grid=(nb,),
        in_specs=dir_specs + [_const_spec((1, bw)), _state_in_spec(tail, n_ctx, layer),
                              _const_spec(tri.shape), _const_spec(msk.shape)],
        out_specs=[pl.BlockSpec((tb, bw), _dir_row(0, 0)), pl.BlockSpec((tb, bw), _dir_row(1, 0)),
                   _state_out_spec(head_tail, n_ctx)],
        scratch_shapes=[pltpu.VMEM(tail, f32)],
    )
    return pl.pallas_call(
        kern, grid_spec=grid_spec,
        out_shape=[jax.ShapeDtypeStruct((t, bw), f32), jax.ShapeDtypeStruct((t, bw), f32),
                   jax.ShapeDtypeStruct((n_ctx,) + head_tail, f32)],
        compiler_params=_cparams(("arbitrary",)),
        name="rwkv_scan",
    )(jnp.asarray(blk), jnp.asarray(first), jnp.asarray(seq), *dir_args, ka, s0p,
      jnp.asarray(tri), jnp.asarray(msk))


def _merge_kernel(x_ref, mod_ref, ohg0_ref, ohg1_ref, ogd0_ref, ogd1_ref, orw0_ref, orw1_ref, hgg_ref, gdz_ref,
                  bonus_ref, og_ref, g0_ref, g1_ref, g2_ref, hgn_ref, gdn_ref, lng_ref, lnb_ref, wb_ref, wo_ref,
                  o_ref, *, bw):
    def head_mean(x, width):
        outs = []
        for g in range(bw // LANES):
            xg = x[:, g * LANES:(g + 1) * LANES]
            if width == LANES:
                outs.append(jnp.broadcast_to(jnp.mean(xg, axis=-1, keepdims=True), xg.shape))
            else:
                lane = lax.broadcasted_iota(jnp.int32, xg.shape, 1)
                lo = lane < width
                s_lo = jnp.sum(jnp.where(lo, xg, 0.0), axis=-1, keepdims=True)
                s_hi = jnp.sum(jnp.where(lo, 0.0, xg), axis=-1, keepdims=True)
                outs.append(jnp.where(lo, s_lo, s_hi) * (1.0 / width))
        return jnp.concatenate(outs, axis=-1)

    ohg = ohg0_ref[...] + ohg1_ref[...]
    ohg = ohg * lax.rsqrt(head_mean(ohg * ohg, HG_DK) + NORM_EPS) * hgn_ref[...]
    ohg = ohg * _silu(hgg_ref[...])
    ogd = ogd0_ref[...] + ogd1_ref[...]
    ogd = ogd * lax.rsqrt(head_mean(ogd * ogd, GD_DK) + NORM_EPS) * gdn_ref[...]
    ogd = ogd * _silu(gdz_ref[...])
    orw = orw0_ref[...] + orw1_ref[...]
    mu = head_mean(orw, RW_DK)
    cen = orw - mu
    var = head_mean(cen * cen, RW_DK)
    orw = cen * lax.rsqrt(var + RW_GN_EPS) * lng_ref[...] + lnb_ref[...]
    orw = (orw + bonus_ref[...]) * og_ref[...]
    merged = (_sigmoid(g0_ref[...]) * _dot(ohg, wb_ref[0])
              + _sigmoid(g1_ref[...]) * _dot(ogd, wb_ref[1])
              + _sigmoid(g2_ref[...]) * _dot(orw, wb_ref[2]))
    o_ref[...] = x_ref[...] + mod_ref[5:6, :] * _dot(merged, wo_ref[...])


def _merge(x, mod, proj, branch_outs, bonus, og, hgn, gdn, lng, lnb, wb, wo, layer, cond_of_tile, tm, bw,
           hgg_col, gdz_col, gate_col):
    t, d = x.shape
    row = lambda col: (lambda i: (i, col))
    act = pl.BlockSpec((tm, bw), row(0))
    vec = pl.BlockSpec((1, bw), lambda i: (0, 0))
    once = pl.Buffered(1)
    return pl.pallas_call(
        functools.partial(_merge_kernel, bw=bw),
        grid=(t // tm,),
        in_specs=[pl.BlockSpec((tm, d), row(0)),
                  pl.BlockSpec((None, None, N_MOD, d), lambda i: (layer, cond_of_tile(i), 0, 0)),
                  act, act, act, act, act, act,
                  pl.BlockSpec((tm, bw), row(hgg_col)),
                  pl.BlockSpec((tm, bw), row(gdz_col)),
                  act, act,
                  pl.BlockSpec((tm, d), row(gate_col)),
                  pl.BlockSpec((tm, d), row(gate_col + 1)),
                  pl.BlockSpec((tm, d), row(gate_col + 2)),
                  vec, vec, vec, vec,
                  pl.BlockSpec((None, 3, bw, d), lambda i: (layer, 0, 0, 0), pipeline_mode=once),
                  pl.BlockSpec((None, d, d), lambda i: (layer, 0, 0), pipeline_mode=once)],
        out_specs=pl.BlockSpec((tm, d), row(0)),
        out_shape=jax.ShapeDtypeStruct((t, d), f32),
        compiler_params=_cparams(("arbitrary",)),
        name="merge",
    )(x, mod, *branch_outs, proj, proj, bonus, og, proj, proj, proj, hgn, gdn, lng, lnb, wb, wo)


def _final_norm_kernel(x_ref, g_ref, o_ref):
    x = x_ref[...]
    ms = jnp.mean(x * x, axis=-1, keepdims=True)
    o_ref[...] = x * lax.rsqrt(ms + NORM_EPS) * g_ref[...]


def _final_norm(x, g, tm, row0, rows):
    d = x.shape[1]
    first = row0 // tm
    return pl.pallas_call(
        _final_norm_kernel,
        grid=(rows // tm,),
        in_specs=[pl.BlockSpec((tm, d), lambda i: (first + i, 0)), pl.BlockSpec((1, d), lambda i: (0, 0))],
        out_specs=pl.BlockSpec((tm, d), lambda i: (i, 0)),
        out_shape=jax.ShapeDtypeStruct((rows, d), f32),
        compiler_params=_cparams(("arbitrary",)),
        name="final_norm",
    )(x, g.reshape(1, d))


def _pad_rows(w, rows):
    return jnp.concatenate([w, jnp.zeros((rows - w.shape[0],) + w.shape[1:], w.dtype)], axis=0)


def kernel(x_prompt, x_sample, state_hgrn, state_gdn, state_rwkv, c, c_ctx, mod_w, mod_b, norm_g, ffn_up,
           ffn_down, mix_in, hgrn_lb, hgrn_norm, gdn_conv, gdn_a_log, gdn_dt_bias, gdn_norm, rwkv_w0, rwkv_w2,
           rwkv_a0, rwkv_a2, rwkv_g2, rwkv_kk, rwkv_ka, rwkv_rk, rwkv_ln_g, rwkv_ln_b, mix_branch, mix_out,
           final_norm):
    bc, lc, d = x_prompt.shape
    bs, ls, _ = x_sample.shape
    depth = mod_w.shape[0]
    bw = d // 4
    tb = lc
    assert tb % GD_C == 0 and ls % tb == 0 and bw % LANES == 0 and tb % GRID_W == 0
    assert bs + 1 <= N_COND_PAD
    n_ctx = bc
    bps = ls // tb
    tc = bc * lc
    t = tc + bs * ls
    gd_h, rw_h = bw // GD_DK, bw // RW_DK
    lw_w, la_w, lg_w = rwkv_w2.shape[2], rwkv_a2.shape[2], rwkv_g2.shape[1]
    assert max(2 * lw_w, 2 * la_w, lg_w, 4 * gd_h) <= LANES

    def token_tile(cap):
        rows = tb
        while rows * 2 <= cap and tc % (rows * 2) == 0 and ls % (rows * 2) == 0:
            rows *= 2
        return rows

    tm, tp = token_tile(TM_FFN), token_tile(TM_PROJ)
    cond_of_tile = lambda tile_rows: (lambda i: jnp.where(i * tile_rows < tc, 0, 1 + (i * tile_rows - tc) // ls))

    x = (x_prompt.reshape(tc, d), x_sample.reshape(bs * ls, d))
    cond = jnp.concatenate([c_ctx[None, :], c, jnp.zeros((N_COND_PAD - 1 - bs, d), f32)], axis=0)
    mod_all = _modulation(cond, mod_w, mod_b).reshape(depth, N_COND_PAD, N_MOD, d)

    lb_p = jax.nn.softmax(hgrn_lb.astype(f32), axis=0)
    lb_all = jnp.cumsum(lb_p, axis=0) - lb_p[0:1]

    o_gda = 9 * bw
    o_rw = 9 * bw + 4 * gd_h
    o_lo = o_rw + 3 * bw
    o_mg = o_lo + 2 * lw_w + 2 * la_w + lg_w
    base = 24 * bw
    pieces = ((0, 9 * bw, 0), (o_rw, 3 * bw, 9 * bw), (o_mg, 3 * d, 12 * bw),
              (o_lo, 2 * lw_w, base), (o_lo + 2 * lw_w, 2 * la_w, base + LANES),
              (o_lo + 2 * lw_w + 2 * la_w, lg_w, base + 2 * LANES), (o_gda, 4 * gd_h, base + 3 * LANES))
    w_in = _relayout_mix_in(mix_in, pieces, base + MISC_W)
    col_gdq, col_gdz, col_rw, col_gate = 5, 8, 9, 3
    col_misc = (24 * bw) // MISC_W
    assert (24 * bw) % MISC_W == 0

    up16 = ffn_up.astype(bf16)
    down16 = ffn_down.astype(bf16)
    wb16 = mix_branch.astype(bf16)
    wo16 = mix_out.astype(bf16)

    exa = np.zeros((2, LANES, bw), np.float32)
    exb = np.zeros((2, LANES, bw), np.float32)
    for dd in range(2):
        for h in range(gd_h):
            exa[dd, dd * gd_h + h, h * GD_DK:(h + 1) * GD_DK] = 1.0
            exb[dd, 2 * gd_h + dd * gd_h + h, h * GD_DK:(h + 1) * GD_DK] = 1.0
    exa, exb = jnp.asarray(exa), jnp.asarray(exb)
    b64 = jnp.asarray(np.kron(np.eye(rw_h), np.ones((RW_DK, RW_DK))).astype(np.float32))

    tables = _block_tables_np(n_ctx, bs, bps)

    s0_hg = jnp.swapaxes(state_hgrn, -1, -2)
    s0_gd = state_gdn
    eye2 = jnp.eye(2, dtype=f32)
    s0_rw = state_rwkv.reshape(bs, depth, 2, rw_h // 2, 2, RW_DK, RW_DK)
    s0_rw = jnp.einsum('bldpeij,ef->bldpeifj', s0_rw, eye2).reshape(bs, depth, 2, rw_h // 2, LANES, LANES)

    new_hg, new_gd, new_rw = [], [], []
    for l in range(depth):
        mod = mod_all
        x = _ffn(x, mod, norm_g, up16, down16, l, 0, cond_of_tile(tm), tm)
        proj = _mixin(x, mod, norm_g, w_in, l, cond_of_tile(tp), tp)

        ohg0, ohg1, s_hg = _hgrn_scan(proj, lb_all[l].reshape(2, 1, bw), s0_hg, l, tables, n_ctx, tb, bw)

        conv_w = gdn_conv[l].reshape(CONV_K * CONV_K, 3 * bw)
        qkv = _gdn_prep(proj, conv_w, n_ctx, bps, tb, bw, col_gdq)
        alog = jnp.repeat(gdn_a_log[l], GD_DK, axis=-1).reshape(2, 1, bw)
        dtb = jnp.repeat(gdn_dt_bias[l], GD_DK, axis=-1).reshape(2, 1, bw)

        w2p = jnp.stack([_pad_rows(jnp.concatenate([jnp.zeros((dd * lw_w, bw), f32), rwkv_w2[l, dd]], 0), LANES)
                         for dd in range(2)])
        a2p = jnp.stack([_pad_rows(jnp.concatenate([jnp.zeros((dd * la_w, bw), f32), rwkv_a2[l, dd]], 0), LANES)
                         for dd in range(2)])
        g2p = _pad_rows(rwkv_g2[l], LANES)
        ka = rwkv_ka[l].reshape(1, bw)
        kk, lw, icl, bonus, og = _rwkv_prep(proj, col_rw, col_misc, rwkv_w0[l], w2p, rwkv_a0[l], a2p, g2p,
                                            rwkv_kk[l].reshape(1, bw), ka, rwkv_rk[l].reshape(1, bw), b64,
                                            tb, bw)
        ogd0, ogd1, s_gd = _gdn_scan(qkv, proj, col_misc, exa, exb, alog, dtb, s0_gd, l, tables, n_ctx, tb, bw)
        orw0, orw1, s_rw = _rwkv_scan(proj, col_rw, kk, lw, icl, ka, s0_rw, l, tables, n_ctx, tb, bw)

        x = _merge(x, mod, proj, (ohg0, ohg1, ogd0, ogd1, orw0, orw1), bonus, og, hgrn_norm[l].reshape(1, bw),
                   jnp.tile(gdn_norm[l], gd_h).reshape(1, bw), rwkv_ln_g[l].reshape(1, bw),
                   rwkv_ln_b[l].reshape(1, bw), wb16, wo16, l, cond_of_tile(tb), tb, bw,
                   4, col_gdz, col_gate)
        x = _ffn(x, mod, norm_g, up16, down16, l, 2, cond_of_tile(tm), tm)

        new_hg.append(s_hg)
        new_gd.append(s_gd)
        new_rw.append(s_rw)

    y_ctx = _final_norm(x, final_norm, tp, 0, tc)
    y_lat = _final_norm(x, final_norm, tp, tc, bs * ls)
    return (y_ctx.reshape(bc, lc, d), y_lat.reshape(bs, ls, d),
            jnp.stack(new_hg, axis=1), jnp.stack(new_gd, axis=1), jnp.stack(new_rw, axis=1))
```

```python
import functools

import numpy as np
import jax
import jax.numpy as jnp
from jax import lax
from jax.experimental import pallas as pl
from jax.experimental.pallas import tpu as pltpu

f32 = jnp.float32
bf16 = jnp.bfloat16

LANES = 128
NORM_ROWS = 16
MXU_W = 256
VMEM_LIMIT = 60 * 1024 * 1024

HG_DK = 128
GD_DK = 128
RW_DK = 64
N_MOD = 9
GRID_W = 64
CONV_K = 3
NORM_EPS = 1e-6
RW_GN_EPS = 64e-5
GATE_FLOOR = 1e-30
N_COND_PAD = 16

HG_C = 64
GD_C = 128
RW_C = 64
SUB = 8
INV_PASSES = 1
MISC_W = 4 * LANES
TM_FFN = 512
TM_PROJ = 1024


def _sigmoid(x):
    return 1.0 / (1.0 + jnp.exp(-x))


def _silu(x):
    return x * _sigmoid(x)


def _softplus(x):
    return jnp.maximum(x, 0.0) + jnp.log(1.0 + jnp.exp(-jnp.abs(x)))


def _dot(a, b):
    return jnp.dot(a.astype(bf16), b.astype(bf16), preferred_element_type=f32)


def _dot_nt(a, b):
    return lax.dot_general(a.astype(bf16), b.astype(bf16), (((1,), (1,)), ((), ())),
                           preferred_element_type=f32)


def _dot_tn(a, b):
    return lax.dot_general(a.astype(bf16), b.astype(bf16), (((0,), (0,)), ((), ())),
                           preferred_element_type=f32)


def _split3(x):
    hi = x.astype(bf16)
    r = x - hi.astype(f32)
    mid = r.astype(bf16)
    lo = (r - mid.astype(f32)).astype(bf16)
    return hi, mid, lo


def _ldot01(m, x):
    hi, mid, lo = _split3(x)
    m = m.astype(bf16)
    d = functools.partial(jnp.dot, preferred_element_type=f32)
    return d(m, hi) + d(m, mid) + d(m, lo)


def _rdot01(x, m):
    hi, mid, lo = _split3(x)
    m = m.astype(bf16)
    d = functools.partial(jnp.dot, preferred_element_type=f32)
    return d(hi, m) + d(mid, m) + d(lo, m)


def _dotp(a, b):
    if INV_PASSES == 1:
        return _dot(a, b)
    ah = a.astype(bf16)
    al = (a - ah.astype(f32)).astype(bf16)
    bh = b.astype(bf16)
    bl = (b - bh.astype(f32)).astype(bf16)
    d = functools.partial(jnp.dot, preferred_element_type=f32)
    return d(ah, bh) + d(ah, bl) + d(al, bh)


def _tri_inverse_y(ns, mdiag, moffs):
    nd = [n * m for n, m in zip(ns, mdiag)]
    n2 = [_dotp(a, a) for a in nd]
    n4 = [_dotp(a, a) for a in n2]
    y = [b - a - _dotp(a, b) for a, b in zip(nd, n2)]
    y = [a + b + _dotp(a, b) for a, b in zip(y, n4)]
    for lvl in range(len(moffs[0])):
        xl = [n * m[lvl] for n, m in zip(ns, moffs)]
        xl = [l + _dotp(a, l) for a, l in zip(y, xl)]
        y = [a - l - _dotp(l, a) for a, l in zip(y, xl)]
    return y


def _normmod_rows(read_rows, h_ref, g, scale, shift):
    gain = g * (1.0 + scale)

    def body(r, carry):
        r0 = pl.multiple_of(r * NORM_ROWS, NORM_ROWS)
        x = read_rows(r0)
        ms = jnp.mean(x * x, axis=-1, keepdims=True)
        h_ref[pl.ds(r0, NORM_ROWS), :] = ((x * lax.rsqrt(ms + NORM_EPS)) * gain + shift).astype(bf16)
        return carry

    lax.fori_loop(0, h_ref.shape[0] // NORM_ROWS, body, 0, unroll=8)


def _tri_np(c, rev):
    i = np.arange(c)
    return (i[None, :] >= i[:, None]) if rev else (i[None, :] <= i[:, None])


def _inverse_masks_np(n, c):
    i = np.arange(n)
    same = (i[:, None] // c) == (i[None, :] // c)
    eye = np.eye(n, dtype=bool)
    mdiag = (i[:, None] // SUB) == (i[None, :] // SUB)
    out = [eye, mdiag]
    m = SUB
    while m < c:
        out.append(same & ((i[:, None] // (2 * m)) == (i[None, :] // (2 * m)))
                   & ((i[:, None] // m) != (i[None, :] // m)))
        m *= 2
    return out


def _causal_np(n, c, rev):
    i = np.arange(n)
    same = (i[:, None] // c) == (i[None, :] // c)
    il, jl = (i % c)[:, None], (i % c)[None, :]
    strict = same & ((jl > il) if rev else (jl < il))
    incl = same & ((jl >= il) if rev else (jl <= il))
    return strict, incl


def _hgrn_tables_np(c):
    levels = []
    m = SUB
    while m < c:
        levels.append(m)
        m *= 2
    stacks, pmasks = [], []
    t = np.arange(c)
    for rev in (False, True):
        mats = [_tri_np(c, rev)]
        pms = []
        for m in levels:
            pair = t // (2 * m)
            half = (t // m) % 2
            tau = t[None, :]
            if not rev:
                bd = (pair * 2 * m + m - 1)[:, None]
                qside = (half == 1)[:, None]
                mat = np.where(qside, (tau > bd) & (tau <= t[:, None]), (tau > t[:, None]) & (tau <= bd))
                pm = (pair[:, None] == pair[None, :]) & (half == 1)[:, None] & (half == 0)[None, :]
            else:
                bd = (pair * 2 * m + m)[:, None]
                qside = (half == 0)[:, None]
                mat = np.where(qside, (tau >= t[:, None]) & (tau < bd), (tau >= bd) & (tau < t[:, None]))
                pm = (pair[:, None] == pair[None, :]) & (half == 0)[:, None] & (half == 1)[None, :]
            mats.append(mat)
            pms.append(pm)
        stacks.append(np.concatenate(mats, axis=0))
        pmasks.append(np.stack(pms, axis=0))
    return (np.stack(stacks).astype(np.float32), np.stack(pmasks).astype(np.float32), len(levels))


def _block_tables_np(n_ctx, n_lat_seq, bps):
    nb = n_ctx + n_lat_seq * bps
    blk = np.zeros((2, nb), np.int32)
    first = np.zeros((nb,), np.int32)
    seq = np.zeros((nb,), np.int32)
    j = 0
    for s in range(n_ctx):
        blk[0, j] = blk[1, j] = s
        first[j], seq[j] = 1, s
        j += 1
    for s in range(n_lat_seq):
        for p in range(bps):
            blk[0, j] = n_ctx + s * bps + p
            blk[1, j] = n_ctx + s * bps + bps - 1 - p
            first[j], seq[j] = int(p == 0), n_ctx + s
            j += 1
    return blk, first, seq


def _pick(n, cap):
    best = LANES
    t = LANES
    while t <= min(n, cap):
        if n % t == 0:
            best = t
        t += LANES
    return best


def _pick_mxu(n, cap):
    best = 0
    t = MXU_W
    while t <= min(n, cap):
        if n % t == 0:
            best = t
        t += MXU_W
    return best or _pick(n, cap)


def _cparams(sem):
    return pltpu.CompilerParams(dimension_semantics=sem, vmem_limit_bytes=VMEM_LIMIT)


def _mod_kernel(cond_ref, w_ref, b_ref, o_ref):
    x = cond_ref[...]
    o_ref[...] = _dot(_silu(x), w_ref[...]) + b_ref[...]


def _modulation(cond, mod_w, mod_b):
    depth, d, nd = mod_w.shape
    tn = _pick(nd, 1024)
    return pl.pallas_call(
        _mod_kernel,
        grid=(depth, nd // tn),
        in_specs=[pl.BlockSpec((N_COND_PAD, d), lambda l, j: (0, 0)),
                  pl.BlockSpec((None, d, tn), lambda l, j: (l, 0, j)),
                  pl.BlockSpec((None, 1, tn), lambda l, j: (l, 0, j))],
        out_specs=pl.BlockSpec((None, N_COND_PAD, tn), lambda l, j: (l, 0, j)),
        out_shape=jax.ShapeDtypeStruct((depth, N_COND_PAD, nd), f32),
        compiler_params=_cparams(("arbitrary", "arbitrary")),
        name="modulation",
    )(cond, mod_w, mod_b.reshape(depth, 1, nd))


def _relayout_kernel(w_ref, o_ref, *, pieces):
    nw = o_ref.shape[1]
    o_ref[:, nw - MISC_W:nw] = jnp.zeros((o_ref.shape[0], MISC_W), bf16)
    for src, n, dst in pieces:
        o_ref[:, dst:dst + n] = w_ref[:, src:src + n].astype(bf16)


def _relayout_mix_in(w, pieces, nw):
    depth, d, win = w.shape
    tr = LANES
    return pl.pallas_call(
        functools.partial(_relayout_kernel, pieces=pieces),
        grid=(depth, d // tr),
        in_specs=[pl.BlockSpec((None, tr, win), lambda l, r: (l, r, 0))],
        out_specs=pl.BlockSpec((None, tr, nw), lambda l, r: (l, r, 0)),
        out_shape=jax.ShapeDtypeStruct((depth, d, nw), bf16),
        compiler_params=_cparams(("arbitrary", "arbitrary")),
        name="relayout_mix_in",
    )(w)


def _ffn_kernel(*refs, slot, nf, split):
    if split is None:
        x_ref, mod_ref, g_ref, wa_ref, wb_ref, wd_ref, o_ref, h_ref = refs
        read_x = lambda rows: x_ref[rows, :]
    else:
        xa_ref, xb_ref, mod_ref, g_ref, wa_ref, wb_ref, wd_ref, o_ref, h_ref = refs
        first = pl.program_id(0) < split
        read_x = lambda rows: jnp.where(first, xa_ref[rows, :], xb_ref[rows, :])
    f = pl.program_id(1)

    @pl.when(f == 0)
    def _():
        _normmod_rows(lambda r0: read_x(pl.ds(r0, NORM_ROWS)), h_ref, g_ref[slot:slot + 1, :],
                      mod_ref[3 * slot + 1:3 * slot + 2, :], mod_ref[3 * slot:3 * slot + 1, :])
        o_ref[...] = jnp.zeros_like(o_ref)

    h = h_ref[...]
    a = jnp.dot(h, wa_ref[...], preferred_element_type=f32)
    b = jnp.dot(h, wb_ref[...], preferred_element_type=f32)
    act = (_silu(a) * b).astype(bf16)
    o_ref[...] += jnp.dot(act, wd_ref[...], preferred_element_type=f32)

    @pl.when(f == nf - 1)
    def _():
        o_ref[...] = read_x(slice(None)) + 0.5 * mod_ref[3 * slot + 2:3 * slot + 3, :] * o_ref[...]


def _ffn(xs, mod, norm_g, w_up, w_down, layer, slot, cond_of_tile, tm):
    if isinstance(xs, tuple):
        xa, xb = xs
        d = xa.shape[1]
        split = xa.shape[0] // tm
        assert xa.shape[0] % tm == 0 and xb.shape[0] % tm == 0
        t = xa.shape[0] + xb.shape[0]
        x_specs = [pl.BlockSpec((tm, d), lambda i, f: (jnp.minimum(i, split - 1), 0)),
                   pl.BlockSpec((tm, d), lambda i, f: (jnp.maximum(i - split, 0), 0))]
        x_args = [xa, xb]
    else:
        t, d = xs.shape
        split = None
        x_specs = [pl.BlockSpec((tm, d), lambda i, f: (i, 0))]
        x_args = [xs]
    dff = w_down.shape[2]
    tf = _pick(dff, 512)
    nf = dff // tf
    which = slot // 2
    return pl.pallas_call(
        functools.partial(_ffn_kernel, slot=slot, nf=nf, split=split),
        grid=(t // tm, nf),
        in_specs=x_specs + [
                  pl.BlockSpec((None, None, N_MOD, d), lambda i, f: (layer, cond_of_tile(i), 0, 0)),
                  pl.BlockSpec((None, 3, d), lambda i, f: (layer, 0, 0)),
                  pl.BlockSpec((None, None, d, tf), lambda i, f: (layer, which, 0, f)),
                  pl.BlockSpec((None, None, d, tf), lambda i, f: (layer, which, 0, f + nf)),
                  pl.BlockSpec((None, None, tf, d), lambda i, f: (layer, which, f, 0))],
        out_specs=pl.BlockSpec((tm, d), lambda i, f: (i, 0)),
        out_shape=jax.ShapeDtypeStruct((t, d), f32),
        scratch_shapes=[pltpu.VMEM((tm, d), bf16)],
        compiler_params=_cparams(("arbitrary", "arbitrary")),
        name="ffn%d" % slot,
    )(*x_args, mod, norm_g, w_up, w_up, w_down)


def _mixin_kernel(x_ref, mod_ref, g_ref, w_ref, o_ref, h_ref):
    @pl.when(pl.program_id(1) == 0)
    def _():
        _normmod_rows(lambda r0: x_ref[pl.ds(r0, NORM_ROWS), :], h_ref, g_ref[1:2, :], mod_ref[4:5, :],
                      mod_ref[3:4, :])

    o_ref[...] = jnp.dot(h_ref[...], w_ref[...], preferred_element_type=f32)


def _mixin(x, mod, norm_g, w, layer, cond_of_tile, tm):
    t, d = x.shape
    nw = w.shape[2]
    tn = _pick_mxu(nw, 1280)
    return pl.pallas_call(
        _mixin_kernel,
        grid=(t // tm, nw // tn),
        in_specs=[pl.BlockSpec((tm, d), lambda i, j: (i, 0)),
                  pl.BlockSpec((None, None, N_MOD, d), lambda i, j: (layer, cond_of_tile(i), 0, 0)),
                  pl.BlockSpec((None, 3, d), lambda i, j: (layer, 0, 0)),
                  pl.BlockSpec((None, d, tn), lambda i, j: (layer, 0, j))],
        out_specs=pl.BlockSpec((tm, tn), lambda i, j: (i, j)),
        out_shape=jax.ShapeDtypeStruct((t, nw), f32),
        scratch_shapes=[pltpu.VMEM((tm, d), bf16)],
        compiler_params=_cparams(("arbitrary", "arbitrary")),
        name="mix_in",
    )(x, mod, norm_g, w)


def _dir_row(d, col):
    return lambda j, b, f, s: (b[d, j], col)


def _dir_slab(d):
    return lambda j, b, f, s: (d, b[d, j], 0)


def _const_spec(shape):
    return pl.BlockSpec(shape, lambda j, b, f, s: (0,) * len(shape))


def _state_in_spec(tail, n_ctx, layer):
    return pl.BlockSpec((None, None) + tail,
                        lambda j, b, f, s: (jnp.maximum(s[j] - n_ctx, 0), layer) + (0,) * len(tail))


def _state_out_spec(tail, n_ctx):
    return pl.BlockSpec((None,) + tail, lambda j, b, f, s: (jnp.minimum(s[j], n_ctx - 1),) + (0,) * len(tail))


def _chunk_rows(ci, nc, c):
    return (pl.multiple_of(ci * c, c), pl.multiple_of((nc - 1 - ci) * c, c))


def _hgrn_kernel(blk_ref, first_ref, seq_ref, q0_ref, v0_ref, f0_ref, q1_ref, v1_ref, f1_ref, lb_ref, s0_ref,
                 mst_ref, pm_ref, o0_ref, o1_ref, sfin_ref, st_ref, qs_ref, ks_ref, lf_ref,
                 *, n_ctx, tb, bw, nlev):
    j = pl.program_id(0)
    c = HG_C
    nc = tb // c
    nh = bw // HG_DK
    is_lat = seq_ref[j] >= n_ctx
    q_refs, v_refs, f_refs, o_refs = (q0_ref, q1_ref), (v0_ref, v1_ref), (f0_ref, f1_ref), (o0_ref, o1_ref)

    @pl.when(first_ref[j] == 1)
    def _():
        st_ref[...] = jnp.where(is_lat, s0_ref[...], 0.0)

    for d in range(2):
        lb = lb_ref[d]
        fl = f_refs[d][...]
        sig = _sigmoid(fl)
        lf_ref[d] = jnp.log(jnp.maximum(lb + (1.0 - lb) * sig, GATE_FLOOR))
        ks_ref[d] = (1.0 - lb) * (1.0 - sig)
        qs_ref[d] = _silu(q_refs[d][...]) * HG_DK ** -0.5

    mst = [mst_ref[d].astype(bf16) for d in range(2)]
    tl = lax.broadcasted_iota(jnp.int32, (c, bw), 0) % SUB
    heads = [slice(h * HG_DK, (h + 1) * HG_DK) for h in range(nh)]
    items = [(d, h) for d in range(2) for h in range(nh)]

    def chunk(ci, carry):
        r0 = _chunk_rows(ci, nc, c)
        qe, kd, etot, qm, km, vs, accs = [], [], [], [], [], [], []
        for d in range(2):
            rows = pl.ds(r0[d], c)
            lf = lf_ref[d, rows, :]
            q = qs_ref[d, rows, :]
            k = ks_ref[d, rows, :]
            v = v_refs[d][rows, :]
            xs = _ldot01(mst[d], lf)
            b = xs[0:c]
            btot = jnp.sum(lf, axis=0, keepdims=True)
            qe.append(q * jnp.exp(b))
            kd.append(k * jnp.exp(btot - b))
            etot.append(jnp.exp(btot))
            es = [jnp.exp(xs[(lv + 1) * c:(lv + 2) * c]) for lv in range(nlev)]
            qm.append([q * e for e in es])
            km.append([k * e for e in es])
            vs.append(v)
            b3 = b.reshape(c // SUB, SUB, bw)
            k3 = k.reshape(c // SUB, SUB, bw)
            v3 = v.reshape(c // SUB, SUB, bw)
            acc = [jnp.zeros((c, HG_DK), f32) for _ in range(nh)]
            for sl in range(SUB):
                bs = jnp.broadcast_to(b3[:, sl:sl + 1, :], b3.shape).reshape(c, bw)
                kk = jnp.broadcast_to(k3[:, sl:sl + 1, :], k3.shape).reshape(c, bw)
                vv = jnp.broadcast_to(v3[:, sl:sl + 1, :], v3.shape).reshape(c, bw)
                valid = (tl >= sl) if d == 0 else (tl <= sl)
                p = jnp.where(valid, q * kk * jnp.exp(b - bs), 0.0)
                for h, hs in enumerate(heads):
                    acc[h] = acc[h] + jnp.sum(p[:, hs], axis=-1, keepdims=True) * vv[:, hs]
            accs.append(acc)
        sts = [st_ref[d, h] for d, h in items]
        inter = [_dot_nt(qe[d][:, heads[h]], st) for (d, h), st in zip(items, sts)]
        sc = [jnp.zeros((c, c), f32) for _ in items]
        for lv in range(nlev):
            sc = [s + _dot_nt(qm[d][lv][:, heads[h]], km[d][lv][:, heads[h]]) * pm_ref[d, lv]
                  for (d, h), s in zip(items, sc)]
        outs = [a + accs[d][h] + _dot(s, vs[d][:, heads[h]]) for (d, h), a, s in zip(items, inter, sc)]
        snew = [st * etot[d][:, heads[h]] + _dot_tn(vs[d][:, heads[h]], kd[d][:, heads[h]])
                for (d, h), st in zip(items, sts)]
        for (d, h), o, s in zip(items, outs, snew):
            o_refs[d][pl.ds(r0[d], c), heads[h]] = o
            st_ref[d, h] = s
        return carry

    lax.fori_loop(0, nc, chunk, 0, unroll=True)

    @pl.when(jnp.logical_not(is_lat))
    def _():
        for d, h in items:
            sfin_ref[d, h] = st_ref[d, h].T


def _hgrn_scan(proj, lb, s0t, layer, tables, n_ctx, tb, bw):
    t = proj.shape[0]
    nh = bw // HG_DK
    blk, first, seq = tables
    nb = blk.shape[1]
    mst, pm, nlev = _hgrn_tables_np(HG_C)
    kern = functools.partial(_hgrn_kernel, n_ctx=n_ctx, tb=tb, bw=bw, nlev=nlev)
    tail = (2, nh, HG_DK, HG_DK)
    dir_specs = [pl.BlockSpec((tb, bw), _dir_row(d, col)) for d in range(2) for col in (0, 1, 2 + d)]
    grid_spec = pltpu.PrefetchScalarGridSpec(
        num_scalar_prefetch=3,
        grid=(nb,),
        in_specs=dir_specs + [_const_spec((2, 1, bw)), _state_in_spec(tail, n_ctx, layer),
                              _const_spec(mst.shape), _const_spec(pm.shape)],
        out_specs=[pl.BlockSpec((tb, bw), _dir_row(0, 0)), pl.BlockSpec((tb, bw), _dir_row(1, 0)),
                   _state_out_spec(tail, n_ctx)],
        scratch_shapes=[pltpu.VMEM(tail, f32), pltpu.VMEM((2, tb, bw), f32),
                        pltpu.VMEM((2, tb, bw), f32), pltpu.VMEM((2, tb, bw), f32)],
    )
    return pl.pallas_call(
        kern, grid_spec=grid_spec,
        out_shape=[jax.ShapeDtypeStruct((t, bw), f32), jax.ShapeDtypeStruct((t, bw), f32),
                   jax.ShapeDtypeStruct((n_ctx,) + tail, f32)],
        compiler_params=_cparams(("arbitrary",)),
        name="hgrn_scan",
    )(jnp.asarray(blk), jnp.asarray(first), jnp.asarray(seq), *([proj] * 6), lb, s0t,
      jnp.asarray(mst), jnp.asarray(pm))


def _gdn_prep_kernel(prev_ref, cur_ref, nxt_ref, w_ref, o_ref, xe_ref, *, n_ctx, bps, tb, bw, halo):
    r = pl.program_id(0)
    part = pl.program_id(1)
    is_lat = r >= n_ctx
    pos = jnp.where(is_lat, (r - n_ctx) % bps, 0)
    pv = jnp.where(jnp.logical_and(is_lat, pos > 0), 1.0, 0.0)
    nv = jnp.where(jnp.logical_and(is_lat, pos < bps - 1), 1.0, 0.0)
    lat = jnp.where(is_lat, 1.0, 0.0)
    xe_ref[0:halo, :] = prev_ref[...] * pv
    xe_ref[halo:halo + tb, :] = cur_ref[...]
    xe_ref[halo + tb:halo + tb + halo, :] = nxt_ref[...] * nv
    col = lax.broadcasted_iota(jnp.int32, (tb, LANES), 0) % GRID_W
    m_lo = jnp.where(col >= 1, 1.0, 1.0 - lat)
    m_hi = jnp.where(col <= GRID_W - 2, 1.0, 1.0 - lat)
    for g in range(bw // LANES):
        gs = slice(g * LANES, (g + 1) * LANES)
        acc = None
        for dc in (-1, 0, 1):
            col_sum = None
            for dr in (-1, 0, 1):
                off = halo + GRID_W * dr + dc
                tap = (dr + 1) * CONV_K + (dc + 1)
                w = w_ref[tap:tap + 1, gs]
                if dr != 0:
                    w = w * lat
                term = xe_ref[off:off + tb, gs] * w
                col_sum = term if col_sum is None else col_sum + term
            if dc == -1:
                col_sum = col_sum * m_lo
            elif dc == 1:
                col_sum = col_sum * m_hi
            acc = col_sum if acc is None else acc + col_sum
        y = _silu(acc)
        ss = jnp.sum(y * y, axis=-1, keepdims=True)
        nrm = y * lax.rsqrt(ss + NORM_EPS)
        scale = jnp.where(part == 0, GD_DK ** -0.5, 1.0)
        o_ref[:, gs] = jnp.where(part == 2, y, nrm * scale)


def _gdn_prep(proj, conv_w, n_ctx, bps, tb, bw, qkv_col):
    t = proj.shape[0]
    nb = t // tb
    halo = LANES
    hb = tb // halo
    nhb = t // halo
    kern = functools.partial(_gdn_prep_kernel, n_ctx=n_ctx, bps=bps, tb=tb, bw=bw, halo=halo)
    return pl.pallas_call(
        kern,
        grid=(nb, 3),
        in_specs=[pl.BlockSpec((halo, bw), lambda r, p: (jnp.maximum(r * hb - 1, 0), qkv_col + p)),
                  pl.BlockSpec((tb, bw), lambda r, p: (r, qkv_col + p)),
                  pl.BlockSpec((halo, bw), lambda r, p: (jnp.minimum((r + 1) * hb, nhb - 1), qkv_col + p)),
                  pl.BlockSpec((CONV_K * CONV_K, bw), lambda r, p: (0, p))],
        out_specs=pl.BlockSpec((tb, bw), lambda r, p: (r, p)),
        out_shape=jax.ShapeDtypeStruct((t, 3 * bw), f32),
        scratch_shapes=[pltpu.VMEM((tb + 2 * halo, bw), f32)],
        compiler_params=_cparams(("arbitrary", "arbitrary")),
        name="gdn_prep",
    )(proj, proj, proj, conv_w)


def _gdn_kernel(blk_ref, first_ref, seq_ref, q0_ref, k0_ref, v0_ref, m0_ref, q1_ref, k1_ref, v1_ref, m1_ref,
                alog_ref, dtb_ref, s0_ref, tri_ref, msk_ref, o0_ref, o1_ref, sfin_ref,
                st_ref, lg_ref, be_ref, *, n_ctx, tb, bw, nmerge):
    j = pl.program_id(0)
    c = GD_C
    nc = tb // c
    nh = bw // GD_DK
    is_lat = seq_ref[j] >= n_ctx
    q_refs, k_refs, v_refs = (q0_ref, q1_ref), (k0_ref, k1_ref), (v0_ref, v1_ref)
    m_refs, o_refs = (m0_ref, m1_ref), (o0_ref, o1_ref)

    @pl.when(first_ref[j] == 1)
    def _():
        st_ref[...] = jnp.where(is_lat, s0_ref[...], 0.0)

    for d in range(2):
        ab = m_refs[d][:, MISC_W - LANES:MISC_W]
        spread = lambda j0: jnp.concatenate(
            [jnp.broadcast_to(ab[:, j0 + h:j0 + h + 1], (tb, GD_DK)) for h in range(nh)], axis=1)
        a_in = spread(d * nh)
        b_in = spread(2 * nh + d * nh)
        lg_ref[d] = -jnp.exp(alog_ref[d]) * _softplus(a_in + dtb_ref[d])
        be_ref[d] = _sigmoid(b_in)

    tri = [tri_ref[d].astype(bf16) for d in range(2)]
    heads = [slice(h * GD_DK, (h + 1) * GD_DK) for h in range(nh)]
    items = [(d, h) for d in range(2) for h in range(nh)]
    strict = [msk_ref[d, 0] for d, _ in items]
    incl = [msk_ref[d, 1] for d, _ in items]
    mdiag = [msk_ref[d, 3] for d, _ in items]
    moffs = [[msk_ref[d, 4 + m] for m in range(nmerge)] for d, _ in items]

    def chunk(ci, carry):
        r0 = _chunk_rows(ci, nc, c)
        be, q, k, v, gcum, eg, qdec, kdec, gend = [], [], [], [], [], [], [], [], []
        for d in range(2):
            rows = pl.ds(r0[d], c)
            lg = lg_ref[d, rows, :]
            be.append(be_ref[d, rows, :])
            q.append(q_refs[d][rows, :])
            k.append(k_refs[d][rows, :])
            v.append(v_refs[d][rows, :])
            g = _ldot01(tri[d], lg)
            gtot = jnp.sum(lg, axis=0, keepdims=True)
            gcum.append(g)
            eg.append(jnp.exp(g))
            qdec.append(q[d] * eg[d])
            kdec.append(k[d] * jnp.exp(gtot - g))
            gend.append(jnp.exp(gtot))
        qk = [_dot_nt(jnp.concatenate([q[d][:, heads[h]], k[d][:, heads[h]]], axis=0), k[d][:, heads[h]])
              for d, h in items]
        dec = []
        for d, h in items:
            gc = gcum[d][:, heads[h]]
            dec.append(jnp.exp(jnp.minimum(gc - gc.T, 0.0)))
        ns = [be[d][:, heads[h]] * a[c:2 * c] * e * m for (d, h), a, e, m in zip(items, qk, dec, strict)]
        ys = _tri_inverse_y(ns, mdiag, moffs)
        rhs = [jnp.concatenate([be[d][:, heads[h]] * eg[d][:, heads[h]] * k[d][:, heads[h]],
                                be[d][:, heads[h]] * v[d][:, heads[h]]], axis=1) for d, h in items]
        wu = [r + _dotp(y, r) for y, r in zip(ys, rhs)]
        ps = [a[0:c] * e * m for a, e, m in zip(qk, dec, incl)]
        ss = [st_ref[d, h] for d, h in items]
        ws = [_dot(jnp.concatenate([x[:, 0:GD_DK], qdec[d][:, heads[h]]], axis=0), s)
              for (d, h), x, s in zip(items, wu, ss)]
        vnew = [x[:, GD_DK:2 * GD_DK] - y[0:c] for x, y in zip(wu, ws)]
        outs = [y[c:2 * c] + _dot(p, vn) for y, p, vn in zip(ws, ps, vnew)]
        snew = [gend[d][:, heads[h]] * s + _dot_tn(kdec[d][:, heads[h]], vn)
                for (d, h), s, vn in zip(items, ss, vnew)]
        for (d, h), o, s in zip(items, outs, snew):
            o_refs[d][pl.ds(r0[d], c), heads[h]] = o
            st_ref[d, h] = s
        return carry

    lax.fori_loop(0, nc, chunk, 0, unroll=True)

    @pl.when(jnp.logical_not(is_lat))
    def _():
        sfin_ref[...] = st_ref[...]


def _gdn_scan(qkv, proj, misc_col, alog, dtb, s0, layer, tables, n_ctx, tb, bw):
    t = proj.shape[0]
    nh = bw // GD_DK
    blk, first, seq = tables
    nb = blk.shape[1]
    c = GD_C
    tri = np.stack([_tri_np(c, False), _tri_np(c, True)]).astype(np.float32)
    inv = _inverse_masks_np(c, c)
    msk = np.stack([np.stack(list(_causal_np(c, c, rev)) + inv) for rev in (False, True)]).astype(np.float32)
    nmerge = len(inv) - 2
    kern = functools.partial(_gdn_kernel, n_ctx=n_ctx, tb=tb, bw=bw, nmerge=nmerge)
    tail = (2, nh, GD_DK, GD_DK)
    dir_specs = []
    for d in range(2):
        dir_specs += [pl.BlockSpec((tb, bw), _dir_row(d, col)) for col in range(3)]
        dir_specs += [pl.BlockSpec((tb, MISC_W), _dir_row(d, misc_col))]
    grid_spec = pltpu.PrefetchScalarGridSpec(
        num_scalar_prefetch=3,
        grid=(nb,),
        in_specs=dir_specs + [_const_spec((2, 1, bw)), _const_spec((2, 1, bw)), _state_in_spec(tail, n_ctx, layer),
                              _const_spec(tri.shape), _const_spec(msk.shape)],
        out_specs=[pl.BlockSpec((tb, bw), _dir_row(0, 0)), pl.BlockSpec((tb, bw), _dir_row(1, 0)),
                   _state_out_spec(tail, n_ctx)],
        scratch_shapes=[pltpu.VMEM(tail, f32), pltpu.VMEM((2, tb, bw), f32), pltpu.VMEM((2, tb, bw), f32)],
    )
    return pl.pallas_call(
        kern, grid_spec=grid_spec,
        out_shape=[jax.ShapeDtypeStruct((t, bw), f32), jax.ShapeDtypeStruct((t, bw), f32),
                   jax.ShapeDtypeStruct((n_ctx,) + tail, f32)],
        compiler_params=_cparams(("arbitrary",)),
        name="gdn_scan",
    )(jnp.asarray(blk), jnp.asarray(first), jnp.asarray(seq), qkv, qkv, qkv, proj, qkv, qkv, qkv, proj,
      alog, dtb, s0, jnp.asarray(tri), jnp.asarray(msk))


def _rwkv_prep_kernel(r_ref, k_ref, v_ref, misc_ref, w0_ref, w2_ref, a0_ref, a2_ref, g2_ref, kkp_ref,
                      ka_ref, rk_ref, b64_ref, kk_ref, lw_ref, icl_ref, bonus_ref, og_ref):
    rr = r_ref[...]
    rk = k_ref[...]
    wl = jnp.tanh(misc_ref[:, 0:LANES])
    al = misc_ref[:, LANES:2 * LANES]
    gl = _sigmoid(misc_ref[:, 2 * LANES:3 * LANES])
    ka = ka_ref[...]
    bon = jnp.zeros_like(rr)
    for d in range(2):
        logw = -_softplus(-(w0_ref[d:d + 1, :] + _dot(wl, w2_ref[d]))) - 0.5
        lw_ref[d] = -jnp.exp(logw)
        icl = _sigmoid(a0_ref[d:d + 1, :] + _dot(al, a2_ref[d]))
        icl_ref[d] = icl
        bon = bon + rr * (rk * (1.0 + (icl - 1.0) * ka)) * rk_ref[...]
    og_ref[...] = _dot(gl, g2_ref[...])
    b64 = b64_ref[...]
    kx = rk * kkp_ref[...]
    kk_ref[...] = kx * lax.rsqrt(_rdot01(kx * kx, b64) + NORM_EPS)
    bonus_ref[...] = _rdot01(bon, b64) * v_ref[...]


def _rwkv_prep(proj, rkv_col, misc_col, w0, w2p, a0, a2p, g2p, kkp, ka, rkp, b64, tb, bw):
    t = proj.shape[0]
    full = lambda shape: pl.BlockSpec(shape, lambda r: (0,) * len(shape))
    return pl.pallas_call(
        _rwkv_prep_kernel,
        grid=(t // tb,),
        in_specs=[pl.BlockSpec((tb, bw), lambda r: (r, rkv_col)),
                  pl.BlockSpec((tb, bw), lambda r: (r, rkv_col + 1)),
                  pl.BlockSpec((tb, bw), lambda r: (r, rkv_col + 2)),
                  pl.BlockSpec((tb, MISC_W), lambda r: (r, misc_col)),
                  full((2, bw)), full((2, LANES, bw)), full((2, bw)), full((2, LANES, bw)),
                  full((LANES, bw)), full((1, bw)), full((1, bw)), full((1, bw)), full((bw, bw))],
        out_specs=[pl.BlockSpec((tb, bw), lambda r: (r, 0)),
                   pl.BlockSpec((2, tb, bw), lambda r: (0, r, 0)),
                   pl.BlockSpec((2, tb, bw), lambda r: (0, r, 0)),
                   pl.BlockSpec((tb, bw), lambda r: (r, 0)),
                   pl.BlockSpec((tb, bw), lambda r: (r, 0))],
        out_shape=[jax.ShapeDtypeStruct((t, bw), f32), jax.ShapeDtypeStruct((2, t, bw), f32),
                   jax.ShapeDtypeStruct((2, t, bw), f32), jax.ShapeDtypeStruct((t, bw), f32),
                   jax.ShapeDtypeStruct((t, bw), f32)],
        compiler_params=_cparams(("arbitrary",)),
        name="rwkv_prep",
    )(proj, proj, proj, proj, w0, w2p, a0, a2p, g2p, kkp, ka, rkp, b64)


def _rwkv_kernel(blk_ref, first_ref, seq_ref, r0_ref, k0_ref, v0_ref, kk0_ref, lw0_ref, icl0_ref,
                 r1_ref, k1_ref, v1_ref, kk1_ref, lw1_ref, icl1_ref, ka_ref, s0_ref, tri_ref, msk_ref,
                 o0_ref, o1_ref, sfin_ref, st_ref, *, n_ctx, tb, bw, nmerge):
    j = pl.program_id(0)
    c = RW_C
    nc = tb // c
    n2c = 2 * c
    npair = bw // LANES
    is_lat = seq_ref[j] >= n_ctx
    r_refs, k_refs, v_refs = (r0_ref, r1_ref), (k0_ref, k1_ref), (v0_ref, v1_ref)
    kk_refs, lw_refs, icl_refs, o_refs = (kk0_ref, kk1_ref), (lw0_ref, lw1_ref), (icl0_ref, icl1_ref), (o0_ref, o1_ref)

    @pl.when(first_ref[j] == 1)
    def _():
        st_ref[...] = jnp.where(is_lat, s0_ref[...], 0.0)

    tri = [tri_ref[d].astype(bf16) for d in range(2)]
    pairs = [slice(p * LANES, (p + 1) * LANES) for p in range(npair)]
    items = [(d, p) for d in range(2) for p in range(npair)]
    strict = [msk_ref[d, 0] for d, _ in items]
    incl2 = [jnp.concatenate([msk_ref[d, 1], msk_ref[d, 1]], axis=1) for d in range(2)]
    mdiag = [msk_ref[d, 3] for d, _ in items]
    moffs = [[msk_ref[d, 4 + m] for m in range(nmerge)] for d, _ in items]
    lane = lax.broadcasted_iota(jnp.int32, (c, LANES), 1)
    m_even = jnp.where(lane < RW_DK, 1.0, 0.0)
    m_odd = 1.0 - m_even
    ka = ka_ref[...]

    def stack2(x):
        return jnp.concatenate([x * m_even, x * m_odd], axis=0)

    def chunk(ci, carry):
        r0 = _chunk_rows(ci, nc, c)
        at, kt, bt, rt, rv, kdec, bdec, etot = [], [], [], [], [], [], [], []
        for d in range(2):
            rows = pl.ds(r0[d], c)
            lw = lw_refs[d][rows, :]
            icl = icl_refs[d][rows, :]
            kk = kk_refs[d][rows, :]
            cum = _ldot01(tri[d], lw)
            ctot = jnp.sum(lw, axis=0, keepdims=True)
            kdir = k_refs[d][rows, :] * (1.0 + (icl - 1.0) * ka)
            bvec = kk * icl
            einv = jnp.exp(-cum)
            edec = jnp.exp(ctot - cum)
            at.append((-kk) * jnp.exp(cum - lw))
            kt.append(kdir * einv)
            bt.append(bvec * einv)
            rt.append(r_refs[d][rows, :] * jnp.exp(cum))
            rv.append(v_refs[d][rows, :])
            kdec.append(kdir * edec)
            bdec.append(bvec * edec)
            etot.append(jnp.exp(ctot))
        a2 = [stack2(at[d][:, pairs[p]]) for d, p in items]
        r2 = [stack2(rt[d][:, pairs[p]]) for d, p in items]
        v2 = [stack2(rv[d][:, pairs[p]]) for d, p in items]
        kb2 = [jnp.concatenate([stack2(kt[d][:, pairs[p]]), stack2(bt[d][:, pairs[p]])], axis=0) for d, p in items]
        kbd2 = [jnp.concatenate([stack2(kdec[d][:, pairs[p]]), stack2(bdec[d][:, pairs[p]])], axis=0)
                for d, p in items]
        big = [_dot_nt(jnp.concatenate([a, r], axis=0), kb) for a, r, kb in zip(a2, r2, kb2)]
        aak = [x[0:n2c, 0:n2c] * m for x, m in zip(big, strict)]
        ns = [-(x[0:n2c, n2c:2 * n2c] * m) for x, m in zip(big, strict)]
        arkb = [x[n2c:2 * n2c] * incl2[d] for x, (d, _) in zip(big, items)]
        ys = _tri_inverse_y(ns, mdiag, moffs)
        rhs = [jnp.concatenate([a, _dot(m, v)], axis=1) for a, m, v in zip(a2, aak, v2)]
        tat = [x + _dotp(y, x) for y, x in zip(ys, rhs)]
        sts = [st_ref[d, p] for d, p in items]
        ur = [_dot_nt(jnp.concatenate([x[:, 0:LANES], r], axis=0), st) for x, r, st in zip(tat, r2, sts)]
        u2 = [y[0:n2c] + x[:, LANES:2 * LANES] for y, x in zip(ur, tat)]
        vu = [jnp.concatenate([v, u], axis=0) for v, u in zip(v2, u2)]
        y2 = [y[n2c:2 * n2c] + _dot(m, x) for y, m, x in zip(ur, arkb, vu)]
        snew = [st * etot[d][:, pairs[p]] + _dot_tn(x, kbd) for (d, p), st, x, kbd in zip(items, sts, vu, kbd2)]
        for (d, p), y, s in zip(items, y2, snew):
            o_refs[d][pl.ds(r0[d], c), pairs[p]] = y[0:c] + y[c:n2c]
            st_ref[d, p] = s
        return carry

    lax.fori_loop(0, nc, chunk, 0, unroll=True)

    @pl.when(jnp.logical_not(is_lat))
    def _():
        for d, p in items:
            st = st_ref[d, p]
            sfin_ref[d, 2 * p] = st[0:RW_DK, 0:RW_DK]
            sfin_ref[d, 2 * p + 1] = pltpu.roll(st, RW_DK, axis=1)[RW_DK:LANES, 0:RW_DK]


def _rwkv_scan(proj, rkv_col, kk, lw, icl, ka, s0p, layer, tables, n_ctx, tb, bw):
    t = proj.shape[0]
    npair = bw // LANES
    blk, first, seq = tables
    nb = blk.shape[1]
    c = RW_C
    n = 2 * c
    tri = np.stack([_tri_np(c, False), _tri_np(c, True)]).astype(np.float32)
    inv = _inverse_masks_np(n, c)
    msk = np.stack([np.stack(list(_causal_np(n, c, rev)) + inv) for rev in (False, True)]).astype(np.float32)
    nmerge = len(inv) - 2
    kern = functools.partial(_rwkv_kernel, n_ctx=n_ctx, tb=tb, bw=bw, nmerge=nmerge)
    tail = (2, npair, LANES, LANES)
    head_tail = (2, 2 * npair, RW_DK, RW_DK)
    dir_specs, dir_args = [], []
    for d in range(2):
        dir_specs += [pl.BlockSpec((tb, bw), _dir_row(d, rkv_col + i)) for i in range(3)]
        dir_specs += [pl.BlockSpec((tb, bw), _dir_row(d, 0)),
                      pl.BlockSpec((None, tb, bw), _dir_slab(d)), pl.BlockSpec((None, tb, bw), _dir_slab(d))]
        dir_args += [proj, proj, proj, kk, lw, icl]
    grid_spec = pltpu.PrefetchScalarGridSpec(
        num_scalar_prefetch=3,
        grid=(nb,),
        in_specs=dir_specs + [_const_spec((1, bw)), _state_in_spec(tail, n_ctx, layer),
                              _const_spec(tri.shape), _const_spec(msk.shape)],
        out_specs=[pl.BlockSpec((tb, bw), _dir_row(0, 0)), pl.BlockSpec((tb, bw), _dir_row(1, 0)),
                   _state_out_spec(head_tail, n_ctx)],
        scratch_shapes=[pltpu.VMEM(tail, f32)],
    )
    return pl.pallas_call(
        kern, grid_spec=grid_spec,
        out_shape=[jax.ShapeDtypeStruct((t, bw), f32), jax.ShapeDtypeStruct((t, bw), f32),
                   jax.ShapeDtypeStruct((n_ctx,) + head_tail, f32)],
        compiler_params=_cparams(("arbitrary",)),
        name="rwkv_scan",
    )(jnp.asarray(blk), jnp.asarray(first), jnp.asarray(seq), *dir_args, ka, s0p,
      jnp.asarray(tri), jnp.asarray(msk))


def _merge_kernel(x_ref, mod_ref, ohg0_ref, ohg1_ref, ogd0_ref, ogd1_ref, orw0_ref, orw1_ref, hgg_ref, gdz_ref,
                  bonus_ref, og_ref, g0_ref, g1_ref, g2_ref, hgn_ref, gdn_ref, lng_ref, lnb_ref, wb_ref, wo_ref,
                  o_ref, *, bw):
    def head_mean(x, width):
        outs = []
        for g in range(bw // LANES):
            xg = x[:, g * LANES:(g + 1) * LANES]
            if width == LANES:
                outs.append(jnp.broadcast_to(jnp.mean(xg, axis=-1, keepdims=True), xg.shape))
            else:
                lane = lax.broadcasted_iota(jnp.int32, xg.shape, 1)
                lo = lane < width
                s_lo = jnp.sum(jnp.where(lo, xg, 0.0), axis=-1, keepdims=True)
                s_hi = jnp.sum(jnp.where(lo, 0.0, xg), axis=-1, keepdims=True)
                outs.append(jnp.where(lo, s_lo, s_hi) * (1.0 / width))
        return jnp.concatenate(outs, axis=-1)

    ohg = ohg0_ref[...] + ohg1_ref[...]
    ohg = ohg * lax.rsqrt(head_mean(ohg * ohg, HG_DK) + NORM_EPS) * hgn_ref[...]
    ohg = ohg * _silu(hgg_ref[...])
    ogd = ogd0_ref[...] + ogd1_ref[...]
    ogd = ogd * lax.rsqrt(head_mean(ogd * ogd, GD_DK) + NORM_EPS) * gdn_ref[...]
    ogd = ogd * _silu(gdz_ref[...])
    orw = orw0_ref[...] + orw1_ref[...]
    mu = head_mean(orw, RW_DK)
    cen = orw - mu
    var = head_mean(cen * cen, RW_DK)
    orw = cen * lax.rsqrt(var + RW_GN_EPS) * lng_ref[...] + lnb_ref[...]
    orw = (orw + bonus_ref[...]) * og_ref[...]
    merged = (_sigmoid(g0_ref[...]) * _dot(ohg, wb_ref[0])
              + _sigmoid(g1_ref[...]) * _dot(ogd, wb_ref[1])
              + _sigmoid(g2_ref[...]) * _dot(orw, wb_ref[2]))
    o_ref[...] = x_ref[...] + mod_ref[5:6, :] * _dot(merged, wo_ref[...])


def _merge(x, mod, proj, branch_outs, bonus, og, hgn, gdn, lng, lnb, wb, wo, layer, cond_of_tile, tm, bw,
           hgg_col, gdz_col, gate_col):
    t, d = x.shape
    row = lambda col: (lambda i: (i, col))
    act = pl.BlockSpec((tm, bw), row(0))
    vec = pl.BlockSpec((1, bw), lambda i: (0, 0))
    once = pl.Buffered(1)
    return pl.pallas_call(
        functools.partial(_merge_kernel, bw=bw),
        grid=(t // tm,),
        in_specs=[pl.BlockSpec((tm, d), row(0)),
                  pl.BlockSpec((None, None, N_MOD, d), lambda i: (layer, cond_of_tile(i), 0, 0)),
                  act, act, act, act, act, act,
                  pl.BlockSpec((tm, bw), row(hgg_col)),
                  pl.BlockSpec((tm, bw), row(gdz_col)),
                  act, act,
                  pl.BlockSpec((tm, d), row(gate_col)),
                  pl.BlockSpec((tm, d), row(gate_col + 1)),
                  pl.BlockSpec((tm, d), row(gate_col + 2)),
                  vec, vec, vec, vec,
                  pl.BlockSpec((None, 3, bw, d), lambda i: (layer, 0, 0, 0), pipeline_mode=once),
                  pl.BlockSpec((None, d, d), lambda i: (layer, 0, 0), pipeline_mode=once)],
        out_specs=pl.BlockSpec((tm, d), row(0)),
        out_shape=jax.ShapeDtypeStruct((t, d), f32),
        compiler_params=_cparams(("arbitrary",)),
        name="merge",
    )(x, mod, *branch_outs, proj, proj, bonus, og, proj, proj, proj, hgn, gdn, lng, lnb, wb, wo)


def _final_norm_kernel(x_ref, g_ref, o_ref):
    x = x_ref[...]
    ms = jnp.mean(x * x, axis=-1, keepdims=True)
    o_ref[...] = x * lax.rsqrt(ms + NORM_EPS) * g_ref[...]


def _final_norm(x, g, tm, row0, rows):
    d = x.shape[1]
    first = row0 // tm
    return pl.pallas_call(
        _final_norm_kernel,
        grid=(rows // tm,),
        in_specs=[pl.BlockSpec((tm, d), lambda i: (first + i, 0)), pl.BlockSpec((1, d), lambda i: (0, 0))],
        out_specs=pl.BlockSpec((tm, d), lambda i: (i, 0)),
        out_shape=jax.ShapeDtypeStruct((rows, d), f32),
        compiler_params=_cparams(("arbitrary",)),
        name="final_norm",
    )(x, g.reshape(1, d))


def _pad_rows(w, rows):
    return jnp.concatenate([w, jnp.zeros((rows - w.shape[0],) + w.shape[1:], w.dtype)], axis=0)


def kernel(x_prompt, x_sample, state_hgrn, state_gdn, state_rwkv, c, c_ctx, mod_w, mod_b, norm_g, ffn_up,
           ffn_down, mix_in, hgrn_lb, hgrn_norm, gdn_conv, gdn_a_log, gdn_dt_bias, gdn_norm, rwkv_w0, rwkv_w2,
           rwkv_a0, rwkv_a2, rwkv_g2, rwkv_kk, rwkv_ka, rwkv_rk, rwkv_ln_g, rwkv_ln_b, mix_branch, mix_out,
           final_norm):
    bc, lc, d = x_prompt.shape
    bs, ls, _ = x_sample.shape
    depth = mod_w.shape[0]
    bw = d // 4
    tb = lc
    assert tb % GD_C == 0 and ls % tb == 0 and bw % LANES == 0 and tb % GRID_W == 0
    assert bs + 1 <= N_COND_PAD
    n_ctx = bc
    bps = ls // tb
    tc = bc * lc
    t = tc + bs * ls
    gd_h, rw_h = bw // GD_DK, bw // RW_DK
    lw_w, la_w, lg_w = rwkv_w2.shape[2], rwkv_a2.shape[2], rwkv_g2.shape[1]
    assert max(2 * lw_w, 2 * la_w, lg_w, 4 * gd_h) <= LANES

    def token_tile(cap):
        rows = tb
        while rows * 2 <= cap and tc % (rows * 2) == 0 and ls % (rows * 2) == 0:
            rows *= 2
        return rows

    tm, tp = token_tile(TM_FFN), token_tile(TM_PROJ)
    cond_of_tile = lambda tile_rows: (lambda i: jnp.where(i * tile_rows < tc, 0, 1 + (i * tile_rows - tc) // ls))

    x = (x_prompt.reshape(tc, d), x_sample.reshape(bs * ls, d))
    cond = jnp.concatenate([c_ctx[None, :], c, jnp.zeros((N_COND_PAD - 1 - bs, d), f32)], axis=0)
    mod_all = _modulation(cond, mod_w, mod_b).reshape(depth, N_COND_PAD, N_MOD, d)

    lb_p = jax.nn.softmax(hgrn_lb.astype(f32), axis=0)
    lb_all = jnp.cumsum(lb_p, axis=0) - lb_p[0:1]

    o_gda = 9 * bw
    o_rw = 9 * bw + 4 * gd_h
    o_lo = o_rw + 3 * bw
    o_mg = o_lo + 2 * lw_w + 2 * la_w + lg_w
    base = 24 * bw
    pieces = ((0, 9 * bw, 0), (o_rw, 3 * bw, 9 * bw), (o_mg, 3 * d, 12 * bw),
              (o_lo, 2 * lw_w, base), (o_lo + 2 * lw_w, 2 * la_w, base + LANES),
              (o_lo + 2 * lw_w + 2 * la_w, lg_w, base + 2 * LANES), (o_gda, 4 * gd_h, base + 3 * LANES))
    w_in = _relayout_mix_in(mix_in, pieces, base + MISC_W)
    col_gdq, col_gdz, col_rw, col_gate = 5, 8, 9, 3
    col_misc = (24 * bw) // MISC_W
    assert (24 * bw) % MISC_W == 0

    up16 = ffn_up.astype(bf16)
    down16 = ffn_down.astype(bf16)
    wb16 = mix_branch.astype(bf16)
    wo16 = mix_out.astype(bf16)

    b64 =jnp.asarray(np.kron(np.eye(rw_h), np.ones((RW_DK, RW_DK))).astype(np.float32))

    tables = _block_tables_np(n_ctx, bs, bps)

    s0_hg = jnp.swapaxes(state_hgrn, -1, -2)
    s0_gd = state_gdn
    eye2 = jnp.eye(2, dtype=f32)
    s0_rw = state_rwkv.reshape(bs, depth, 2, rw_h // 2, 2, RW_DK, RW_DK)
    s0_rw = jnp.einsum('bldpeij,ef->bldpeifj', s0_rw, eye2).reshape(bs, depth, 2, rw_h // 2, LANES, LANES)

    new_hg, new_gd, new_rw = [], [], []
    for l in range(depth):
        mod = mod_all
        x = _ffn(x, mod, norm_g, up16, down16, l, 0, cond_of_tile(tm), tm)
        proj = _mixin(x, mod, norm_g, w_in, l, cond_of_tile(tp), tp)

        ohg0, ohg1, s_hg = _hgrn_scan(proj, lb_all[l].reshape(2, 1, bw), s0_hg, l, tables, n_ctx, tb, bw)

        conv_w = gdn_conv[l].reshape(CONV_K * CONV_K, 3 * bw)
        qkv = _gdn_prep(proj, conv_w, n_ctx, bps, tb, bw, col_gdq)
        alog = jnp.repeat(gdn_a_log[l], GD_DK, axis=-1).reshape(2, 1, bw)
        dtb = jnp.repeat(gdn_dt_bias[l], GD_DK, axis=-1).reshape(2, 1, bw)

        w2p = jnp.stack([_pad_rows(jnp.concatenate([jnp.zeros((dd * lw_w, bw), f32), rwkv_w2[l, dd]], 0), LANES)
                         for dd in range(2)])
        a2p = jnp.stack([_pad_rows(jnp.concatenate([jnp.zeros((dd * la_w, bw), f32), rwkv_a2[l, dd]], 0), LANES)
                         for dd in range(2)])
        g2p = _pad_rows(rwkv_g2[l], LANES)
        ka = rwkv_ka[l].reshape(1, bw)
        kk, lw, icl, bonus, og = _rwkv_prep(proj, col_rw, col_misc, rwkv_w0[l], w2p, rwkv_a0[l], a2p, g2p,
                                            rwkv_kk[l].reshape(1, bw), ka, rwkv_rk[l].reshape(1, bw), b64,
                                            tb, bw)
        ogd0, ogd1, s_gd = _gdn_scan(qkv, proj, col_misc, alog, dtb, s0_gd, l, tables, n_ctx, tb, bw)
        orw0, orw1, s_rw = _rwkv_scan(proj, col_rw, kk, lw, icl, ka, s0_rw, l, tables, n_ctx, tb, bw)

        x = _merge(x, mod, proj, (ohg0, ohg1, ogd0, ogd1, orw0, orw1), bonus, og, hgrn_norm[l].reshape(1, bw),
                   jnp.tile(gdn_norm[l], gd_h).reshape(1, bw), rwkv_ln_g[l].reshape(1, bw),
                   rwkv_ln_b[l].reshape(1, bw), wb16, wo16, l, cond_of_tile(tb), tb, bw,
                   4, col_gdz, col_gate)
        x = _ffn(x, mod, norm_g, up16, down16, l, 2, cond_of_tile(tm), tm)

        new_hg.append(s_hg)
        new_gd.append(s_gd)
        new_rw.append(s_rw)

    y_ctx = _final_norm(x, final_norm, tp, 0, tc)
    y_lat = _final_norm(x, final_norm, tp, tc, bs * ls)
    return (y_ctx.reshape(bc, lc, d), y_lat.reshape(bs, ls, d),
            jnp.stack(new_hg, axis=1), jnp.stack(new_gd, axis=1), jnp.stack(new_rw, axis=1))
```

```python
import functools

import numpy as np
import jax
import jax.numpy as jnp
from jax import lax
from jax.experimental import pallas as pl
from jax.experimental.pallas import tpu as pltpu

f32 = jnp.float32
bf16 = jnp.bfloat16

LANES = 128
NORM_ROWS = 16
MXU_W = 256
VMEM_LIMIT = 60 * 1024 * 1024

HG_DK = 128
GD_DK = 128
RW_DK = 64
N_MOD = 9
GRID_W = 64
CONV_K = 3
NORM_EPS = 1e-6
RW_GN_EPS = 64e-5
GATE_FLOOR = 1e-30
N_COND_PAD = 16

HG_C = 64
GD_C = 128
RW_C = 64
SUB = 8
INV_PASSES = 1
MISC_W = 4 * LANES
TM_FFN = 512
TM_PROJ = 1024


def _sigmoid(x):
    return 1.0 / (1.0 + jnp.exp(-x))


def _silu(x):
    return x * _sigmoid(x)


def _softplus(x):
    return jnp.maximum(x, 0.0) + jnp.log(1.0 + jnp.exp(-jnp.abs(x)))


def _dot(a, b):
    return jnp.dot(a.astype(bf16), b.astype(bf16), preferred_element_type=f32)


def _dot_nt(a, b):
    return lax.dot_general(a.astype(bf16), b.astype(bf16), (((1,), (1,)), ((), ())),
                           preferred_element_type=f32)


def _dot_tn(a, b):
    return lax.dot_general(a.astype(bf16), b.astype(bf16), (((0,), (0,)), ((), ())),
                           preferred_element_type=f32)


def _split3(x):
    hi = x.astype(bf16)
    r = x - hi.astype(f32)
    mid = r.astype(bf16)
    lo = (r - mid.astype(f32)).astype(bf16)
    return hi, mid, lo


def _ldot01(m, x):
    hi, mid, lo = _split3(x)
    m = m.astype(bf16)
    d = functools.partial(jnp.dot, preferred_element_type=f32)
    return d(m, hi) + d(m, mid) + d(m, lo)


def _rdot01(x, m):
    hi, mid, lo = _split3(x)
    m = m.astype(bf16)
    d = functools.partial(jnp.dot, preferred_element_type=f32)
    return d(hi, m) + d(mid, m) + d(lo, m)


def _dotp(a, b):
    if INV_PASSES == 1:
        return _dot(a, b)
    ah = a.astype(bf16)
    al = (a - ah.astype(f32)).astype(bf16)
    bh = b.astype(bf16)
    bl = (b - bh.astype(f32)).astype(bf16)
    d = functools.partial(jnp.dot, preferred_element_type=f32)
    return d(ah, bh) + d(ah, bl) + d(al, bh)


def _tri_inverse_y(ns, mdiag, moffs):
    nd = [n * m for n, m in zip(ns, mdiag)]
    n2 = [_dotp(a, a) for a in nd]
    n4 = [_dotp(a, a) for a in n2]
    y = [b - a - _dotp(a, b) for a, b in zip(nd, n2)]
    y = [a + b + _dotp(a, b) for a, b in zip(y, n4)]
    for lvl in range(len(moffs[0])):
        xl = [n * m[lvl] for n, m in zip(ns, moffs)]
        xl = [l + _dotp(a, l) for a, l in zip(y, xl)]
        y = [a - l - _dotp(l, a) for a, l in zip(y, xl)]
    return y


def _normmod_rows(read_rows, h_ref, g, scale, shift):
    gain = g * (1.0 + scale)

    def body(r, carry):
        r0 = pl.multiple_of(r * NORM_ROWS, NORM_ROWS)
        x = read_rows(r0)
        ms = jnp.mean(x * x, axis=-1, keepdims=True)
        h_ref[pl.ds(r0, NORM_ROWS), :] = ((x * lax.rsqrt(ms + NORM_EPS)) * gain + shift).astype(bf16)
        return carry

    lax.fori_loop(0, h_ref.shape[0] // NORM_ROWS, body, 0, unroll=8)


def _tri_np(c, rev):
    i = np.arange(c)
    return (i[None, :] >= i[:, None]) if rev else (i[None, :] <= i[:, None])


def _inverse_masks_np(n, c):
    i = np.arange(n)
    same = (i[:, None] // c) == (i[None, :] // c)
    eye = np.eye(n, dtype=bool)
    mdiag = (i[:, None] // SUB) == (i[None, :] // SUB)
    out = [eye, mdiag]
    m = SUB
    while m < c:
        out.append(same & ((i[:, None] // (2 * m)) == (i[None, :] // (2 * m)))
                   & ((i[:, None] // m) != (i[None, :] // m)))
        m *= 2
    return out


def _causal_np(n, c, rev):
    i = np.arange(n)
    same = (i[:, None] // c) == (i[None, :] // c)
    il, jl = (i % c)[:, None], (i % c)[None, :]
    strict = same & ((jl > il) if rev else (jl < il))
    incl = same & ((jl >= il) if rev else (jl <= il))
    return strict, incl


def _hgrn_tables_np(c):
    levels = []
    m = SUB
    while m < c:
        levels.append(m)
        m *= 2
    stacks, pmasks = [], []
    t = np.arange(c)
    for rev in (False, True):
        mats = [_tri_np(c, rev)]
        pms = []
        for m in levels:
            pair = t // (2 * m)
            half = (t // m) % 2
            tau = t[None, :]
            if not rev:
                bd = (pair * 2 * m + m - 1)[:, None]
                qside = (half == 1)[:, None]
                mat = np.where(qside, (tau > bd) & (tau <= t[:, None]), (tau > t[:, None]) & (tau <= bd))
                pm = (pair[:, None] == pair[None, :]) & (half == 1)[:, None] & (half == 0)[None, :]
            else:
                bd = (pair * 2 * m + m)[:, None]
                qside = (half == 0)[:, None]
                mat = np.where(qside, (tau >= t[:, None]) & (tau < bd), (tau >= bd) & (tau < t[:, None]))
                pm = (pair[:, None] == pair[None, :]) & (half == 0)[:, None] & (half == 1)[None, :]
            mats.append(mat)
            pms.append(pm)
        stacks.append(np.concatenate(mats, axis=0))
        pmasks.append(np.stack(pms, axis=0))
    return (np.stack(stacks).astype(np.float32), np.stack(pmasks).astype(np.float32), len(levels))


def _block_tables_np(n_ctx, n_lat_seq, bps):
    nb = n_ctx + n_lat_seq * bps
    blk = np.zeros((2, nb), np.int32)
    first = np.zeros((nb,), np.int32)
    seq = np.zeros((nb,), np.int32)
    j = 0
    for s in range(n_ctx):
        blk[0, j] = blk[1, j] = s
        first[j], seq[j] = 1, s
        j += 1
    for s in range(n_lat_seq):
        for p in range(bps):
            blk[0, j] = n_ctx + s * bps + p
            blk[1, j] = n_ctx + s * bps + bps - 1 - p
            first[j], seq[j] = int(p == 0), n_ctx + s
            j += 1
    return blk, first, seq


def _pick(n, cap):
    best = LANES
    t = LANES
    while t <= min(n, cap):
        if n % t == 0:
            best = t
        t += LANES
    return best


def _pick_mxu(n, cap):
    best = 0
    t = MXU_W
    while t <= min(n, cap):
        if n % t == 0:
            best = t
        t += MXU_W
    return best or _pick(n, cap)


def _cparams(sem):
    return pltpu.CompilerParams(dimension_semantics=sem, vmem_limit_bytes=VMEM_LIMIT)


def _mod_kernel(cond_ref, w_ref, b_ref, o_ref):
    x = cond_ref[...]
    o_ref[...] = _dot(_silu(x), w_ref[...]) + b_ref[...]


def _modulation(cond, mod_w, mod_b):
    depth, d, nd = mod_w.shape
    tn = _pick(nd, 1024)
    return pl.pallas_call(
        _mod_kernel,
        grid=(depth, nd // tn),
        in_specs=[pl.BlockSpec((N_COND_PAD, d), lambda l, j: (0, 0)),
                  pl.BlockSpec((None, d, tn), lambda l, j: (l, 0, j)),
                  pl.BlockSpec((None, 1, tn), lambda l, j: (l, 0, j))],
        out_specs=pl.BlockSpec((None, N_COND_PAD, tn), lambda l, j: (l, 0, j)),
        out_shape=jax.ShapeDtypeStruct((depth, N_COND_PAD, nd), f32),
        compiler_params=_cparams(("arbitrary", "arbitrary")),
        name="modulation",
    )(cond, mod_w, mod_b.reshape(depth, 1, nd))


def _relayout_kernel(w_ref, o_ref, *, pieces):
    nw = o_ref.shape[1]
    o_ref[:, nw - MISC_W:nw] = jnp.zeros((o_ref.shape[0], MISC_W), bf16)
    for src, n, dst in pieces:
        o_ref[:, dst:dst + n] = w_ref[:, src:src + n].astype(bf16)


def _relayout_mix_in(w, pieces, nw):
    depth, d, win = w.shape
    tr = LANES
    return pl.pallas_call(
        functools.partial(_relayout_kernel, pieces=pieces),
        grid=(depth, d // tr),
        in_specs=[pl.BlockSpec((None, tr, win), lambda l, r: (l, r, 0))],
        out_specs=pl.BlockSpec((None, tr, nw), lambda l, r: (l, r, 0)),
        out_shape=jax.ShapeDtypeStruct((depth, d, nw), bf16),
        compiler_params=_cparams(("arbitrary", "arbitrary")),
        name="relayout_mix_in",
    )(w)


def _ffn_kernel(*refs, slot, nf, split):
    if split is None:
        x_ref, mod_ref, g_ref, wa_ref, wb_ref, wd_ref, o_ref, h_ref = refs
        read_x = lambda rows: x_ref[rows, :]
    else:
        xa_ref, xb_ref, mod_ref, g_ref, wa_ref, wb_ref, wd_ref, o_ref, h_ref = refs
        first = pl.program_id(0) < split
        read_x = lambda rows: jnp.where(first, xa_ref[rows, :], xb_ref[rows, :])
    f = pl.program_id(1)

    @pl.when(f == 0)
    def _():
        _normmod_rows(lambda r0: read_x(pl.ds(r0, NORM_ROWS)), h_ref, g_ref[slot:slot + 1, :],
                      mod_ref[3 * slot + 1:3 * slot + 2, :], mod_ref[3 * slot:3 * slot + 1, :])
        o_ref[...] = jnp.zeros_like(o_ref)

    h = h_ref[...]
    a = jnp.dot(h, wa_ref[...], preferred_element_type=f32)
    b = jnp.dot(h, wb_ref[...], preferred_element_type=f32)
    act = (_silu(a) * b).astype(bf16)
    o_ref[...] += jnp.dot(act, wd_ref[...], preferred_element_type=f32)

    @pl.when(f == nf - 1)
    def _():
        o_ref[...] = read_x(slice(None)) + 0.5 * mod_ref[3 * slot + 2:3 * slot + 3, :] * o_ref[...]


def _ffn(xs, mod, norm_g, w_up, w_down, layer, slot, cond_of_tile, tm):
    if isinstance(xs, tuple):
        xa, xb = xs
        d = xa.shape[1]
        split = xa.shape[0] // tm
        assert xa.shape[0] % tm == 0 and xb.shape[0] % tm == 0
        t = xa.shape[0] + xb.shape[0]
        x_specs = [pl.BlockSpec((tm, d), lambda i, f: (jnp.minimum(i, split - 1), 0)),
                   pl.BlockSpec((tm, d), lambda i, f: (jnp.maximum(i - split, 0), 0))]
        x_args = [xa, xb]
    else:
        t, d = xs.shape
        split = None
        x_specs = [pl.BlockSpec((tm, d), lambda i, f: (i, 0))]
        x_args = [xs]
    dff = w_down.shape[2]
    tf = _pick(dff, 512)
    nf = dff // tf
    which = slot // 2
    return pl.pallas_call(
        functools.partial(_ffn_kernel, slot=slot, nf=nf, split=split),
        grid=(t // tm, nf),
        in_specs=x_specs + [
                  pl.BlockSpec((None, None, N_MOD, d), lambda i, f: (layer, cond_of_tile(i), 0, 0)),
                  pl.BlockSpec((None, 3, d), lambda i, f: (layer, 0, 0)),
                  pl.BlockSpec((None, None, d, tf), lambda i, f: (layer, which, 0, f)),
                  pl.BlockSpec((None, None, d, tf), lambda i, f: (layer, which, 0, f + nf)),
                  pl.BlockSpec((None, None, tf, d), lambda i, f: (layer, which, f, 0))],
        out_specs=pl.BlockSpec((tm, d), lambda i, f: (i, 0)),
        out_shape=jax.ShapeDtypeStruct((t, d), f32),
        scratch_shapes=[pltpu.VMEM((tm, d), bf16)],
        compiler_params=_cparams(("arbitrary", "arbitrary")),
        name="ffn%d" % slot,
    )(*x_args, mod, norm_g, w_up, w_up, w_down)


def _mixin_kernel(x_ref, mod_ref, g_ref, w_ref, o_ref, h_ref):
    @pl.when(pl.program_id(1) == 0)
    def _():
        _normmod_rows(lambda r0: x_ref[pl.ds(r0, NORM_ROWS), :], h_ref, g_ref[1:2, :], mod_ref[4:5, :],
                      mod_ref[3:4, :])

    o_ref[...] = jnp.dot(h_ref[...], w_ref[...], preferred_element_type=f32)


def _mixin(x, mod, norm_g, w, layer, cond_of_tile, tm):
    t, d = x.shape
    nw = w.shape[2]
    tn = _pick_mxu(nw, 1280)
    return pl.pallas_call(
        _mixin_kernel,
        grid=(t // tm, nw // tn),
        in_specs=[pl.BlockSpec((tm, d), lambda i, j: (i, 0)),
                  pl.BlockSpec((None, None, N_MOD, d), lambda i, j: (layer, cond_of_tile(i), 0, 0)),
                  pl.BlockSpec((None, 3, d), lambda i, j: (layer, 0, 0)),
                  pl.BlockSpec((None, d, tn), lambda i, j: (layer, 0, j))],
        out_specs=pl.BlockSpec((tm, tn), lambda i, j: (i, j)),
        out_shape=jax.ShapeDtypeStruct((t, nw), f32),
        scratch_shapes=[pltpu.VMEM((tm, d), bf16)],
        compiler_params=_cparams(("arbitrary", "arbitrary")),
        name="mix_in",
    )(x, mod, norm_g, w)


def _dir_row(d, col):
    return lambda j, b, f, s: (b[d, j], col)


def _dir_slab(d):
    return lambda j, b, f, s: (d, b[d, j], 0)


def _const_spec(shape):
    return pl.BlockSpec(shape, lambda j, b, f, s: (0,) * len(shape))


def _state_in_spec(tail, n_ctx, layer):
    return pl.BlockSpec((None, None) + tail,
                        lambda j, b, f, s: (jnp.maximum(s[j] - n_ctx, 0), layer) + (0,) * len(tail))


def _state_out_spec(tail, n_ctx):
    return pl.BlockSpec((None,) + tail, lambda j, b, f, s: (jnp.minimum(s[j], n_ctx - 1),) + (0,) * len(tail))


def _chunk_rows(ci, nc, c):
    return (pl.multiple_of(ci * c, c), pl.multiple_of((nc - 1 - ci) * c, c))


def _hgrn_kernel(blk_ref, first_ref, seq_ref, q0_ref, v0_ref, f0_ref, q1_ref, v1_ref, f1_ref, lb_ref, s0_ref,
                 mst_ref, pm_ref, o0_ref, o1_ref, sfin_ref, st_ref, qs_ref, ks_ref, lf_ref,
                 *, n_ctx, tb, bw, nlev):
    j = pl.program_id(0)
    c = HG_C
    nc = tb // c
    nh = bw // HG_DK
    is_lat = seq_ref[j] >= n_ctx
    q_refs, v_refs, f_refs, o_refs = (q0_ref, q1_ref), (v0_ref, v1_ref), (f0_ref, f1_ref), (o0_ref, o1_ref)

    @pl.when(first_ref[j] == 1)
    def _():
        st_ref[...] = jnp.where(is_lat, s0_ref[...], 0.0)

    for d in range(2):
        lb = lb_ref[d]
        fl = f_refs[d][...]
        sig = _sigmoid(fl)
        lf_ref[d] = jnp.log(jnp.maximum(lb + (1.0 - lb) * sig, GATE_FLOOR))
        ks_ref[d] = (1.0 - lb) * (1.0 - sig)
        qs_ref[d] = _silu(q_refs[d][...]) * HG_DK ** -0.5

    mst = [mst_ref[d].astype(bf16) for d in range(2)]
    tl = lax.broadcasted_iota(jnp.int32, (c, bw), 0) % SUB
    heads = [slice(h * HG_DK, (h + 1) * HG_DK) for h in range(nh)]
    items = [(d, h) for d in range(2) for h in range(nh)]

    def chunk(ci, carry):
        r0 = _chunk_rows(ci, nc, c)
        qe, kd, etot, qm, km, vs, accs = [], [], [], [], [], [], []
        for d in range(2):
            rows = pl.ds(r0[d], c)
            lf = lf_ref[d, rows, :]
            q = qs_ref[d, rows, :]
            k = ks_ref[d, rows, :]
            v = v_refs[d][rows, :]
            xs = _ldot01(mst[d], lf)
            b = xs[0:c]
            btot = jnp.sum(lf, axis=0, keepdims=True)
            qe.append(q * jnp.exp(b))
            kd.append(k * jnp.exp(btot - b))
            etot.append(jnp.exp(btot))
            es = [jnp.exp(xs[(lv + 1) * c:(lv + 2) * c]) for lv in range(nlev)]
            qm.append([q * e for e in es])
            km.append([k * e for e in es])
            vs.append(v)
            b3 = b.reshape(c // SUB, SUB, bw)
            k3 = k.reshape(c // SUB, SUB, bw)
            v3 = v.reshape(c // SUB, SUB, bw)
            acc = [jnp.zeros((c, HG_DK), f32) for _ in range(nh)]
            for sl in range(SUB):
                bs = jnp.broadcast_to(b3[:, sl:sl + 1, :], b3.shape).reshape(c, bw)
                kk = jnp.broadcast_to(k3[:, sl:sl + 1, :], k3.shape).reshape(c, bw)
                vv = jnp.broadcast_to(v3[:, sl:sl + 1, :], v3.shape).reshape(c, bw)
                valid = (tl >= sl) if d == 0 else (tl <= sl)
                p = jnp.where(valid, q * kk * jnp.exp(b - bs), 0.0)
                for h, hs in enumerate(heads):
                    acc[h] = acc[h] + jnp.sum(p[:, hs], axis=-1, keepdims=True) * vv[:, hs]
            accs.append(acc)
        sts = [st_ref[d, h] for d, h in items]
        inter = [_dot_nt(qe[d][:, heads[h]], st) for (d, h), st in zip(items, sts)]
        sc = [jnp.zeros((c, c), f32) for _ in items]
        for lv in range(nlev):
            sc = [s + _dot_nt(qm[d][lv][:, heads[h]], km[d][lv][:, heads[h]]) * pm_ref[d, lv]
                  for (d, h), s in zip(items, sc)]
        outs = [a + accs[d][h] + _dot(s, vs[d][:, heads[h]]) for (d, h), a, s in zip(items, inter, sc)]
        snew = [st * etot[d][:, heads[h]] + _dot_tn(vs[d][:, heads[h]], kd[d][:, heads[h]])
                for (d, h), st in zip(items, sts)]
        for (d, h), o, s in zip(items, outs, snew):
            o_refs[d][pl.ds(r0[d], c), heads[h]] = o
            st_ref[d, h] = s
        return carry

    lax.fori_loop(0, nc, chunk, 0, unroll=True)

    @pl.when(jnp.logical_not(is_lat))
    def _():
        for d, h in items:
            sfin_ref[d, h] = st_ref[d, h].T


def _hgrn_scan(proj, lb, s0t, layer, tables, n_ctx, tb, bw):
    t = proj.shape[0]
    nh = bw // HG_DK
    blk, first, seq = tables
    nb = blk.shape[1]
    mst, pm, nlev = _hgrn_tables_np(HG_C)
    kern = functools.partial(_hgrn_kernel, n_ctx=n_ctx, tb=tb, bw=bw, nlev=nlev)
    tail = (2, nh, HG_DK, HG_DK)
    dir_specs = [pl.BlockSpec((tb, bw), _dir_row(d, col)) for d in range(2) for col in (0, 1, 2 + d)]
    grid_spec = pltpu.PrefetchScalarGridSpec(
        num_scalar_prefetch=3,
        grid=(nb,),
        in_specs=dir_specs + [_const_spec((2, 1, bw)), _state_in_spec(tail, n_ctx, layer),
                              _const_spec(mst.shape), _const_spec(pm.shape)],
        out_specs=[pl.BlockSpec((tb, bw), _dir_row(0, 0)), pl.BlockSpec((tb, bw), _dir_row(1, 0)),
                   _state_out_spec(tail, n_ctx)],
        scratch_shapes=[pltpu.VMEM(tail, f32), pltpu.VMEM((2, tb, bw), f32),
                        pltpu.VMEM((2, tb, bw), f32), pltpu.VMEM((2, tb, bw), f32)],
    )
    return pl.pallas_call(
        kern, grid_spec=grid_spec,
        out_shape=[jax.ShapeDtypeStruct((t, bw), f32), jax.ShapeDtypeStruct((t, bw), f32),
                   jax.ShapeDtypeStruct((n_ctx,) + tail, f32)],
        compiler_params=_cparams(("arbitrary",)),
        name="hgrn_scan",
    )(jnp.asarray(blk), jnp.asarray(first), jnp.asarray(seq), *([proj] * 6), lb, s0t,
      jnp.asarray(mst), jnp.asarray(pm))


def _gdn_prep_kernel(prev_ref, cur_ref, nxt_ref, w_ref, o_ref, xe_ref, *, n_ctx, bps, tb, bw, halo):
    r = pl.program_id(0)
    part = pl.program_id(1)
    is_lat = r >= n_ctx
    pos = jnp.where(is_lat, (r - n_ctx) % bps, 0)
    pv = jnp.where(jnp.logical_and(is_lat, pos > 0), 1.0, 0.0)
    nv = jnp.where(jnp.logical_and(is_lat, pos < bps - 1), 1.0, 0.0)
    lat = jnp.where(is_lat, 1.0, 0.0)
    xe_ref[0:halo, :] = prev_ref[...] * pv
    xe_ref[halo:halo + tb, :] = cur_ref[...]
    xe_ref[halo + tb:halo + tb + halo, :] = nxt_ref[...] * nv
    col = lax.broadcasted_iota(jnp.int32, (tb, LANES), 0) % GRID_W
    m_lo = jnp.where(col >= 1, 1.0, 1.0 - lat)
    m_hi = jnp.where(col <= GRID_W - 2, 1.0, 1.0 - lat)
    for g in range(bw // LANES):
        gs = slice(g * LANES, (g + 1) * LANES)
        acc = None
        for dc in (-1, 0, 1):
            col_sum = None
            for dr in (-1, 0, 1):
                off = halo + GRID_W * dr + dc
                tap = (dr + 1) * CONV_K + (dc + 1)
                w = w_ref[tap:tap + 1, gs]
                if dr != 0:
                    w = w * lat
                term = xe_ref[off:off + tb, gs] * w
                col_sum = term if col_sum is None else col_sum + term
            if dc == -1:
                col_sum = col_sum * m_lo
            elif dc == 1:
                col_sum = col_sum * m_hi
            acc = col_sum if acc is None else acc + col_sum
        y = _silu(acc)
        ss = jnp.sum(y * y, axis=-1, keepdims=True)
        nrm = y * lax.rsqrt(ss + NORM_EPS)
        scale = jnp.where(part == 0, GD_DK ** -0.5, 1.0)
        o_ref[:, gs] = jnp.where(part == 2, y, nrm * scale)


def _gdn_prep(proj, conv_w, n_ctx, bps, tb, bw, qkv_col):
    t = proj.shape[0]
    nb = t // tb
    halo = LANES
    hb = tb // halo
    nhb = t // halo
    kern = functools.partial(_gdn_prep_kernel, n_ctx=n_ctx, bps=bps, tb=tb, bw=bw, halo=halo)
    return pl.pallas_call(
        kern,
        grid=(nb, 3),
        in_specs=[pl.BlockSpec((halo, bw), lambda r, p: (jnp.maximum(r * hb - 1, 0), qkv_col + p)),
                  pl.BlockSpec((tb, bw), lambda r, p: (r, qkv_col + p)),
                  pl.BlockSpec((halo, bw), lambda r, p: (jnp.minimum((r + 1) * hb, nhb - 1), qkv_col + p)),
                  pl.BlockSpec((CONV_K * CONV_K, bw), lambda r, p: (0, p))],
        out_specs=pl.BlockSpec((tb, bw), lambda r, p: (r, p)),
        out_shape=jax.ShapeDtypeStruct((t, 3 * bw), f32),
        scratch_shapes=[pltpu.VMEM((tb + 2 * halo, bw), f32)],
        compiler_params=_cparams(("arbitrary", "arbitrary")),
        name="gdn_prep",
    )(proj, proj, proj, conv_w)


def _gdn_kernel(blk_ref, first_ref, seq_ref, q0_ref, k0_ref, v0_ref, m0_ref, q1_ref, k1_ref, v1_ref, m1_ref,
                alog_ref, dtb_ref, s0_ref, tri_ref, msk_ref, o0_ref, o1_ref, sfin_ref,
                st_ref, lg_ref, be_ref, *, n_ctx, tb, bw, nmerge):
    j = pl.program_id(0)
    c = GD_C
    nc = tb // c
    nh = bw // GD_DK
    is_lat = seq_ref[j] >= n_ctx
    q_refs, k_refs, v_refs = (q0_ref, q1_ref), (k0_ref, k1_ref), (v0_ref, v1_ref)
    m_refs, o_refs = (m0_ref, m1_ref), (o0_ref, o1_ref)

    @pl.when(first_ref[j] == 1)
    def _():
        st_ref[...] = jnp.where(is_lat, s0_ref[...], 0.0)

    for d in range(2):
        ab = m_refs[d][:, MISC_W - LANES:MISC_W]
        spread = lambda j0: jnp.concatenate(
            [jnp.broadcast_to(ab[:, j0 + h:j0 + h + 1], (tb, GD_DK)) for h in range(nh)], axis=1)
        a_in = spread(d * nh)
        b_in = spread(2 * nh + d * nh)
        lg_ref[d] = -jnp.exp(alog_ref[d]) * _softplus(a_in + dtb_ref[d])
        be_ref[d] = _sigmoid(b_in)

    tri = [tri_ref[d].astype(bf16) for d in range(2)]
    heads = [slice(h * GD_DK, (h + 1) * GD_DK) for h in range(nh)]
    items = [(d, h) for d in range(2) for h in range(nh)]
    strict = [msk_ref[d, 0] for d, _ in items]
    incl = [msk_ref[d, 1] for d, _ in items]
    mdiag = [msk_ref[d, 3] for d, _ in items]
    moffs = [[msk_ref[d, 4 + m] for m in range(nmerge)] for d, _ in items]

    def chunk(ci, carry):
        r0 = _chunk_rows(ci, nc, c)
        be, q, k, v, gcum, eg, qdec, kdec, gend = [], [], [], [], [], [], [], [], []
        for d in range(2):
            rows = pl.ds(r0[d], c)
            lg = lg_ref[d, rows, :]
            be.append(be_ref[d, rows, :])
            q.append(q_refs[d][rows, :])
            k.append(k_refs[d][rows, :])
            v.append(v_refs[d][rows, :])
            g = _ldot01(tri[d], lg)
            gtot = jnp.sum(lg, axis=0, keepdims=True)
            gcum.append(g)
            eg.append(jnp.exp(g))
            qdec.append(q[d] * eg[d])
            kdec.append(k[d] * jnp.exp(gtot - g))
            gend.append(jnp.exp(gtot))
        qk = [_dot_nt(jnp.concatenate([q[d][:, heads[h]], k[d][:, heads[h]]], axis=0), k[d][:, heads[h]])
              for d, h in items]
        dec = []
        for d, h in items:
            gc = gcum[d][:, heads[h]]
            dec.append(jnp.exp(jnp.minimum(gc - gc.T, 0.0)))
        ns = [be[d][:, heads[h]] * a[c:2 * c] * e * m for (d, h), a, e, m in zip(items, qk, dec, strict)]
        ys = _tri_inverse_y(ns, mdiag, moffs)
        rhs = [jnp.concatenate([be[d][:, heads[h]] * eg[d][:, heads[h]] * k[d][:, heads[h]],
                                be[d][:, heads[h]] * v[d][:, heads[h]]], axis=1) for d, h in items]
        wu = [r + _dotp(y, r) for y, r in zip(ys, rhs)]
        ps = [a[0:c] * e * m for a, e, m in zip(qk, dec, incl)]
        ss = [st_ref[d, h] for d, h in items]
        ws = [_dot(jnp.concatenate([x[:, 0:GD_DK], qdec[d][:, heads[h]]], axis=0), s)
              for (d, h), x, s in zip(items, wu, ss)]
        vnew = [x[:, GD_DK:2 * GD_DK] - y[0:c] for x, y in zip(wu, ws)]
        outs = [y[c:2 * c] + _dot(p, vn) for y, p, vn in zip(ws, ps, vnew)]
        snew = [gend[d][:, heads[h]] * s + _dot_tn(kdec[d][:, heads[h]], vn)
                for (d, h), s, vn in zip(items, ss, vnew)]
        for (d, h), o, s in zip(items, outs, snew):
            o_refs[d][pl.ds(r0[d], c), heads[h]] = o
            st_ref[d, h] = s
        return carry

    lax.fori_loop(0, nc, chunk, 0, unroll=True)

    @pl.when(jnp.logical_not(is_lat))
    def _():
        sfin_ref[...] = st_ref[...]


def _gdn_scan(qkv, proj, misc_col, alog, dtb, s0, layer, tables, n_ctx, tb, bw):
    t = proj.shape[0]
    nh = bw // GD_DK
    blk, first, seq = tables
    nb = blk.shape[1]
    c = GD_C
    tri = np.stack([_tri_np(c, False), _tri_np(c, True)]).astype(np.float32)
    inv = _inverse_masks_np(c, c)
    msk = np.stack([np.stack(list(_causal_np(c, c, rev)) + inv) for rev in (False, True)]).astype(np.float32)
    nmerge = len(inv) - 2
    kern = functools.partial(_gdn_kernel, n_ctx=n_ctx, tb=tb, bw=bw, nmerge=nmerge)
    tail = (2, nh, GD_DK, GD_DK)
    dir_specs = []
    for d in range(2):
        dir_specs += [pl.BlockSpec((tb, bw), _dir_row(d, col)) for col in range(3)]
        dir_specs += [pl.BlockSpec((tb, MISC_W), _dir_row(d, misc_col))]
    grid_spec = pltpu.PrefetchScalarGridSpec(
        num_scalar_prefetch=3,
        grid=(nb,),
        in_specs=dir_specs + [_const_spec((2, 1, bw)), _const_spec((2, 1, bw)), _state_in_spec(tail, n_ctx, layer),
                              _const_spec(tri.shape), _const_spec(msk.shape)],
        out_specs=[pl.BlockSpec((tb, bw), _dir_row(0, 0)), pl.BlockSpec((tb, bw), _dir_row(1, 0)),
                   _state_out_spec(tail, n_ctx)],
        scratch_shapes=[pltpu.VMEM(tail, f32), pltpu.VMEM((2, tb, bw), f32), pltpu.VMEM((2, tb, bw), f32)],
    )
    return pl.pallas_call(
        kern, grid_spec=grid_spec,
        out_shape=[jax.ShapeDtypeStruct((t, bw), f32), jax.ShapeDtypeStruct((t, bw), f32),
                   jax.ShapeDtypeStruct((n_ctx,) + tail, f32)],
        compiler_params=_cparams(("arbitrary",)),
        name="gdn_scan",
    )(jnp.asarray(blk), jnp.asarray(first), jnp.asarray(seq), qkv, qkv, qkv, proj, qkv, qkv, qkv, proj,
      alog, dtb, s0, jnp.asarray(tri), jnp.asarray(msk))


def _rwkv_prep_kernel(r_ref, k_ref, v_ref, misc_ref, w0_ref, w2_ref, a0_ref, a2_ref, g2_ref, kkp_ref,
                      ka_ref, rk_ref, b64_ref, kk_ref, lw_ref, icl_ref, bonus_ref, og_ref):
    rr = r_ref[...]
    rk = k_ref[...]
    wl = jnp.tanh(misc_ref[:, 0:LANES])
    al = misc_ref[:, LANES:2 * LANES]
    gl = _sigmoid(misc_ref[:, 2 * LANES:3 * LANES])
    ka = ka_ref[...]
    bon = jnp.zeros_like(rr)
    for d in range(2):
        logw = -_softplus(-(w0_ref[d:d + 1, :] + _dot(wl, w2_ref[d]))) - 0.5
        lw_ref[d] = -jnp.exp(logw)
        icl = _sigmoid(a0_ref[d:d + 1, :] + _dot(al, a2_ref[d]))
        icl_ref[d] = icl
        bon = bon + rr * (rk * (1.0 + (icl - 1.0) * ka)) * rk_ref[...]
    og_ref[...] = _dot(gl, g2_ref[...])
    b64 = b64_ref[...]

    def head_sums(x):
        return jnp.concatenate([_rdot01(x[:, g * LANES:(g + 1) * LANES], b64)
                                for g in range(x.shape[1] // LANES)], axis=1)

    kx = rk * kkp_ref[...]
    kk_ref[...] = kx * lax.rsqrt(head_sums(kx * kx) + NORM_EPS)
    bonus_ref[...] = head_sums(bon) * v_ref[...]


def _rwkv_prep(proj, rkv_col, misc_col, w0, w2p, a0, a2p, g2p, kkp, ka, rkp, b64, tb, bw):
    t = proj.shape[0]
    full = lambda shape: pl.BlockSpec(shape, lambda r: (0,) * len(shape))
    return pl.pallas_call(
        _rwkv_prep_kernel,
        grid=(t // tb,),
        in_specs=[pl.BlockSpec((tb, bw), lambda r: (r, rkv_col)),
                  pl.BlockSpec((tb, bw), lambda r: (r, rkv_col + 1)),
                  pl.BlockSpec((tb, bw), lambda r: (r, rkv_col + 2)),
                  pl.BlockSpec((tb, MISC_W), lambda r: (r, misc_col)),
                  full((2, bw)), full((2, LANES, bw)), full((2, bw)), full((2, LANES, bw)),
                  full((LANES, bw)), full((1, bw)), full((1, bw)), full((1, bw)), full((LANES, LANES))],
        out_specs=[pl.BlockSpec((tb, bw), lambda r: (r, 0)),
                   pl.BlockSpec((2, tb, bw), lambda r: (0, r, 0)),
                   pl.BlockSpec((2, tb, bw), lambda r: (0, r, 0)),
                   pl.BlockSpec((tb, bw), lambda r: (r, 0)),
                   pl.BlockSpec((tb, bw), lambda r: (r, 0))],
        out_shape=[jax.ShapeDtypeStruct((t, bw), f32), jax.ShapeDtypeStruct((2, t, bw), f32),
                   jax.ShapeDtypeStruct((2, t, bw), f32), jax.ShapeDtypeStruct((t, bw), f32),
                   jax.ShapeDtypeStruct((t, bw), f32)],
        compiler_params=_cparams(("arbitrary",)),
        name="rwkv_prep",
    )(proj, proj, proj, proj, w0, w2p, a0, a2p, g2p, kkp, ka, rkp, b64)


def _rwkv_kernel(blk_ref, first_ref, seq_ref, r0_ref, k0_ref, v0_ref, kk0_ref, lw0_ref, icl0_ref,
                 r1_ref, k1_ref, v1_ref, kk1_ref, lw1_ref, icl1_ref, ka_ref, s0_ref, tri_ref, msk_ref,
                 o0_ref, o1_ref, sfin_ref, st_ref, *, n_ctx, tb, bw, nmerge):
    j = pl.program_id(0)
    c = RW_C
    nc = tb // c
    n2c = 2 * c
    npair = bw // LANES
    is_lat = seq_ref[j] >= n_ctx
    r_refs, k_refs, v_refs = (r0_ref, r1_ref), (k0_ref, k1_ref), (v0_ref, v1_ref)
    kk_refs, lw_refs, icl_refs, o_refs = (kk0_ref, kk1_ref), (lw0_ref, lw1_ref), (icl0_ref, icl1_ref), (o0_ref, o1_ref)

    @pl.when(first_ref[j] == 1)
    def _():
        st_ref[...] = jnp.where(is_lat, s0_ref[...], 0.0)

    tri = [tri_ref[d].astype(bf16) for d in range(2)]
    pairs = [slice(p * LANES, (p + 1) * LANES) for p in range(npair)]
    items = [(d, p) for d in range(2) for p in range(npair)]
    strict = [msk_ref[d, 0] for d, _ in items]
    incl2 = [jnp.concatenate([msk_ref[d, 1], msk_ref[d, 1]], axis=1) for d in range(2)]
    mdiag = [msk_ref[d, 3] for d, _ in items]
    moffs = [[msk_ref[d, 4 + m] for m in range(nmerge)] for d, _ in items]
    lane = lax.broadcasted_iota(jnp.int32, (c, LANES), 1)
    m_even = jnp.where(lane < RW_DK, 1.0, 0.0)
    m_odd = 1.0 - m_even
    ka = ka_ref[...]

    def stack2(x):
        return jnp.concatenate([x * m_even, x * m_odd], axis=0)

    def chunk(ci, carry):
        r0 = _chunk_rows(ci, nc, c)
        at, kt, bt, rt, rv, kdec, bdec, etot = [], [], [], [], [], [], [], []
        for d in range(2):
            rows = pl.ds(r0[d], c)
            lw = lw_refs[d][rows, :]
            icl = icl_refs[d][rows, :]
            kk = kk_refs[d][rows, :]
            cum = _ldot01(tri[d], lw)
            ctot = jnp.sum(lw, axis=0, keepdims=True)
            kdir = k_refs[d][rows, :] * (1.0 + (icl - 1.0) * ka)
            bvec = kk * icl
            einv = jnp.exp(-cum)
            edec = jnp.exp(ctot - cum)
            at.append((-kk) * jnp.exp(cum - lw))
            kt.append(kdir * einv)
            bt.append(bvec * einv)
            rt.append(r_refs[d][rows, :] * jnp.exp(cum))
            rv.append(v_refs[d][rows, :])
            kdec.append(kdir * edec)
            bdec.append(bvec * edec)
            etot.append(jnp.exp(ctot))
        a2 = [stack2(at[d][:, pairs[p]]) for d, p in items]
        r2 = [stack2(rt[d][:, pairs[p]]) for d, p in items]
        v2 = [stack2(rv[d][:, pairs[p]]) for d, p in items]
        kb2 = [jnp.concatenate([stack2(kt[d][:, pairs[p]]), stack2(bt[d][:, pairs[p]])], axis=0) for d, p in items]
        kbd2 = [jnp.concatenate([stack2(kdec[d][:, pairs[p]]), stack2(bdec[d][:, pairs[p]])], axis=0)
                for d, p in items]
        big = [_dot_nt(jnp.concatenate([a, r], axis=0), kb) for a, r, kb in zip(a2, r2, kb2)]
        aak = [x[0:n2c, 0:n2c] * m for x, m in zip(big, strict)]
        ns = [-(x[0:n2c, n2c:2 * n2c] * m) for x, m in zip(big, strict)]
        arkb = [x[n2c:2 * n2c] * incl2[d] for x, (d, _) in zip(big, items)]
        ys = _tri_inverse_y(ns, mdiag, moffs)
        rhs = [jnp.concatenate([a, _dot(m, v)], axis=1) for a, m, v in zip(a2, aak, v2)]
        tat = [x + _dotp(y, x) for y, x in zip(ys, rhs)]
        sts = [st_ref[d, p] for d, p in items]
        ur = [_dot_nt(jnp.concatenate([x[:, 0:LANES], r], axis=0), st) for x, r, st in zip(tat, r2, sts)]
        u2 = [y[0:n2c] + x[:, LANES:2 * LANES] for y, x in zip(ur, tat)]
        vu = [jnp.concatenate([v, u], axis=0) for v, u in zip(v2, u2)]
        y2 = [y[n2c:2 * n2c] + _dot(m, x) for y, m, x in zip(ur, arkb, vu)]
        snew = [st * etot[d][:, pairs[p]] + _dot_tn(x, kbd) for (d, p), st, x, kbd in zip(items, sts, vu, kbd2)]
        for (d, p), y, s in zip(items, y2, snew):
            o_refs[d][pl.ds(r0[d], c), pairs[p]] = y[0:c] + y[c:n2c]
            st_ref[d, p] = s
        return carry

    lax.fori_loop(0, nc, chunk, 0, unroll=True)

    @pl.when(jnp.logical_not(is_lat))
    def _():
        for d, p in items:
            st = st_ref[d, p]
            sfin_ref[d, 2 * p] = st[0:RW_DK, 0:RW_DK]
            sfin_ref[d, 2 * p + 1] = pltpu.roll(st, RW_DK, axis=1)[RW_DK:LANES, 0:RW_DK]


def _rwkv_scan(proj, rkv_col, kk, lw, icl, ka, s0p, layer, tables, n_ctx, tb, bw):
    t = proj.shape[0]
    npair = bw // LANES
    blk, first, seq = tables
    nb = blk.shape[1]
    c = RW_C
    n = 2 * c
    tri = np.stack([_tri_np(c, False), _tri_np(c, True)]).astype(np.float32)
    inv = _inverse_masks_np(n, c)
    msk = np.stack([np.stack(list(_causal_np(n, c, rev)) + inv) for rev in (False, True)]).astype(np.float32)
    nmerge = len(inv) - 2
    kern = functools.partial(_rwkv_kernel, n_ctx=n_ctx, tb=tb, bw=bw, nmerge=nmerge)
    tail = (2, npair, LANES, LANES)
    head_tail = (2, 2 * npair, RW_DK, RW_DK)
    dir_specs, dir_args = [], []
    for d in range(2):
        dir_specs += [pl.BlockSpec((tb, bw), _dir_row(d, rkv_col + i)) for i in range(3)]
        dir_specs += [pl.BlockSpec((tb, bw), _dir_row(d, 0)),
                      pl.BlockSpec((None, tb, bw), _dir_slab(d)), pl.BlockSpec((None, tb, bw), _dir_slab(d))]
        dir_args += [proj, proj, proj, kk, lw, icl]
    grid_spec = pltpu.PrefetchScalarGridSpec(
        num_scalar_prefetch=3,
        grid=(nb,),
        in_specs=dir_specs + [_const_spec((1, bw)), _state_in_spec(tail, n_ctx, layer),
                              _const_spec(tri.shape), _const_spec(msk.shape)],
        out_specs=[pl.BlockSpec((tb, bw), _dir_row(0, 0)), pl.BlockSpec((tb, bw), _dir_row(1, 0)),
                   _state_out_spec(head_tail, n_ctx)],
        scratch_shapes=[pltpu.VMEM(tail, f32)],
    )
    return pl.pallas_call(
        kern, grid_spec=grid_spec,
        out_shape=[jax.ShapeDtypeStruct((t, bw), f32), jax.ShapeDtypeStruct((t, bw), f32),
                   jax.ShapeDtypeStruct((n_ctx,) + head_tail, f32)],
        compiler_params=_cparams(("arbitrary",)),
        name="rwkv_scan",
    )(jnp.asarray(blk), jnp.asarray(first), jnp.asarray(seq), *dir_args, ka, s0p,
      jnp.asarray(tri), jnp.asarray(msk))


def _merge_kernel(x_ref, mod_ref, ohg0_ref, ohg1_ref, ogd0_ref, ogd1_ref, orw0_ref, orw1_ref, hgg_ref, gdz_ref,
                  bonus_ref, og_ref, g0_ref, g1_ref, g2_ref, hgn_ref, gdn_ref, lng_ref, lnb_ref, wb_ref, wo_ref,
                  o_ref, *, bw):
    def head_mean(x, width):
        outs = []
        for g in range(bw // LANES):
            xg = x[:, g * LANES:(g + 1) * LANES]
            if width == LANES:
                outs.append(jnp.broadcast_to(jnp.mean(xg, axis=-1, keepdims=True), xg.shape))
            else:
                lane = lax.broadcasted_iota(jnp.int32, xg.shape, 1)
                lo = lane < width
                s_lo = jnp.sum(jnp.where(lo, xg, 0.0), axis=-1, keepdims=True)
                s_hi = jnp.sum(jnp.where(lo, 0.0, xg), axis=-1, keepdims=True)
                outs.append(jnp.where(lo, s_lo, s_hi) * (1.0 / width))
        return jnp.concatenate(outs, axis=-1)

    ohg = ohg0_ref[...] + ohg1_ref[...]
    ohg = ohg * lax.rsqrt(head_mean(ohg * ohg, HG_DK) + NORM_EPS) * hgn_ref[...]
    ohg = ohg * _silu(hgg_ref[...])
    ogd = ogd0_ref[...] + ogd1_ref[...]
    ogd = ogd * lax.rsqrt(head_mean(ogd * ogd, GD_DK) + NORM_EPS) * gdn_ref[...]
    ogd = ogd * _silu(gdz_ref[...])
    orw = orw0_ref[...] + orw1_ref[...]
    mu = head_mean(orw, RW_DK)
    cen = orw - mu
    var = head_mean(cen * cen, RW_DK)
    orw = cen * lax.rsqrt(var + RW_GN_EPS) * lng_ref[...] + lnb_ref[...]
    orw = (orw + bonus_ref[...]) * og_ref[...]
    merged = (_sigmoid(g0_ref[...]) * _dot(ohg, wb_ref[0])
              + _sigmoid(g1_ref[...]) * _dot(ogd, wb_ref[1])
              + _sigmoid(g2_ref[...]) * _dot(orw, wb_ref[2]))
    o_ref[...] = x_ref[...] + mod_ref[5:6, :] * _dot(merged, wo_ref[...])


def _merge(x, mod, proj, branch_outs, bonus, og, hgn, gdn, lng, lnb, wb, wo, layer, cond_of_tile, tm, bw,
           hgg_col, gdz_col, gate_col):
    t, d = x.shape
    row = lambda col: (lambda i: (i, col))
    act = pl.BlockSpec((tm, bw), row(0))
    vec = pl.BlockSpec((1, bw), lambda i: (0, 0))
    once = pl.Buffered(1)
    return pl.pallas_call(
        functools.partial(_merge_kernel, bw=bw),
        grid=(t // tm,),
        in_specs=[pl.BlockSpec((tm, d), row(0)),
                  pl.BlockSpec((None, None, N_MOD, d), lambda i: (layer, cond_of_tile(i), 0, 0)),
                  act, act, act, act, act, act,
                  pl.BlockSpec((tm, bw), row(hgg_col)),
                  pl.BlockSpec((tm, bw), row(gdz_col)),
                  act, act,
                  pl.BlockSpec((tm, d), row(gate_col)),
                  pl.BlockSpec((tm, d), row(gate_col + 1)),
                  pl.BlockSpec((tm, d), row(gate_col + 2)),
                  vec, vec, vec, vec,
                  pl.BlockSpec((None, 3, bw, d), lambda i: (layer, 0, 0, 0), pipeline_mode=once),
                  pl.BlockSpec((None, d, d), lambda i: (layer, 0, 0), pipeline_mode=once)],
        out_specs=pl.BlockSpec((tm, d), row(0)),
        out_shape=jax.ShapeDtypeStruct((t, d), f32),
        compiler_params=_cparams(("arbitrary",)),
        name="merge",
    )(x, mod, *branch_outs, proj, proj, bonus, og, proj, proj, proj, hgn, gdn, lng, lnb, wb, wo)


def _final_norm_kernel(x_ref, g_ref, o_ref):
    x = x_ref[...]
    ms = jnp.mean(x * x, axis=-1, keepdims=True)
    o_ref[...] = x * lax.rsqrt(ms + NORM_EPS) * g_ref[...]


def _final_norm(x, g, tm, row0, rows):
    d = x.shape[1]
    first = row0 // tm
    return pl.pallas_call(
        _final_norm_kernel,
        grid=(rows // tm,),
        in_specs=[pl.BlockSpec((tm, d), lambda i: (first + i, 0)), pl.BlockSpec((1, d), lambda i: (0, 0))],
        out_specs=pl.BlockSpec((tm, d), lambda i: (i, 0)),
        out_shape=jax.ShapeDtypeStruct((rows, d), f32),
        compiler_params=_cparams(("arbitrary",)),
        name="final_norm",
    )(x, g.reshape(1, d))


def _pad_rows(w, rows):
    return jnp.concatenate([w, jnp.zeros((rows - w.shape[0],) + w.shape[1:], w.dtype)], axis=0)


def kernel(x_prompt, x_sample, state_hgrn, state_gdn, state_rwkv, c, c_ctx, mod_w, mod_b, norm_g, ffn_up,
           ffn_down, mix_in, hgrn_lb, hgrn_norm, gdn_conv, gdn_a_log, gdn_dt_bias, gdn_norm, rwkv_w0, rwkv_w2,
           rwkv_a0, rwkv_a2, rwkv_g2, rwkv_kk, rwkv_ka, rwkv_rk, rwkv_ln_g, rwkv_ln_b, mix_branch, mix_out,
           final_norm):
    bc, lc, d = x_prompt.shape
    bs, ls, _ = x_sample.shape
    depth = mod_w.shape[0]
    bw = d // 4
    tb = lc
    assert tb % GD_C == 0 and ls % tb == 0 and bw % LANES == 0 and tb % GRID_W == 0
    assert bs + 1 <= N_COND_PAD
    n_ctx = bc
    bps = ls // tb
    tc = bc * lc
    t = tc + bs * ls
    gd_h, rw_h = bw // GD_DK, bw // RW_DK
    lw_w, la_w, lg_w = rwkv_w2.shape[2], rwkv_a2.shape[2], rwkv_g2.shape[1]
    assert max(2 * lw_w, 2 * la_w, lg_w, 4 * gd_h) <= LANES

    def token_tile(cap):
        rows = tb
        while rows * 2 <= cap and tc % (rows * 2) == 0 and ls % (rows * 2) == 0:
            rows *= 2
        return rows

    tm, tp = token_tile(TM_FFN), token_tile(TM_PROJ)
    cond_of_tile = lambda tile_rows: (lambda i: jnp.where(i * tile_rows < tc, 0, 1 + (i * tile_rows - tc) // ls))

    x = (x_prompt.reshape(tc, d), x_sample.reshape(bs * ls, d))
    cond = jnp.concatenate([c_ctx[None, :], c, jnp.zeros((N_COND_PAD - 1 - bs, d), f32)], axis=0)
    mod_all = _modulation(cond, mod_w, mod_b).reshape(depth, N_COND_PAD, N_MOD, d)

    lb_p = jax.nn.softmax(hgrn_lb.astype(f32), axis=0)
    lb_all = jnp.cumsum(lb_p, axis=0) - lb_p[0:1]

    o_gda = 9 * bw
    o_rw = 9 * bw + 4 * gd_h
    o_lo = o_rw + 3 * bw
    o_mg = o_lo + 2 * lw_w + 2 * la_w + lg_w
    base = 24 * bw
    pieces = ((0, 9 * bw, 0), (o_rw, 3 * bw, 9 * bw), (o_mg, 3 * d, 12 * bw),
              (o_lo, 2 * lw_w, base), (o_lo + 2 * lw_w, 2 * la_w, base + LANES),
              (o_lo + 2 * lw_w + 2 * la_w, lg_w, base + 2 * LANES), (o_gda, 4 * gd_h, base + 3 * LANES))
    w_in = _relayout_mix_in(mix_in, pieces, base + MISC_W)
    col_gdq, col_gdz, col_rw, col_gate = 5, 8, 9, 3
    col_misc = (24 * bw) // MISC_W
    assert (24 * bw) % MISC_W == 0

    up16 = ffn_up.astype(bf16)
    down16 = ffn_down.astype(bf16)
    wb16 = mix_branch.astype(bf16)
    wo16 = mix_out.astype(bf16)

    b64 = jnp.asarray(np.kron(np.eye(LANES // RW_DK), np.ones((RW_DK, RW_DK))).astype(np.float32))

    tables = _block_tables_np(n_ctx, bs, bps)

    s0_hg = jnp.swapaxes(state_hgrn, -1, -2)
    s0_gd = state_gdn
    eye2 = jnp.eye(2, dtype=f32)
    s0_rw = state_rwkv.reshape(bs, depth, 2, rw_h // 2, 2, RW_DK, RW_DK)
    s0_rw = jnp.einsum('bldpeij,ef->bldpeifj', s0_rw, eye2).reshape(bs, depth, 2, rw_h // 2, LANES, LANES)

    new_hg, new_gd, new_rw = [], [], []
    for l in range(depth):
        mod = mod_all
        x = _ffn(x, mod, norm_g, up16, down16, l, 0, cond_of_tile(tm), tm)
        proj = _mixin(x, mod, norm_g, w_in, l, cond_of_tile(tp), tp)

        ohg0, ohg1, s_hg = _hgrn_scan(proj, lb_all[l].reshape(2, 1, bw), s0_hg, l, tables, n_ctx, tb, bw)

        conv_w = gdn_conv[l].reshape(CONV_K * CONV_K, 3 * bw)
        qkv = _gdn_prep(proj, conv_w, n_ctx, bps, tb, bw, col_gdq)
        alog = jnp.repeat(gdn_a_log[l], GD_DK, axis=-1).reshape(2, 1, bw)
        dtb = jnp.repeat(gdn_dt_bias[l], GD_DK, axis=-1).reshape(2, 1, bw)

        w2p = jnp.stack([_pad_rows(jnp.concatenate([jnp.zeros((dd * lw_w, bw), f32), rwkv_w2[l, dd]], 0), LANES)
                         for dd in range(2)])
        a2p = jnp.stack([_pad_rows(jnp.concatenate([jnp.zeros((dd * la_w, bw), f32), rwkv_a2[l, dd]], 0), LANES)
                         for dd in range(2)])
        g2p = _pad_rows(rwkv_g2[l], LANES)
        ka = rwkv_ka[l].reshape(1, bw)
        kk, lw, icl, bonus, og = _rwkv_prep(proj, col_rw, col_misc, rwkv_w0[l], w2p, rwkv_a0[l], a2p, g2p,
                                            rwkv_kk[l].reshape(1, bw), ka, rwkv_rk[l].reshape(1, bw), b64,
                                            tb, bw)
        ogd0, ogd1, s_gd = _gdn_scan(qkv, proj, col_misc, alog, dtb, s0_gd, l, tables, n_ctx, tb, bw)
        orw0, orw1, s_rw = _rwkv_scan(proj, col_rw, kk, lw, icl, ka, s0_rw, l, tables, n_ctx, tb, bw)

        x = _merge(x, mod, proj, (ohg0, ohg1, ogd0, ogd1, orw0, orw1), bonus, og, hgrn_norm[l].reshape(1, bw),
                   jnp.tile(gdn_norm[l], gd_h).reshape(1, bw), rwkv_ln_g[l].reshape(1, bw),
                   rwkv_ln_b[l].reshape(1, bw), wb16, wo16, l, cond_of_tile(tb), tb, bw,
                   4, col_gdz, col_gate)
        x = _ffn(x, mod, norm_g, up16, down16, l, 2, cond_of_tile(tm), tm)

        new_hg.append(s_hg)
        new_gd.append(s_gd)
        new_rw.append(s_rw)

    y_ctx = _final_norm(x, final_norm, tp, 0, tc)
    y_lat = _final_norm(x, final_norm, tp, tc, bs * ls)
    return (y_ctx.reshape(bc, lc, d), y_lat.reshape(bs, ls, d),
            jnp.stack(new_hg, axis=1), jnp.stack(new_gd, axis=1), jnp.stack(new_rw, axis=1))
```
